```python
import math
import jax, jax.numpy as jnp
from jax import lax
import numpy as np

D_MODEL = 1024
BATCH = 16
SEQ = 2048
DEPTH = 1
DEC_BATCH = 128
DEC_SEQ = 8
PAST_LEN = 16384
PAGE_SIZE = 128

ATTN_HEADS = 8
ATTN_KV_HEADS = 2
GROUP = ATTN_HEADS // ATTN_KV_HEADS
HEAD_DIM = 64
ATTN_WIDTH = ATTN_HEADS * HEAD_DIM
KV_WIDTH = ATTN_KV_HEADS * HEAD_DIM
WINDOW = 128
ATTN_SCALE = HEAD_DIM ** -0.5
HGRN_HEADS = 4
HGRN_DK = 128
HGRN_DV = (D_MODEL - ATTN_WIDTH) // HGRN_HEADS
HGRN_KW = HGRN_HEADS * HGRN_DK
HGRN_VW = HGRN_HEADS * HGRN_DV
CHUNK = 64
D_MIX = ATTN_WIDTH + HGRN_VW
D_IN = ATTN_WIDTH + 2 * KV_WIDTH + 2 * HGRN_KW + 2 * HGRN_VW
D_FF = ((8 * D_MODEL // 3 + 127) // 128) * 128
ALPHA = (2.0 * DEPTH) ** 0.25
BETA = (8.0 * DEPTH) ** -0.25
LN_EPS = 1e-5
RMS_EPS = 1e-6

kernel_name = "hymba_hgrn2_swa_sink_macaron_deepnorm_step"


def _layer_norm(x, g, b):
    xf = x.astype(jnp.float32)
    mu = jnp.mean(xf, axis=-1, keepdims=True)
    var = jnp.mean(jnp.square(xf - mu), axis=-1, keepdims=True)
    y = (xf - mu) * lax.rsqrt(var + LN_EPS) * g.astype(jnp.float32) + b.astype(jnp.float32)
    return y.astype(x.dtype)


def _rms_norm(x, g):
    xf = x.astype(jnp.float32)
    return xf * lax.rsqrt(jnp.mean(jnp.square(xf), axis=-1, keepdims=True) + RMS_EPS) * g.astype(jnp.float32)


def _swiglu(x, w13, w2):
    gate, up = jnp.split(x @ w13, 2, axis=-1)
    return (jax.nn.silu(gate) * up) @ w2


def _alibi_slopes():
    h = jnp.arange(1, ATTN_HEADS + 1, dtype=jnp.float32)
    return jnp.exp2(-8.0 * h / ATTN_HEADS).reshape(ATTN_KV_HEADS, GROUP)


def _sink_attend(s, sinks, v, eq):
    sk = sinks.astype(jnp.float32)[:, :, None]
    m = jnp.maximum(jnp.max(s, axis=-1), sk)
    p = jnp.exp(s - m[..., None])
    denom = jnp.sum(p, axis=-1) + jnp.exp(sk - m)
    p = p / denom[..., None]
    return jnp.einsum(eq, p, v.astype(jnp.float32))


def _swa_prompt(q, k, v, sinks):
    B, L = q.shape[:2]
    nb = L // WINDOW
    qb = q.reshape(B, nb, WINDOW, ATTN_KV_HEADS, GROUP, HEAD_DIM)

    def band(a):
        ap = jnp.pad(a, ((0, 0), (WINDOW, 0), (0, 0), (0, 0)))
        ap = ap.reshape(B, nb + 1, WINDOW, ATTN_KV_HEADS, HEAD_DIM)
        return jnp.concatenate([ap[:, :-1], ap[:, 1:]], axis=2)

    kb, vb = band(k), band(v)
    s = jnp.einsum('bnqhgd,bnkhd->bnhgqk', qb, kb, preferred_element_type=jnp.float32) * ATTN_SCALE
    qi = jnp.arange(WINDOW)
    kc = jnp.arange(2 * WINDOW)
    dist = qi[:, None] + WINDOW - kc[None, :]
    key_pos = (jnp.arange(nb)[:, None] - 1) * WINDOW + kc[None, :]
    mask = ((dist >= 0) & (dist < WINDOW))[None] & (key_pos >= 0)[:, None, :]
    bias = -_alibi_slopes()[:, :, None, None] * dist.astype(jnp.float32)
    s = jnp.where(mask[None, :, None, None], s + bias, -jnp.inf)
    o = _sink_attend(s, sinks, vb, 'bnhgqk,bnkhd->bnqhgd')
    return o.reshape(B, L, ATTN_WIDTH), k[:, -WINDOW:], v[:, -WINDOW:]


def _swa_sample(q, k, v, buf_k, buf_v, sinks):
    B, T = q.shape[:2]
    wb = buf_k.shape[1]
    qg = q.reshape(B, T, ATTN_KV_HEADS, GROUP, HEAD_DIM)
    kk = jnp.concatenate([buf_k.astype(k.dtype), k], axis=1)
    vv = jnp.concatenate([buf_v.astype(v.dtype), v], axis=1)
    s = jnp.einsum('bqhgd,bkhd->bhgqk', qg, kk, preferred_element_type=jnp.float32) * ATTN_SCALE
    dist = wb + jnp.arange(T)[:, None] - jnp.arange(wb + T)[None, :]
    mask = (dist >= 0) & (dist < WINDOW)
    bias = -_alibi_slopes()[:, :, None, None] * dist.astype(jnp.float32)
    s = jnp.where(mask, s + bias, -jnp.inf)
    o = _sink_attend(s, sinks, vv, 'bhgqk,bkhd->bqhgd')
    return o.reshape(B, T, ATTN_WIDTH), kk[:, -wb:], vv[:, -wb:]


def _hgrn2(q, k, v, logf, S0):
    B, L = q.shape[:2]
    C = CHUNK if L % CHUNK == 0 else L
    nc = L // C

    def chunks(a):
        return a.astype(jnp.float32).reshape(B, nc, C, *a.shape[2:]).swapaxes(0, 1)

    causal = jnp.tril(jnp.ones((C, C), dtype=bool))[None, :, :, None, None]

    def step(S, xs):
        qc, kc, vc, gc = xs
        b = jnp.cumsum(gc, axis=1)
        o = jnp.einsum('bthk,bhkv->bthv', qc * jnp.exp(b), S)
        dec = jnp.exp(jnp.where(causal, b[:, :, None] - b[:, None, :], -jnp.inf))
        a = jnp.einsum('bthk,bshk,btshk->bhts', qc, kc, dec)
        o = o + jnp.einsum('bhts,bshv->bthv', a, vc)
        b_last = b[:, -1]
        S = jnp.exp(b_last)[..., None] * S + jnp.einsum(
            'bshk,bshv->bhkv', kc * jnp.exp(b_last[:, None] - b), vc)
        return S, o

    S, o = lax.scan(step, S0.astype(jnp.float32), (chunks(q), chunks(k), chunks(v), chunks(logf)))
    return o.swapaxes(0, 1).reshape(B, L, HGRN_HEADS, HGRN_DV), S


def _mixer(h, S0, buf_k, buf_v, lb, w_in, attn_sink, attn_norm_g, hgrn_norm_g, w_out):
    B, L, _ = h.shape
    offs = [ATTN_WIDTH, ATTN_WIDTH + KV_WIDTH, ATTN_WIDTH + 2 * KV_WIDTH,
            ATTN_WIDTH + 2 * KV_WIDTH + HGRN_KW, ATTN_WIDTH + 2 * KV_WIDTH + 2 * HGRN_KW,
            ATTN_WIDTH + 2 * KV_WIDTH + 2 * HGRN_KW + HGRN_VW]
    qa, ka, va, hq, hf, hi, hg = jnp.split(h @ w_in, offs, axis=-1)
    qa = qa.reshape(B, L, ATTN_HEADS, HEAD_DIM)
    ka = ka.reshape(B, L, ATTN_KV_HEADS, HEAD_DIM)
    va = va.reshape(B, L, ATTN_KV_HEADS, HEAD_DIM)
    sinks = attn_sink.reshape(ATTN_KV_HEADS, GROUP)
    if buf_k is None:
        oa, nk, nv = _swa_prompt(qa, ka, va, sinks)
    else:
        oa, nk, nv = _swa_sample(qa, ka, va, buf_k, buf_v, sinks)
    oa = _rms_norm(oa, attn_norm_g)
    fq = jax.nn.silu(hq.astype(jnp.float32)).reshape(B, L, HGRN_HEADS, HGRN_DK)
    f = lb + (1.0 - lb) * jax.nn.sigmoid(hf.astype(jnp.float32))
    logf = jnp.log(f).reshape(B, L, HGRN_HEADS, HGRN_DK)
    kin = (1.0 - f).reshape(B, L, HGRN_HEADS, HGRN_DK)
    vin = hi.reshape(B, L, HGRN_HEADS, HGRN_DV)
    oh, S = _hgrn2(fq, kin, vin, logf, S0)
    oh = _rms_norm(oh, hgrn_norm_g.reshape(HGRN_HEADS, HGRN_DV)).reshape(B, L, HGRN_VW)
    oh = oh * jax.nn.silu(hg.astype(jnp.float32))
    y = jnp.concatenate([oa, oh], axis=-1).astype(h.dtype) @ w_out
    return y, S, nk, nv


def _decoder_layer(x, S0, buf_k, buf_v, lb, ln1_g, ln1_b, ffn1_w13, ffn1_w2, w_in, attn_sink,
                   attn_norm_g, hgrn_norm_g, w_out, ln2_g, ln2_b, ffn2_w13, ffn2_w2, ln3_g, ln3_b):
    x = _layer_norm(ALPHA * x + 0.5 * _swiglu(x, ffn1_w13, ffn1_w2), ln1_g, ln1_b)
    y, S, nk, nv = _mixer(x, S0, buf_k, buf_v, lb, w_in, attn_sink, attn_norm_g, hgrn_norm_g, w_out)
    x = _layer_norm(ALPHA * x + y, ln2_g, ln2_b)
    x = _layer_norm(ALPHA * x + 0.5 * _swiglu(x, ffn2_w13, ffn2_w2), ln3_g, ln3_b)
    return x, S, nk, nv


def setup_inputs(seed: int = 0) -> dict:
    key = jax.random.key(seed)
    ks = jax.random.split(key, 24)
    D = D_MODEL
    wb = min(WINDOW, PAST_LEN)

    def nrm(k, shape, scale):
        return jax.random.normal(k, shape, jnp.float32) * scale

    col_scale = jnp.concatenate([
        jnp.ones((ATTN_WIDTH + KV_WIDTH,), jnp.float32),
        jnp.full((KV_WIDTH,), BETA, jnp.float32),
        jnp.ones((2 * HGRN_KW,), jnp.float32),
        jnp.full((HGRN_VW,), BETA, jnp.float32),
        jnp.ones((HGRN_VW,), jnp.float32)])
    return {
        "x_prompt": nrm(ks[0], (BATCH, SEQ, D), 1.0),
        "x_sample": nrm(ks[1], (DEC_BATCH, DEC_SEQ, D), 1.0),
        "state_hgrn": nrm(ks[2], (DEPTH, DEC_BATCH, HGRN_HEADS, HGRN_DK, HGRN_DV), 0.3),
        "cache_win_k": nrm(ks[3], (DEPTH, DEC_BATCH, wb, ATTN_KV_HEADS, HEAD_DIM), 1.0),
        "cache_win_v": nrm(ks[4], (DEPTH, DEC_BATCH, wb, ATTN_KV_HEADS, HEAD_DIM), BETA),
        "ln1_g": 1.0 + nrm(ks[5], (DEPTH, D), 0.02),
        "ln1_b": nrm(ks[6], (DEPTH, D), 0.02),
        "ffn1_w13": nrm(ks[7], (DEPTH, D, 2 * D_FF), D ** -0.5),
        "ffn1_w2": nrm(ks[8], (DEPTH, D_FF, D), BETA * D_FF ** -0.5),
        "w_in": nrm(ks[9], (DEPTH, D, D_IN), D ** -0.5) * col_scale,
        "lb_param": nrm(ks[10], (DEPTH + 1, HGRN_KW), 0.1),
        "attn_sink": nrm(ks[11], (DEPTH, ATTN_HEADS), 0.5),
        "attn_norm_g": 1.0 + nrm(ks[12], (DEPTH, ATTN_WIDTH), 0.02),
        "hgrn_norm_g": 1.0 + nrm(ks[13], (DEPTH, HGRN_VW), 0.02),
        "w_out": nrm(ks[14], (DEPTH, D_MIX, D), BETA * D_MIX ** -0.5),
        "ln2_g": 1.0 + nrm(ks[15], (DEPTH, D), 0.02),
        "ln2_b": nrm(ks[16], (DEPTH, D), 0.02),
        "ffn2_w13": nrm(ks[17], (DEPTH, D, 2 * D_FF), D ** -0.5),
        "ffn2_w2": nrm(ks[18], (DEPTH, D_FF, D), BETA * D_FF ** -0.5),
        "ln3_g": 1.0 + nrm(ks[19], (DEPTH, D), 0.02),
        "ln3_b": nrm(ks[20], (DEPTH, D), 0.02),
    }


def reference(x_prompt, x_sample, state_hgrn, cache_win_k, cache_win_v, ln1_g, ln1_b, ffn1_w13,
              ffn1_w2, w_in, lb_param, attn_sink, attn_norm_g, hgrn_norm_g, w_out, ln2_g, ln2_b,
              ffn2_w13, ffn2_w2, ln3_g, ln3_b):
    lb_all = jnp.cumsum(jax.nn.softmax(lb_param.astype(jnp.float32), axis=0), axis=0)
    xp, xs = x_prompt, x_sample
    sp_list, kp_list, vp_list, ss_list, ks_list, vs_list = [], [], [], [], [], []
    for l in range(DEPTH):
        prm = (ln1_g[l], ln1_b[l], ffn1_w13[l], ffn1_w2[l], w_in[l], attn_sink[l], attn_norm_g[l],
               hgrn_norm_g[l], w_out[l], ln2_g[l], ln2_b[l], ffn2_w13[l], ffn2_w2[l], ln3_g[l], ln3_b[l])
        S0p = jnp.zeros((xp.shape[0], HGRN_HEADS, HGRN_DK, HGRN_DV), jnp.float32)
        xp, Sp, kp, vp = _decoder_layer(xp, S0p, None, None, lb_all[l], *prm)
        xs, Ss, kq, vq = _decoder_layer(xs, state_hgrn[l], cache_win_k[l], cache_win_v[l], lb_all[l], *prm)
        sp_list.append(Sp); kp_list.append(kp); vp_list.append(vp)
        ss_list.append(Ss); ks_list.append(kq); vs_list.append(vq)
    return (xp, xs, jnp.stack(sp_list), jnp.stack(kp_list), jnp.stack(vp_list),
            jnp.stack(ss_list), jnp.stack(ks_list), jnp.stack(vs_list))
```

```python
import functools

import jax
import jax.numpy as jnp
from jax import lax
from jax.experimental import pallas as pl
from jax.experimental.pallas import tpu as pltpu

F32 = jnp.float32
BF16 = jnp.bfloat16

D_MODEL = 1024
DEPTH = 1
ATTN_HEADS = 8
ATTN_KV_HEADS = 2
HEAD_DIM = 64
ATTN_WIDTH = ATTN_HEADS * HEAD_DIM
KV_WIDTH = ATTN_KV_HEADS * HEAD_DIM
WINDOW = 128
ATTN_SCALE = HEAD_DIM ** -0.5
HGRN_HEADS = 4
HGRN_DK = 128
HGRN_DV = 128
HGRN_KW = HGRN_HEADS * HGRN_DK
HGRN_VW = HGRN_HEADS * HGRN_DV
D_MIX = ATTN_WIDTH + HGRN_VW
D_IN = ATTN_WIDTH + 2 * KV_WIDTH + 2 * HGRN_KW + 2 * HGRN_VW
D_FF = 2816
ALPHA = (2.0 * DEPTH) ** 0.25
LN_EPS = 1e-5
RMS_EPS = 1e-6

OFF_K = ATTN_WIDTH
OFF_V = OFF_K + KV_WIDTH
OFF_HQ = OFF_V + KV_WIDTH
OFF_HF = OFF_HQ + HGRN_KW
OFF_HI = OFF_HF + HGRN_KW
OFF_HG = OFF_HI + HGRN_VW

LANES = 128
SUBLANES = 8
BLK = 128
SEQ_PER_STEP = BLK // 8
VMEM_LIMIT = 56 * 1024 * 1024

NT_DIMS = (((1,), (1,)), ((), ()))
TN_DIMS = (((0,), (0,)), ((), ()))


def _dot(a, b):
    return jnp.dot(a, b, preferred_element_type=F32)


def _dot_nt(a, b):
    return lax.dot_general(a, b, NT_DIMS, preferred_element_type=F32)


def _dot_tn(a, b):
    return lax.dot_general(a, b, TN_DIMS, preferred_element_type=F32)


def _layer_norm(r, g, b):
    mu = jnp.mean(r, axis=-1, keepdims=True)
    c = r - mu
    var = jnp.mean(c * c, axis=-1, keepdims=True)
    return c * lax.rsqrt(var + LN_EPS) * g + b


def _rms_norm(x, g):
    return x * lax.rsqrt(jnp.mean(x * x, axis=-1, keepdims=True) + RMS_EPS) * g


def _ffn_ln_kernel(x_ref, w13_ref, w2_ref, g_ref, b_ref, o_ref):
    x = x_ref[...]
    h = _dot(x.astype(BF16), w13_ref[...])
    gate = h[:, :D_FF]
    up = h[:, D_FF:]
    act = (jax.nn.silu(gate) * up).astype(BF16)
    y = _dot(act, w2_ref[...])
    o_ref[...] = _layer_norm(ALPHA * x + 0.5 * y, g_ref[...], b_ref[...])


def _const_spec(shape):
    return pl.BlockSpec(shape, lambda *_: (0,) * len(shape), pipeline_mode=pl.Buffered(1))


def _ffn_ln(x, w13, w2, g, b, tm):
    n = x.shape[0]
    return pl.pallas_call(
        _ffn_ln_kernel,
        grid=(n // tm,),
        in_specs=[
            pl.BlockSpec((tm, D_MODEL), lambda i: (i, 0)),
            _const_spec((D_MODEL, 2 * D_FF)),
            _const_spec((D_FF, D_MODEL)),
            _const_spec((1, D_MODEL)),
            _const_spec((1, D_MODEL)),
        ],
        out_specs=pl.BlockSpec((tm, D_MODEL), lambda i: (i, 0)),
        out_shape=jax.ShapeDtypeStruct((n, D_MODEL), F32),
        compiler_params=pltpu.CompilerParams(
            dimension_semantics=("arbitrary",), vmem_limit_bytes=VMEM_LIMIT),
        name="ffn_ln",
    )(x, w13, w2, g, b)


def _lower_bound(lbp):
    mx = jnp.max(lbp, axis=0, keepdims=True)
    e = jnp.exp(lbp - mx)
    return e[0:1] / jnp.sum(e, axis=0, keepdims=True)


def _split3(x):
    hi = x.astype(BF16)
    r = x - hi.astype(F32)
    mid = r.astype(BF16)
    lo = (r - mid.astype(F32)).astype(BF16)
    return hi, mid, lo


def _block_cumsum(tri, g):
    w = g.shape[1]
    parts = jnp.concatenate(_split3(g), axis=1)
    s = _dot(tri, parts)
    return s[:, :w] + s[:, w:2 * w] + s[:, 2 * w:]


def _pair_ref_rows(b, m):
    t, w = b.shape
    if m >= SUBLANES:
        b3 = b.reshape(t // (2 * m), 2 * m, w)
        return jnp.broadcast_to(b3[:, m - 1:m, :], b3.shape).reshape(t, w)
    b3 = b.reshape(t // SUBLANES, SUBLANES, w)
    sub = lax.broadcasted_iota(jnp.int32, b3.shape, 1)

    def row(i):
        return jnp.broadcast_to(b3[:, i:i + 1, :], b3.shape)

    if m == 4:
        out = row(3)
    elif m == 2:
        out = jnp.where(sub < 4, row(1), row(5))
    else:
        out = jnp.where(sub < 4, jnp.where(sub < 2, row(0), row(2)),
                        jnp.where(sub < 6, row(4), row(6)))
    return out.reshape(t, w)


def _intra_scores(q, k, b, max_m):
    t = q.shape[0]
    row = lax.broadcasted_iota(jnp.int32, (t, t), 0)
    col = lax.broadcasted_iota(jnp.int32, (t, t), 1)
    xor = row ^ col
    rid = lax.broadcasted_iota(jnp.int32, q.shape, 0)
    a = jnp.where(xor == 0, jnp.sum(q * k, axis=-1, keepdims=True), 0.0)
    m = max_m
    while m >= 1:
        e = jnp.exp(-jnp.abs(b - _pair_ref_rows(b, m)))
        upper = (rid & m) != 0
        qs = jnp.where(upper, q * e, 0.0).astype(BF16)
        ks = jnp.where(upper, 0.0, k * e).astype(BF16)
        p = _dot_nt(qs, ks)
        a = a + jnp.where((xor >= m) & (xor < 2 * m), p, 0.0)
        m //= 2
    return a


def _hgrn_gates(proj, lb):
    q = jax.nn.silu(proj[:, OFF_HQ:OFF_HF])
    f = lb + (1.0 - lb) * jax.nn.sigmoid(proj[:, OFF_HF:OFF_HI])
    return q, jnp.log(f), 1.0 - f, proj[:, OFF_HI:OFF_HG]


def _attend(qa, k_all, v_all, sink_ref, limit):
    tq = qa.shape[0]
    nk = k_all.shape[0]
    lane = lax.broadcasted_iota(jnp.int32, k_all.shape, 1)
    low = lane < HEAD_DIM
    k_rot = pltpu.roll(k_all, HEAD_DIM, axis=1)
    v_rot = pltpu.roll(v_all, HEAD_DIM, axis=1)
    zero = jnp.zeros_like(k_all)
    kb = [[jnp.where(low, k_all, zero).astype(BF16), jnp.where(low, zero, k_rot).astype(BF16)],
          [jnp.where(low, k_rot, zero).astype(BF16), jnp.where(low, zero, k_all).astype(BF16)]]
    vb = [[jnp.where(low, v_all, zero).astype(BF16), jnp.where(low, zero, v_rot).astype(BF16)],
          [jnp.where(low, v_rot, zero).astype(BF16), jnp.where(low, zero, v_all).astype(BF16)]]
    trow = lax.broadcasted_iota(jnp.int32, (tq, nk), 0)
    ccol = lax.broadcasted_iota(jnp.int32, (tq, nk), 1)
    dist = trow + WINDOW - ccol
    visible = dist.astype(jnp.uint32) < limit.astype(jnp.uint32)
    distf = dist.astype(F32)
    outs = []
    for pair in range(ATTN_HEADS // 2):
        kv = pair // 2
        qp = (qa[:, LANES * pair:LANES * (pair + 1)] * ATTN_SCALE).astype(BF16)
        acc = jnp.zeros((tq, LANES), F32)
        for half in range(2):
            head = 2 * pair + half
            slope = 2.0 ** (-8.0 * (head + 1) / ATTN_HEADS)
            sink = sink_ref[head]
            s = _dot_nt(qp, kb[kv][half])
            s = jnp.where(visible, s - slope * distf, -jnp.inf)
            mx = jnp.maximum(jnp.max(s, axis=-1, keepdims=True), sink)
            p = jnp.exp(s - mx)
            denom = jnp.sum(p, axis=-1, keepdims=True) + jnp.exp(sink - mx)
            acc = acc + _dot(p.astype(BF16), vb[kv][half]) / denom
        outs.append(acc)
    return jnp.concatenate(outs, axis=1)


def _mix_out(x, oa, oh, proj, w_out_ref, ang_ref, hng_ref, g2_ref, b2_ref):
    oa = _rms_norm(oa, ang_ref[...])
    hng = hng_ref[...]
    gate = jax.nn.silu(proj[:, OFF_HG:])
    ohn = [_rms_norm(oh[h], hng[:, HGRN_DV * h:HGRN_DV * (h + 1)]) for h in range(HGRN_HEADS)]
    mix = jnp.concatenate([oa, jnp.concatenate(ohn, axis=1) * gate], axis=1).astype(BF16)
    y = _dot(mix, w_out_ref[...])
    return _layer_norm(ALPHA * x + y, g2_ref[...], b2_ref[...])


def _tri_matrix(group):
    r = lax.broadcasted_iota(jnp.int32, (BLK, BLK), 0)
    c = lax.broadcasted_iota(jnp.int32, (BLK, BLK), 1)
    keep = (c <= r) & ((r ^ c) < group)
    return jnp.where(keep, 1.0, 0.0).astype(BF16)


def _mixer_prompt_kernel(sink_ref, x_ref, w_in_ref, w_out_ref, lbp_ref, ang_ref, hng_ref, g2_ref,
                         b2_ref, y_ref, s_out_ref, k_out_ref, v_out_ref, st_ref, kv_ref):
    j = pl.program_id(1)

    @pl.when(j == 0)
    def _():
        st_ref[...] = jnp.zeros_like(st_ref)
        kv_ref[...] = jnp.zeros_like(kv_ref)

    x = x_ref[0]
    proj = _dot(x.astype(BF16), w_in_ref[...])
    ka = proj[:, OFF_K:OFF_V]
    va = proj[:, OFF_V:OFF_HQ]

    trow = lax.broadcasted_iota(jnp.int32, (BLK, 2 * WINDOW), 0)
    limit = jnp.where(j > 0, WINDOW, trow + 1)
    k_all = jnp.concatenate([kv_ref[0], ka], axis=0)
    v_all = jnp.concatenate([kv_ref[1], va], axis=0)
    oa = _attend(proj[:, :ATTN_WIDTH], k_all, v_all, sink_ref, limit)
    kv_ref[0] = ka
    kv_ref[1] = va

    lb = _lower_bound(lbp_ref[...])
    q, logf, kin, vin = _hgrn_gates(proj, lb)
    b = _block_cumsum(_tri_matrix(BLK), logf)
    oh = []
    for h in range(HGRN_HEADS):
        sl = slice(HGRN_DK * h, HGRN_DK * (h + 1))
        qh, kh, bh = q[:, sl], kin[:, sl], b[:, sl]
        vh = vin[:, sl].astype(BF16)
        st = st_ref[h]
        a = _intra_scores(qh, kh, bh, BLK // 2)
        o = _dot_nt((qh * jnp.exp(bh)).astype(BF16), st.astype(BF16))
        oh.append(o + _dot(a.astype(BF16), vh))
        b_last = bh[BLK - 1:BLK, :]
        kd = (kh * jnp.exp(b_last - bh)).astype(BF16)
        st_ref[h] = st * jnp.exp(b_last) + _dot_tn(vh, kd)

    y_ref[0] = _mix_out(x, oa, oh, proj, w_out_ref, ang_ref, hng_ref, g2_ref, b2_ref)

    @pl.when(j == pl.num_programs(1) - 1)
    def _():
        k_out_ref[0] = ka
        v_out_ref[0] = va
        for h in range(HGRN_HEADS):
            s_out_ref[0, h] = st_ref[h].T


def _mixer_prompt(x, sink, w_in, w_out, lbp, ang, hng, g2, b2):
    nb, seq, _ = x.shape
    return pl.pallas_call(
        _mixer_prompt_kernel,
        grid=(nb, seq // BLK),
        in_specs=[
            pl.BlockSpec(memory_space=pltpu.SMEM),
            pl.BlockSpec((1, BLK, D_MODEL), lambda b, j: (b, j, 0)),
            _const_spec((D_MODEL, D_IN)),
            _const_spec((D_MIX, D_MODEL)),
            _const_spec((DEPTH + 1, HGRN_KW)),
            _const_spec((1, ATTN_WIDTH)),
            _const_spec((1, HGRN_VW)),
            _const_spec((1, D_MODEL)),
            _const_spec((1, D_MODEL)),
        ],
        out_specs=[
            pl.BlockSpec((1, BLK, D_MODEL), lambda b, j: (b, j, 0)),
            pl.BlockSpec((1, HGRN_HEADS, HGRN_DK, HGRN_DV), lambda b, j: (b, 0, 0, 0)),
            pl.BlockSpec((1, WINDOW, KV_WIDTH), lambda b, j: (b, 0, 0)),
            pl.BlockSpec((1, WINDOW, KV_WIDTH), lambda b, j: (b, 0, 0)),
        ],
        out_shape=[
            jax.ShapeDtypeStruct((nb, seq, D_MODEL), F32),
            jax.ShapeDtypeStruct((nb, HGRN_HEADS, HGRN_DK, HGRN_DV), F32),
            jax.ShapeDtypeStruct((nb, WINDOW, KV_WIDTH), F32),
            jax.ShapeDtypeStruct((nb, WINDOW, KV_WIDTH), F32),
        ],
        scratch_shapes=[
            pltpu.VMEM((HGRN_HEADS, HGRN_DV, HGRN_DK), F32),
            pltpu.VMEM((2, WINDOW, KV_WIDTH), F32),
        ],
        compiler_params=pltpu.CompilerParams(
            dimension_semantics=("arbitrary", "arbitrary"), vmem_limit_bytes=VMEM_LIMIT),
        name="mixer_prompt",
    )(sink, x, w_in, w_out, lbp, ang, hng, g2, b2)


def _mixer_sample_kernel(sink_ref, x_ref, s0_ref, ck_ref, cv_ref, w_in_ref, w_out_ref, lbp_ref,
                         ang_ref, hng_ref, g2_ref, b2_ref, y_ref, s_out_ref, k_out_ref, v_out_ref,
                         oa_ref):
    tn = x_ref.shape[0] // SEQ_PER_STEP
    x = x_ref[...]
    proj = _dot(x.astype(BF16), w_in_ref[...])

    limit = jnp.full((tn, 2 * WINDOW), WINDOW, jnp.int32)
    pad = jnp.zeros((WINDOW - tn, KV_WIDTH), F32)
    for i in range(SEQ_PER_STEP):
        rows = slice(tn * i, tn * (i + 1))
        k_new = proj[rows, OFF_K:OFF_V]
        v_new = proj[rows, OFF_V:OFF_HQ]
        ck = ck_ref[i]
        cv = cv_ref[i]
        k_all = jnp.concatenate([ck, k_new, pad], axis=0)
        v_all = jnp.concatenate([cv, v_new, pad], axis=0)
        oa_ref[rows, :] = _attend(proj[rows, :ATTN_WIDTH], k_all, v_all, sink_ref, limit)
        k_out_ref[i] = jnp.concatenate([ck[tn:], k_new], axis=0)
        v_out_ref[i] = jnp.concatenate([cv[tn:], v_new], axis=0)

    lb = _lower_bound(lbp_ref[...])
    q, logf, kin, vin = _hgrn_gates(proj, lb)
    b = _block_cumsum(_tri_matrix(tn), logf)
    lane = lax.broadcasted_iota(jnp.int32, (HGRN_DK, BLK), 1)
    oh = []
    for h in range(HGRN_HEADS):
        sl = slice(HGRN_DK * h, HGRN_DK * (h + 1))
        qh, kh, bh = q[:, sl], kin[:, sl], b[:, sl]
        vh = vin[:, sl].astype(BF16)
        a = _intra_scores(qh, kh, bh, tn // 2)
        o_intra = _dot(a.astype(BF16), vh)
        qe = (qh * jnp.exp(bh)).astype(BF16)
        b3 = bh.reshape(SEQ_PER_STEP, tn, HGRN_DK)
        b_last = jnp.broadcast_to(b3[:, tn - 1:tn, :], b3.shape).reshape(BLK, HGRN_DK)
        kd_t = (kh * jnp.exp(b_last - bh)).T
        dec_t = jnp.exp(b_last).T
        o_rows = []
        for i in range(SEQ_PER_STEP):
            s0 = s0_ref[i, h]
            o_rows.append(_dot(qe[tn * i:tn * (i + 1)], s0.astype(BF16)))
            kd_i = jnp.where((lane >= tn * i) & (lane < tn * (i + 1)), kd_t, 0.0).astype(BF16)
            s_out_ref[i, h] = s0 * dec_t[:, tn * i:tn * i + 1] + _dot(kd_i, vh)
        oh.append(o_intra + jnp.concatenate(o_rows, axis=0))

    y_ref[...] = _mix_out(x, oa_ref[...], oh, proj, w_out_ref, ang_ref, hng_ref, g2_ref, b2_ref)


def _mixer_sample(x, s0, ck, cv, sink, w_in, w_out, lbp, ang, hng, g2, b2):
    n = x.shape[0]
    nseq = s0.shape[0]
    state_spec = pl.BlockSpec((SEQ_PER_STEP, HGRN_HEADS, HGRN_DK, HGRN_DV), lambda i: (i, 0, 0, 0))
    cache_spec = pl.BlockSpec((SEQ_PER_STEP, WINDOW, KV_WIDTH), lambda i: (i, 0, 0))
    return pl.pallas_call(
        _mixer_sample_kernel,
        grid=(n // BLK,),
        in_specs=[
            pl.BlockSpec(memory_space=pltpu.SMEM),
            pl.BlockSpec((BLK, D_MODEL), lambda i: (i, 0)),
            state_spec, cache_spec, cache_spec,
            _const_spec((D_MODEL, D_IN)),
            _const_spec((D_MIX, D_MODEL)),
            _const_spec((DEPTH + 1, HGRN_KW)),
            _const_spec((1, ATTN_WIDTH)),
            _const_spec((1, HGRN_VW)),
            _const_spec((1, D_MODEL)),
            _const_spec((1, D_MODEL)),
        ],
        out_specs=[
            pl.BlockSpec((BLK, D_MODEL), lambda i: (i, 0)),
            state_spec, cache_spec, cache_spec,
        ],
        out_shape=[
            jax.ShapeDtypeStruct((n, D_MODEL), F32),
            jax.ShapeDtypeStruct((nseq, HGRN_HEADS, HGRN_DK, HGRN_DV), F32),
            jax.ShapeDtypeStruct((nseq, WINDOW, KV_WIDTH), F32),
            jax.ShapeDtypeStruct((nseq, WINDOW, KV_WIDTH), F32),
        ],
        scratch_shapes=[pltpu.VMEM((BLK, ATTN_WIDTH), F32)],
        compiler_params=pltpu.CompilerParams(
            dimension_semantics=("arbitrary",), vmem_limit_bytes=VMEM_LIMIT),
        name="mixer_sample",
    )(sink, x, s0, ck, cv, w_in, w_out, lbp, ang, hng, g2, b2)


def kernel(x_prompt, x_sample, state_hgrn, cache_win_k, cache_win_v, ln1_g, ln1_b, ffn1_w13,
           ffn1_w2, w_in, lb_param, attn_sink, attn_norm_g, hgrn_norm_g, w_out, ln2_g, ln2_b,
           ffn2_w13, ffn2_w2, ln3_g, ln3_b):
    nb, seq, d = x_prompt.shape
    ns, tn, _ = x_sample.shape
    assert state_hgrn.shape[0] == DEPTH == 1 and d == D_MODEL
    assert seq % BLK == 0 and tn == 8 and (ns * tn) % BLK == 0
    assert cache_win_k.shape[2] == WINDOW

    w13_1 = ffn1_w13[0].astype(BF16)
    w2_1 = ffn1_w2[0].astype(BF16)
    w13_2 = ffn2_w13[0].astype(BF16)
    w2_2 = ffn2_w2[0].astype(BF16)
    w_in_b = w_in[0].astype(BF16)
    w_out_b = w_out[0].astype(BF16)
    sink = attn_sink[0]
    lbp = lb_param.astype(F32)
    mix_params = (sink, w_in_b, w_out_b, lbp, attn_norm_g, hgrn_norm_g, ln2_g, ln2_b)

    xp = x_prompt.reshape(nb * seq, d)
    xs = x_sample.reshape(ns * tn, d)

    xp = _ffn_ln(xp, w13_1, w2_1, ln1_g, ln1_b, tm=256)
    xs = _ffn_ln(xs, w13_1, w2_1, ln1_g, ln1_b, tm=256)

    xp, sp, kp, vp = _mixer_prompt(xp.reshape(nb, seq, d), *mix_params)
    xs, ss, ks, vs = _mixer_sample(
        xs, state_hgrn[0], cache_win_k[0].reshape(ns, WINDOW, KV_WIDTH),
        cache_win_v[0].reshape(ns, WINDOW, KV_WIDTH), *mix_params)

    xp = _ffn_ln(xp.reshape(nb * seq, d), w13_2, w2_2, ln3_g, ln3_b, tm=256)
    xs = _ffn_ln(xs, w13_2, w2_2, ln3_g, ln3_b, tm=256)

    cache_shape = (DEPTH, -1, WINDOW, ATTN_KV_HEADS, HEAD_DIM)
    return (xp.reshape(nb, seq, d), xs.reshape(ns, tn, d),
            sp[None], kp.reshape((DEPTH, nb) + cache_shape[2:]), vp.reshape((DEPTH, nb) + cache_shape[2:]),
            ss[None], ks.reshape((DEPTH, ns) + cache_shape[2:]), vs.reshape((DEPTH, ns) + cache_shape[2:]))
```

```python
import jax
import jax.numpy as jnp
from jax import lax
from jax.experimental import pallas as pl
from jax.experimental.pallas import tpu as pltpu

F32 = jnp.float32
BF16 = jnp.bfloat16

D_MODEL = 1024
DEPTH = 1
ATTN_HEADS = 8
ATTN_KV_HEADS = 2
GROUP = ATTN_HEADS // ATTN_KV_HEADS
HEAD_DIM = 64
ATTN_WIDTH = ATTN_HEADS * HEAD_DIM
KV_WIDTH = ATTN_KV_HEADS * HEAD_DIM
WINDOW = 128
ATTN_SCALE = HEAD_DIM ** -0.5
HGRN_HEADS = 4
HGRN_DK = 128
HGRN_DV = 128
HGRN_KW = HGRN_HEADS * HGRN_DK
HGRN_VW = HGRN_HEADS * HGRN_DV
D_MIX = ATTN_WIDTH + HGRN_VW
D_IN = ATTN_WIDTH + 2 * KV_WIDTH + 2 * HGRN_KW + 2 * HGRN_VW
D_FF = 2816
ALPHA = (2.0 * DEPTH) ** 0.25
LN_EPS = 1e-5
RMS_EPS = 1e-6

OFF_K = ATTN_WIDTH
OFF_V = OFF_K + KV_WIDTH
OFF_HQ = OFF_V + KV_WIDTH
OFF_HF = OFF_HQ + HGRN_KW
OFF_HI = OFF_HF + HGRN_KW
OFF_HG = OFF_HI + HGRN_VW

LANES = 128
SUBLANES = 8
BLK = 128
PROMPT_BLOCKS = 2
TOK = SUBLANES
SEQ_PER_STEP = BLK // TOK
VMEM_LIMIT = 56 * 1024 * 1024

NT_DIMS = (((1,), (1,)), ((), ()))
TN_DIMS = (((0,), (0,)), ((), ()))


def _dot(a, b):
    return jnp.dot(a, b, preferred_element_type=F32)


def _dot_nt(a, b):
    return lax.dot_general(a, b, NT_DIMS, preferred_element_type=F32)


def _dot_tn(a, b):
    return lax.dot_general(a, b, TN_DIMS, preferred_element_type=F32)


def _layer_norm(r, g, b):
    mu = jnp.mean(r, axis=-1, keepdims=True)
    c = r - mu
    var = jnp.mean(c * c, axis=-1, keepdims=True)
    return c * lax.rsqrt(var + LN_EPS) * g + b


def _rms_norm(x, g):
    return x * lax.rsqrt(jnp.mean(x * x, axis=-1, keepdims=True) + RMS_EPS) * g


def _ffn_ln_kernel(x_ref, w13_ref, w2_ref, g_ref, b_ref, o_ref):
    x = x_ref[...]
    h = _dot(x.astype(BF16), w13_ref[...])
    gate = h[:, :D_FF]
    up = h[:, D_FF:]
    act = (jax.nn.silu(gate) * up).astype(BF16)
    y = _dot(act, w2_ref[...])
    o_ref[...] = _layer_norm(ALPHA * x + 0.5 * y, g_ref[...], b_ref[...])


def _const_spec(shape):
    return pl.BlockSpec(shape, lambda *_: (0,) * len(shape), pipeline_mode=pl.Buffered(1))


def _ffn_ln(x, w13, w2, g, b, tm):
    n = x.shape[0]
    return pl.pallas_call(
        _ffn_ln_kernel,
        grid=(n // tm,),
        in_specs=[
            pl.BlockSpec((tm, D_MODEL), lambda i: (i, 0)),
            _const_spec((D_MODEL, 2 * D_FF)),
            _const_spec((D_FF, D_MODEL)),
            _const_spec((1, D_MODEL)),
            _const_spec((1, D_MODEL)),
        ],
        out_specs=pl.BlockSpec((tm, D_MODEL), lambda i: (i, 0)),
        out_shape=jax.ShapeDtypeStruct((n, D_MODEL), F32),
        compiler_params=pltpu.CompilerParams(
            dimension_semantics=("arbitrary",), vmem_limit_bytes=VMEM_LIMIT),
        name="ffn_ln",
    )(x, w13, w2, g, b)


def _lower_bound(lbp):
    mx = jnp.max(lbp, axis=0, keepdims=True)
    e = jnp.exp(lbp - mx)
    return e[0:1] / jnp.sum(e, axis=0, keepdims=True)


def _split3(x):
    hi = x.astype(BF16)
    r = x - hi.astype(F32)
    mid = r.astype(BF16)
    lo = (r - mid.astype(F32)).astype(BF16)
    return hi, mid, lo


def _block_cumsum(tri, g):
    w = g.shape[1]
    parts = jnp.concatenate(_split3(g), axis=1)
    s = _dot(tri, parts)
    return s[:, :w] + s[:, w:2 * w] + s[:, 2 * w:]


def _tri_matrix(group):
    r = lax.broadcasted_iota(jnp.int32, (BLK, BLK), 0)
    c = lax.broadcasted_iota(jnp.int32, (BLK, BLK), 1)
    keep = (c <= r) & ((r ^ c) < group)
    return jnp.where(keep, 1.0, 0.0).astype(BF16)


def _pair_ref_rows(b, m):
    t, w = b.shape
    if m >= SUBLANES:
        b3 = b.reshape(t // (2 * m), 2 * m, w)
        return jnp.broadcast_to(b3[:, m - 1:m, :], b3.shape).reshape(t, w)
    b3 = b.reshape(t // SUBLANES, SUBLANES, w)
    sub = lax.broadcasted_iota(jnp.int32, b3.shape, 1)

    def row(i):
        return jnp.broadcast_to(b3[:, i:i + 1, :], b3.shape)

    if m == 4:
        out = row(3)
    elif m == 2:
        out = jnp.where(sub < 4, row(1), row(5))
    else:
        out = jnp.where(sub < 4, jnp.where(sub < 2, row(0), row(2)),
                        jnp.where(sub < 6, row(4), row(6)))
    return out.reshape(t, w)


def _level_operands(q, k, e, m):
    t, w = q.shape
    if m >= SUBLANES:
        zero = jnp.zeros((m, w), F32)
        qs, ks = [], []
        for blk in range(t // m):
            rows = slice(blk * m, (blk + 1) * m)
            if blk % 2:
                qs.append(q[rows] * e[rows])
                ks.append(zero)
            else:
                qs.append(zero)
                ks.append(k[rows] * e[rows])
        return jnp.concatenate(qs, axis=0).astype(BF16), jnp.concatenate(ks, axis=0).astype(BF16)
    upper = (lax.broadcasted_iota(jnp.int32, q.shape, 0) & m) != 0
    return (jnp.where(upper, q * e, 0.0).astype(BF16), jnp.where(upper, 0.0, k * e).astype(BF16))


def _intra_scores(q, k, b, max_m, xor):
    a = jnp.where(xor == 0, jnp.sum(q * k, axis=-1, keepdims=True), 0.0)
    m = 1
    while m <= max_m:
        e = jnp.exp(-jnp.abs(b - _pair_ref_rows(b, m)))
        qs, ks = _level_operands(q, k, e, m)
        a = jnp.where(xor >= m, _dot_nt(qs, ks), a)
        m *= 2
    if 2 * max_m < q.shape[0]:
        a = jnp.where(xor < 2 * max_m, a, 0.0)
    return a


def _hgrn_gates(proj, lb):
    q = jax.nn.silu(proj[:, OFF_HQ:OFF_HF])
    f = lb + (1.0 - lb) * jax.nn.sigmoid(proj[:, OFF_HF:OFF_HI])
    return q, jnp.log(f), 1.0 - f, proj[:, OFF_HI:OFF_HG]


def _head_queries(qa):
    low = lax.broadcasted_iota(jnp.int32, (qa.shape[0], LANES), 1) < HEAD_DIM
    out = []
    for pair in range(ATTN_HEADS // 2):
        piece = qa[:, LANES * pair:LANES * (pair + 1)] * ATTN_SCALE
        rolled = pltpu.roll(piece, HEAD_DIM, axis=1)
        for half in range(2):
            kv = (2 * pair + half) // GROUP
            src = piece if half == kv else rolled
            z = jnp.where(low, src, 0.0) if kv == 0 else jnp.where(low, 0.0, src)
            out.append(z.astype(BF16))
    return out


def _merge_heads(o):
    low = lax.broadcasted_iota(jnp.int32, o[0].shape, 1) < HEAD_DIM
    cols = []
    for pair in range(ATTN_HEADS // 2):
        kv = (2 * pair) // GROUP
        even, odd = o[2 * pair], o[2 * pair + 1]
        if kv == 0:
            odd = pltpu.roll(odd, HEAD_DIM, axis=1)
        else:
            even = pltpu.roll(even, HEAD_DIM, axis=1)
        cols.append(jnp.where(low, even, odd))
    return jnp.concatenate(cols, axis=1)


def _slope(head):
    return 2.0 ** (-8.0 * (head + 1) / ATTN_HEADS)


def _attend_block(qz, k_all, v_all, sink_ref, limit):
    tq = qz[0].shape[0]
    nk = k_all.shape[0]
    s_all = _dot_nt(jnp.concatenate(qz, axis=0), k_all.astype(BF16))
    trow = lax.broadcasted_iota(jnp.int32, (tq, nk), 0)
    ccol = lax.broadcasted_iota(jnp.int32, (tq, nk), 1)
    dist = trow + WINDOW - ccol
    visible = dist.astype(jnp.uint32) < limit.astype(jnp.uint32)
    distf = dist.astype(F32)
    ps, dens = [], []
    for head in range(ATTN_HEADS):
        sink = sink_ref[head]
        s = s_all[tq * head:tq * (head + 1)]
        s = jnp.where(visible, s - _slope(head) * distf, -jnp.inf)
        mx = jnp.maximum(jnp.max(s, axis=-1, keepdims=True), sink)
        p = jnp.exp(s - mx)
        dens.append(jnp.sum(p, axis=-1, keepdims=True) + jnp.exp(sink - mx))
        ps.append(p.astype(BF16))
    o_all = _dot(jnp.concatenate(ps, axis=0), v_all.astype(BF16))
    return _merge_heads([o_all[tq * h:tq * (h + 1)] / dens[h] for h in range(ATTN_HEADS)])


def _attend_sample(qz, k_new, v_new, ck_ref, cv_ref, sink_ref):
    rows = qz[0].shape[0]
    k_new = k_new.astype(BF16)
    v_new = v_new.astype(BF16)
    sn_all = _dot_nt(jnp.concatenate(qz, axis=0), k_new)
    sc_seq = []
    for i in range(SEQ_PER_STEP):
        lhs = jnp.concatenate([z[TOK * i:TOK * (i + 1)] for z in qz], axis=0)
        sc_seq.append(_dot_nt(lhs, ck_ref[i].astype(BF16)))

    r = lax.broadcasted_iota(jnp.int32, (rows, rows), 0)
    c = lax.broadcasted_iota(jnp.int32, (rows, rows), 1)
    new_visible = ((r ^ c) < TOK) & (c <= r)
    new_dist = (r - c).astype(F32)
    tok = lax.broadcasted_iota(jnp.int32, (rows, WINDOW), 0) & (TOK - 1)
    key = lax.broadcasted_iota(jnp.int32, (rows, WINDOW), 1)
    old_visible = key > tok
    old_dist = (WINDOW + tok - key).astype(F32)

    pn, pc, dens = [], [], []
    for head in range(ATTN_HEADS):
        sink = sink_ref[head]
        slope = _slope(head)
        sn = sn_all[rows * head:rows * (head + 1)]
        sc = jnp.concatenate([s[TOK * head:TOK * (head + 1)] for s in sc_seq], axis=0)
        sn = jnp.where(new_visible, sn - slope * new_dist, -jnp.inf)
        sc = jnp.where(old_visible, sc - slope * old_dist, -jnp.inf)
        mx = jnp.maximum(jnp.maximum(jnp.max(sn, axis=-1, keepdims=True),
                                     jnp.max(sc, axis=-1, keepdims=True)), sink)
        en = jnp.exp(sn - mx)
        ec = jnp.exp(sc - mx)
        dens.append(jnp.sum(en, axis=-1, keepdims=True) + jnp.sum(ec, axis=-1, keepdims=True)
                    + jnp.exp(sink - mx))
        pn.append(en.astype(BF16))
        pc.append(ec.astype(BF16))

    on_all = _dot(jnp.concatenate(pn, axis=0), v_new)
    oc_seq = []
    for i in range(SEQ_PER_STEP):
        lhs = jnp.concatenate([p[TOK * i:TOK * (i + 1)] for p in pc], axis=0)
        oc_seq.append(_dot(lhs, cv_ref[i].astype(BF16)))
    o = []
    for head in range(ATTN_HEADS):
        oc = jnp.concatenate([s[TOK * head:TOK * (head + 1)] for s in oc_seq], axis=0)
        o.append((on_all[rows * head:rows * (head + 1)] + oc) / dens[head])
    return _merge_heads(o)


def _mix_out(x, oa, oh, proj, w_out_ref, ang_ref, hng_ref, g2_ref, b2_ref):
    oa = _rms_norm(oa, ang_ref[...])
    hng = hng_ref[...]
    gate = jax.nn.silu(proj[:, OFF_HG:])
    ohn = [_rms_norm(oh[h], hng[:, HGRN_DV * h:HGRN_DV * (h + 1)]) for h in range(HGRN_HEADS)]
    mix = jnp.concatenate([oa, jnp.concatenate(ohn, axis=1) * gate], axis=1).astype(BF16)
    y = _dot(mix, w_out_ref[...])
    return _layer_norm(ALPHA * x + y, g2_ref[...], b2_ref[...])


def _mixer_prompt_kernel(sink_ref, x_ref, w_in_ref, w_out_ref, lbp_ref, ang_ref, hng_ref, g2_ref,
                         b2_ref, y_ref, s_out_ref, k_out_ref, v_out_ref, st_ref, kv_ref):
    j = pl.program_id(1)

    @pl.when(j == 0)
    def _():
        st_ref[...] = jnp.zeros_like(st_ref)
        kv_ref[...] = jnp.zeros_like(kv_ref)

    x = x_ref[0]
    proj = _dot(x.astype(BF16), w_in_ref[...])
    qz = _head_queries(proj[:, :ATTN_WIDTH])
    q, logf, kin, vin = _hgrn_gates(proj, _lower_bound(lbp_ref[...]))
    tri = _tri_matrix(BLK)
    xor = (lax.broadcasted_iota(jnp.int32, (BLK, BLK), 0)
           ^ lax.broadcasted_iota(jnp.int32, (BLK, BLK), 1))
    trow = lax.broadcasted_iota(jnp.int32, (BLK, 2 * WINDOW), 0)

    k_prev, v_prev = kv_ref[0], kv_ref[1]
    st = [st_ref[h] for h in range(HGRN_HEADS)]
    oa = []
    oh = [[] for _ in range(HGRN_HEADS)]
    for blk in range(PROMPT_BLOCKS):
        rows = slice(BLK * blk, BLK * (blk + 1))
        ka = proj[rows, OFF_K:OFF_V]
        va = proj[rows, OFF_V:OFF_HQ]
        if blk == 0:
            limit = jnp.where(j > 0, WINDOW, trow + 1)
        else:
            limit = jnp.full(trow.shape, WINDOW, jnp.int32)
        oa.append(_attend_block([z[rows] for z in qz], jnp.concatenate([k_prev, ka], axis=0),
                                jnp.concatenate([v_prev, va], axis=0), sink_ref, limit))
        k_prev, v_prev = ka, va

        b = _block_cumsum(tri, logf[rows])
        for h in range(HGRN_HEADS):
            sl = slice(HGRN_DK * h, HGRN_DK * (h + 1))
            qh, kh, bh = q[rows, sl], kin[rows, sl], b[:, sl]
            vh = vin[rows, sl].astype(BF16)
            a = _intra_scores(qh, kh, bh, BLK // 2, xor)
            o = _dot_nt((qh * jnp.exp(bh)).astype(BF16), st[h].astype(BF16))
            oh[h].append(o + _dot(a.astype(BF16), vh))
            b_last = bh[BLK - 1:BLK, :]
            kd = (kh * jnp.exp(b_last - bh)).astype(BF16)
            st[h] = st[h] * jnp.exp(b_last) + _dot_tn(vh, kd)

    kv_ref[0] = k_prev
    kv_ref[1] = v_prev
    for h in range(HGRN_HEADS):
        st_ref[h] = st[h]
    y_ref[0] = _mix_out(x, jnp.concatenate(oa, axis=0), [jnp.concatenate(o, axis=0) for o in oh],
                        proj, w_out_ref, ang_ref, hng_ref, g2_ref, b2_ref)

    @pl.when(j == pl.num_programs(1) - 1)
    def _():
        k_out_ref[0] = k_prev
        v_out_ref[0] = v_prev
        for h in range(HGRN_HEADS):
            s_out_ref[0, h] = st[h].T


def _mixer_prompt(x, sink, w_in, w_out, lbp, ang, hng, g2, b2):
    nb, seq, _ = x.shape
    rows = PROMPT_BLOCKS * BLK
    return pl.pallas_call(
        _mixer_prompt_kernel,
        grid=(nb, seq // rows),
        in_specs=[
            pl.BlockSpec(memory_space=pltpu.SMEM),
            pl.BlockSpec((1, rows, D_MODEL), lambda b, j: (b, j, 0)),
            _const_spec((D_MODEL, D_IN)),
            _const_spec((D_MIX, D_MODEL)),
            _const_spec((DEPTH + 1, HGRN_KW)),
            _const_spec((1, ATTN_WIDTH)),
            _const_spec((1, HGRN_VW)),
            _const_spec((1, D_MODEL)),
            _const_spec((1, D_MODEL)),
        ],
        out_specs=[
            pl.BlockSpec((1, rows, D_MODEL), lambda b, j: (b, j, 0)),
            pl.BlockSpec((1, HGRN_HEADS, HGRN_DK, HGRN_DV), lambda b, j: (b, 0, 0, 0)),
            pl.BlockSpec((1, WINDOW, KV_WIDTH), lambda b, j: (b, 0, 0)),
            pl.BlockSpec((1, WINDOW, KV_WIDTH), lambda b, j: (b, 0, 0)),
        ],
        out_shape=[
            jax.ShapeDtypeStruct((nb, seq, D_MODEL), F32),
            jax.ShapeDtypeStruct((nb, HGRN_HEADS, HGRN_DK, HGRN_DV), F32),
            jax.ShapeDtypeStruct((nb, WINDOW, KV_WIDTH), F32),
            jax.ShapeDtypeStruct((nb, WINDOW, KV_WIDTH), F32),
        ],
        scratch_shapes=[
            pltpu.VMEM((HGRN_HEADS, HGRN_DV, HGRN_DK), F32),
            pltpu.VMEM((2, WINDOW, KV_WIDTH), F32),
        ],
        compiler_params=pltpu.CompilerParams(
            dimension_semantics=("arbitrary", "arbitrary"), vmem_limit_bytes=VMEM_LIMIT),
        name="mixer_prompt",
    )(sink, x, w_in, w_out, lbp, ang, hng, g2, b2)


def _mixer_sample_kernel(sink_ref, x_ref, s0_ref, ck_ref, cv_ref, w_in_ref, w_out_ref, lbp_ref,
                         ang_ref, hng_ref, g2_ref, b2_ref, y_ref, s_out_ref, k_out_ref, v_out_ref):
    x = x_ref[...]
    proj = _dot(x.astype(BF16), w_in_ref[...])
    k_new = proj[:, OFF_K:OFF_V]
    v_new = proj[:, OFF_V:OFF_HQ]

    oa = _attend_sample(_head_queries(proj[:, :ATTN_WIDTH]), k_new, v_new, ck_ref, cv_ref, sink_ref)
    for i in range(SEQ_PER_STEP):
        rows = slice(TOK * i, TOK * (i + 1))
        k_out_ref[i] = jnp.concatenate([ck_ref[i, TOK:, :], k_new[rows]], axis=0)
        v_out_ref[i] = jnp.concatenate([cv_ref[i, TOK:, :], v_new[rows]], axis=0)

    q, logf, kin, vin = _hgrn_gates(proj, _lower_bound(lbp_ref[...]))
    b = _block_cumsum(_tri_matrix(TOK), logf)
    xor = (lax.broadcasted_iota(jnp.int32, (BLK, BLK), 0)
           ^ lax.broadcasted_iota(jnp.int32, (BLK, BLK), 1))
    lane = lax.broadcasted_iota(jnp.int32, (HGRN_DK, BLK), 1)
    oh = []
    for h in range(HGRN_HEADS):
        sl = slice(HGRN_DK * h, HGRN_DK * (h + 1))
        qh, kh, bh = q[:, sl], kin[:, sl], b[:, sl]
        vh = vin[:, sl].astype(BF16)
        a = _intra_scores(qh, kh, bh, TOK // 2, xor)
        o_intra = _dot(a.astype(BF16), vh)
        qe = (qh * jnp.exp(bh)).astype(BF16)
        b3 = bh.reshape(SEQ_PER_STEP, TOK, HGRN_DK)
        b_last = jnp.broadcast_to(b3[:, TOK - 1:TOK, :], b3.shape).reshape(BLK, HGRN_DK)
        kd_t = (kh * jnp.exp(b_last - bh)).T
        dec_t = jnp.exp(b_last).T
        o_rows = []
        for i in range(SEQ_PER_STEP):
            s0 = s0_ref[i, h]
            o_rows.append(_dot(qe[TOK * i:TOK * (i + 1)], s0.astype(BF16)))
            kd_i = jnp.where((lane >= TOK * i) & (lane < TOK * (i + 1)), kd_t, 0.0).astype(BF16)
            s_out_ref[i, h] = s0 * dec_t[:, TOK * i:TOK * i + 1] + _dot(kd_i, vh)
        oh.append(o_intra + jnp.concatenate(o_rows, axis=0))

    y_ref[...] = _mix_out(x, oa, oh, proj, w_out_ref, ang_ref, hng_ref, g2_ref, b2_ref)


def _mixer_sample(x, s0, ck, cv, sink, w_in, w_out, lbp, ang, hng, g2, b2):
    n = x.shape[0]
    nseq = s0.shape[0]
    state_spec = pl.BlockSpec((SEQ_PER_STEP, HGRN_HEADS, HGRN_DK, HGRN_DV), lambda i: (i, 0, 0, 0))
    cache_spec = pl.BlockSpec((SEQ_PER_STEP, WINDOW, KV_WIDTH), lambda i: (i, 0, 0))
    return pl.pallas_call(
        _mixer_sample_kernel,
        grid=(n // BLK,),
        in_specs=[
            pl.BlockSpec(memory_space=pltpu.SMEM),
            pl.BlockSpec((BLK, D_MODEL), lambda i: (i, 0)),
            state_spec, cache_spec, cache_spec,
            _const_spec((D_MODEL, D_IN)),
            _const_spec((D_MIX, D_MODEL)),
            _const_spec((DEPTH + 1, HGRN_KW)),
            _const_spec((1, ATTN_WIDTH)),
            _const_spec((1, HGRN_VW)),
            _const_spec((1, D_MODEL)),
            _const_spec((1, D_MODEL)),
        ],
        out_specs=[
            pl.BlockSpec((BLK, D_MODEL), lambda i: (i, 0)),
            state_spec, cache_spec, cache_spec,
        ],
        out_shape=[
            jax.ShapeDtypeStruct((n, D_MODEL), F32),
            jax.ShapeDtypeStruct((nseq, HGRN_HEADS, HGRN_DK, HGRN_DV), F32),
            jax.ShapeDtypeStruct((nseq, WINDOW, KV_WIDTH), F32),
            jax.ShapeDtypeStruct((nseq, WINDOW, KV_WIDTH), F32),
        ],
        compiler_params=pltpu.CompilerParams(
            dimension_semantics=("arbitrary",), vmem_limit_bytes=VMEM_LIMIT),
        name="mixer_sample",
    )(sink, x, s0, ck, cv, w_in, w_out, lbp, ang, hng, g2, b2)


def kernel(x_prompt, x_sample, state_hgrn, cache_win_k, cache_win_v, ln1_g, ln1_b, ffn1_w13,
           ffn1_w2, w_in, lb_param, attn_sink, attn_norm_g, hgrn_norm_g, w_out, ln2_g, ln2_b,
           ffn2_w13, ffn2_w2, ln3_g, ln3_b):
    nb, seq, d = x_prompt.shape
    ns, tn, _ = x_sample.shape
    assert state_hgrn.shape[0] == DEPTH == 1 and d == D_MODEL
    assert seq % (PROMPT_BLOCKS * BLK) == 0 and tn == TOK and (ns * tn) % BLK == 0
    assert cache_win_k.shape[2] == WINDOW

    w13_1 = ffn1_w13[0].astype(BF16)
    w2_1 = ffn1_w2[0].astype(BF16)
    w13_2 = ffn2_w13[0].astype(BF16)
    w2_2 = ffn2_w2[0].astype(BF16)
    w_in_b = w_in[0].astype(BF16)
    w_out_b = w_out[0].astype(BF16)
    sink = attn_sink[0]
    lbp = lb_param.astype(F32)
    mix_params = (sink, w_in_b, w_out_b, lbp, attn_norm_g, hgrn_norm_g, ln2_g, ln2_b)

    xp = x_prompt.reshape(nb * seq, d)
    xs = x_sample.reshape(ns * tn, d)

    xp = _ffn_ln(xp, w13_1, w2_1, ln1_g, ln1_b, tm=256)
    xs = _ffn_ln(xs, w13_1, w2_1, ln1_g, ln1_b, tm=256)

    xp, sp, kp, vp = _mixer_prompt(xp.reshape(nb, seq, d), *mix_params)
    xs, ss, ks, vs = _mixer_sample(
        xs, state_hgrn[0], cache_win_k[0].reshape(ns, WINDOW, KV_WIDTH),
        cache_win_v[0].reshape(ns, WINDOW, KV_WIDTH), *mix_params)

    xp = _ffn_ln(xp.reshape(nb * seq, d), w13_2, w2_2, ln3_g, ln3_b, tm=256)
    xs = _ffn_ln(xs, w13_2, w2_2, ln3_g, ln3_b, tm=256)

    cache_tail = (WINDOW, ATTN_KV_HEADS, HEAD_DIM)
    return (xp.reshape(nb, seq, d), xs.reshape(ns, tn, d),
            sp[None], kp.reshape((DEPTH, nb) + cache_tail), vp.reshape((DEPTH, nb) + cache_tail),
            ss[None], ks.reshape((DEPTH, ns) + cache_tail), vs.reshape((DEPTH, ns) + cache_tail))
```

```python
import jax
import jax.numpy as jnp
from jax import lax
from jax.experimental import pallas as pl
from jax.experimental.pallas import tpu as pltpu

F32 = jnp.float32
BF16 = jnp.bfloat16

D_MODEL = 1024
DEPTH = 1
ATTN_HEADS = 8
ATTN_KV_HEADS = 2
GROUP = ATTN_HEADS // ATTN_KV_HEADS
HEAD_DIM = 64
ATTN_WIDTH = ATTN_HEADS * HEAD_DIM
KV_WIDTH = ATTN_KV_HEADS * HEAD_DIM
WINDOW = 128
ATTN_SCALE = HEAD_DIM ** -0.5
HGRN_HEADS = 4
HGRN_DK = 128
HGRN_DV = 128
HGRN_KW = HGRN_HEADS * HGRN_DK
HGRN_VW = HGRN_HEADS * HGRN_DV
D_MIX = ATTN_WIDTH + HGRN_VW
D_IN = ATTN_WIDTH + 2 * KV_WIDTH + 2 * HGRN_KW + 2 * HGRN_VW
D_FF = 2816
ALPHA = (2.0 * DEPTH) ** 0.25
LN_EPS = 1e-5
RMS_EPS = 1e-6

OFF_K = ATTN_WIDTH
OFF_V = OFF_K + KV_WIDTH
OFF_HQ = OFF_V + KV_WIDTH
OFF_HF = OFF_HQ + HGRN_KW
OFF_HI = OFF_HF + HGRN_KW
OFF_HG = OFF_HI + HGRN_VW

LANES = 128
SUBLANES = 8
BLK = 128
PROMPT_BLOCKS = 2
PROMPT_TILES = 2
PROJ_CHUNK = 256
TOK = SUBLANES
SEQ_PER_STEP = BLK // TOK
VMEM_LIMIT = 56 * 1024 * 1024

NT_DIMS = (((1,), (1,)), ((), ()))
TN_DIMS = (((0,), (0,)), ((), ()))


def _dot(a, b):
    return jnp.dot(a, b, preferred_element_type=F32)


def _dot_nt(a, b):
    return lax.dot_general(a, b, NT_DIMS, preferred_element_type=F32)


def _dot_tn(a, b):
    return lax.dot_general(a, b, TN_DIMS, preferred_element_type=F32)


def _layer_norm(r, g, b):
    mu = jnp.mean(r, axis=-1, keepdims=True)
    c = r - mu
    var = jnp.mean(c * c, axis=-1, keepdims=True)
    return c * lax.rsqrt(var + LN_EPS) * g + b


def _rms_norm(x, g):
    return x * lax.rsqrt(jnp.mean(x * x, axis=-1, keepdims=True) + RMS_EPS) * g


def _ffn_ln_kernel(x_ref, w13_ref, w2_ref, g_ref, b_ref, o_ref):
    x = x_ref[...]
    h = _dot(x.astype(BF16), w13_ref[...])
    gate = h[:, :D_FF]
    up = h[:, D_FF:]
    act = (jax.nn.silu(gate) * up).astype(BF16)
    y = _dot(act, w2_ref[...])
    o_ref[...] = _layer_norm(ALPHA * x + 0.5 * y, g_ref[...], b_ref[...])


def _const_spec(shape):
    return pl.BlockSpec(shape, lambda *_: (0,) * len(shape), pipeline_mode=pl.Buffered(1))


def _ffn_ln(x, w13, w2, g, b, tm):
    n = x.shape[0]
    return pl.pallas_call(
        _ffn_ln_kernel,
        grid=(n // tm,),
        in_specs=[
            pl.BlockSpec((tm, D_MODEL), lambda i: (i, 0)),
            _const_spec((D_MODEL, 2 * D_FF)),
            _const_spec((D_FF, D_MODEL)),
            _const_spec((1, D_MODEL)),
            _const_spec((1, D_MODEL)),
        ],
        out_specs=pl.BlockSpec((tm, D_MODEL), lambda i: (i, 0)),
        out_shape=jax.ShapeDtypeStruct((n, D_MODEL), F32),
        compiler_params=pltpu.CompilerParams(
            dimension_semantics=("arbitrary",), vmem_limit_bytes=VMEM_LIMIT),
        name="ffn_ln",
    )(x, w13, w2, g, b)


def _lower_bound(lbp):
    mx = jnp.max(lbp, axis=0, keepdims=True)
    e = jnp.exp(lbp - mx)
    return e[0:1] / jnp.sum(e, axis=0, keepdims=True)


def _split3(x):
    hi = x.astype(BF16)
    r = x - hi.astype(F32)
    mid = r.astype(BF16)
    lo = (r - mid.astype(F32)).astype(BF16)
    return hi, mid, lo


def _block_cumsum(tri, g):
    w = g.shape[1]
    parts = jnp.concatenate(_split3(g), axis=1)
    s = _dot(tri, parts)
    return s[:, :w] + s[:, w:2 * w] + s[:, 2 * w:]


def _tri_matrix(group):
    r = lax.broadcasted_iota(jnp.int32, (BLK, BLK), 0)
    c = lax.broadcasted_iota(jnp.int32, (BLK, BLK), 1)
    keep = (c <= r) & ((r ^ c) < group)
    return jnp.where(keep, 1.0, 0.0).astype(BF16)


def _pair_ref_rows(b, m):
    t, w = b.shape
    if m >= SUBLANES:
        b3 = b.reshape(t // (2 * m), 2 * m, w)
        return jnp.broadcast_to(b3[:, m - 1:m, :], b3.shape).reshape(t, w)
    b3 = b.reshape(t // SUBLANES, SUBLANES, w)
    sub = lax.broadcasted_iota(jnp.int32, b3.shape, 1)

    def row(i):
        return jnp.broadcast_to(b3[:, i:i + 1, :], b3.shape)

    if m == 4:
        out = row(3)
    elif m == 2:
        out = jnp.where(sub < 4, row(1), row(5))
    else:
        out = jnp.where(sub < 4, jnp.where(sub < 2, row(0), row(2)),
                        jnp.where(sub < 6, row(4), row(6)))
    return out.reshape(t, w)


def _level_operands(q, k, e, m):
    t, w = q.shape
    if m >= SUBLANES:
        zero = jnp.zeros((m, w), F32)
        qs, ks = [], []
        for blk in range(t // m):
            rows = slice(blk * m, (blk + 1) * m)
            if blk % 2:
                qs.append(q[rows] * e[rows])
                ks.append(zero)
            else:
                qs.append(zero)
                ks.append(k[rows] * e[rows])
        return jnp.concatenate(qs, axis=0).astype(BF16), jnp.concatenate(ks, axis=0).astype(BF16)
    upper = (lax.broadcasted_iota(jnp.int32, q.shape, 0) & m) != 0
    return (jnp.where(upper, q * e, 0.0).astype(BF16), jnp.where(upper, 0.0, k * e).astype(BF16))


def _intra_steps(q, k, b, max_m, xor, out):
    a = jnp.where(xor == 0, jnp.sum(q * k, axis=-1, keepdims=True), 0.0)
    yield
    m = 1
    while m <= max_m:
        e = jnp.exp(-jnp.abs(b - _pair_ref_rows(b, m)))
        qs, ks = _level_operands(q, k, e, m)
        yield
        p = _dot_nt(qs, ks)
        yield
        a = jnp.where(xor >= m, p, a)
        m *= 2
    if 2 * max_m < q.shape[0]:
        a = jnp.where(xor < 2 * max_m, a, 0.0)
    out["a"] = a


def _intra_scores(q, k, b, max_m, xor):
    out = {}
    for _ in _intra_steps(q, k, b, max_m, xor, out):
        pass
    return out["a"]


def _hgrn_gates(proj, lb):
    q = jax.nn.silu(proj[:, OFF_HQ:OFF_HF])
    f = lb + (1.0 - lb) * jax.nn.sigmoid(proj[:, OFF_HF:OFF_HI])
    return q, jnp.log(f), 1.0 - f, proj[:, OFF_HI:OFF_HG]


def _head_queries(qa):
    low = lax.broadcasted_iota(jnp.int32, (qa.shape[0], LANES), 1) < HEAD_DIM
    out = []
    for pair in range(ATTN_HEADS // 2):
        piece = qa[:, LANES * pair:LANES * (pair + 1)] * ATTN_SCALE
        rolled = pltpu.roll(piece, HEAD_DIM, axis=1)
        for half in range(2):
            kv = (2 * pair + half) // GROUP
            src = piece if half == kv else rolled
            z = jnp.where(low, src, 0.0) if kv == 0 else jnp.where(low, 0.0, src)
            out.append(z.astype(BF16))
    return out


def _merge_heads(o):
    low = lax.broadcasted_iota(jnp.int32, o[0].shape, 1) < HEAD_DIM
    cols = []
    for pair in range(ATTN_HEADS // 2):
        kv = (2 * pair) // GROUP
        even, odd = o[2 * pair], o[2 * pair + 1]
        if kv == 0:
            odd = pltpu.roll(odd, HEAD_DIM, axis=1)
        else:
            even = pltpu.roll(even, HEAD_DIM, axis=1)
        cols.append(jnp.where(low, even, odd))
    return jnp.concatenate(cols, axis=1)


def _slope(head):
    return 2.0 ** (-8.0 * (head + 1) / ATTN_HEADS)


def _attend_steps(qz, k_all, v_all, sink_ref, limit, out):
    tq = qz[0].shape[0]
    nk = k_all.shape[0]
    s_all = _dot_nt(jnp.concatenate(qz, axis=0), k_all.astype(BF16))
    trow = lax.broadcasted_iota(jnp.int32, (tq, nk), 0)
    ccol = lax.broadcasted_iota(jnp.int32, (tq, nk), 1)
    dist = trow + WINDOW - ccol
    visible = dist.astype(jnp.uint32) < limit.astype(jnp.uint32)
    distf = dist.astype(F32)
    yield
    ps, dens = [], []
    for head in range(ATTN_HEADS):
        sink = sink_ref[head]
        s = s_all[tq * head:tq * (head + 1)]
        s = jnp.where(visible, s - _slope(head) * distf, -jnp.inf)
        mx = jnp.maximum(jnp.max(s, axis=-1, keepdims=True), sink)
        yield
        p = jnp.exp(s - mx)
        dens.append(jnp.sum(p, axis=-1, keepdims=True) + jnp.exp(sink - mx))
        ps.append(p.astype(BF16))
        yield
    o_all = _dot(jnp.concatenate(ps, axis=0), v_all.astype(BF16))
    yield
    out["oa"] = _merge_heads([o_all[tq * h:tq * (h + 1)] / dens[h] for h in range(ATTN_HEADS)])


def _attend_sample(qz, k_new, v_new, ck_ref, cv_ref, sink_ref):
    rows = qz[0].shape[0]
    k_new = k_new.astype(BF16)
    v_new = v_new.astype(BF16)
    sn_all = _dot_nt(jnp.concatenate(qz, axis=0), k_new)
    sc_seq = []
    for i in range(SEQ_PER_STEP):
        lhs = jnp.concatenate([z[TOK * i:TOK * (i + 1)] for z in qz], axis=0)
        sc_seq.append(_dot_nt(lhs, ck_ref[i].astype(BF16)))

    r = lax.broadcasted_iota(jnp.int32, (rows, rows), 0)
    c = lax.broadcasted_iota(jnp.int32, (rows, rows), 1)
    new_visible = ((r ^ c) < TOK) & (c <= r)
    new_dist = (r - c).astype(F32)
    tok = lax.broadcasted_iota(jnp.int32, (rows, WINDOW), 0) & (TOK - 1)
    key = lax.broadcasted_iota(jnp.int32, (rows, WINDOW), 1)
    old_visible = key > tok
    old_dist = (WINDOW + tok - key).astype(F32)

    pn, pc, dens = [], [], []
    for head in range(ATTN_HEADS):
        sink = sink_ref[head]
        slope = _slope(head)
        sn = sn_all[rows * head:rows * (head + 1)]
        sc = jnp.concatenate([s[TOK * head:TOK * (head + 1)] for s in sc_seq], axis=0)
        sn = jnp.where(new_visible, sn - slope * new_dist, -jnp.inf)
        sc = jnp.where(old_visible, sc - slope * old_dist, -jnp.inf)
        mx = jnp.maximum(jnp.maximum(jnp.max(sn, axis=-1, keepdims=True),
                                     jnp.max(sc, axis=-1, keepdims=True)), sink)
        en = jnp.exp(sn - mx)
        ec = jnp.exp(sc - mx)
        dens.append(jnp.sum(en, axis=-1, keepdims=True) + jnp.sum(ec, axis=-1, keepdims=True)
                    + jnp.exp(sink - mx))
        pn.append(en.astype(BF16))
        pc.append(ec.astype(BF16))

    on_all = _dot(jnp.concatenate(pn, axis=0), v_new)
    oc_seq = []
    for i in range(SEQ_PER_STEP):
        lhs = jnp.concatenate([p[TOK * i:TOK * (i + 1)] for p in pc], axis=0)
        oc_seq.append(_dot(lhs, cv_ref[i].astype(BF16)))
    o = []
    for head in range(ATTN_HEADS):
        oc = jnp.concatenate([s[TOK * head:TOK * (head + 1)] for s in oc_seq], axis=0)
        o.append((on_all[rows * head:rows * (head + 1)] + oc) / dens[head])
    return _merge_heads(o)


def _mix_out(x, oa, oh, proj, w_out_ref, ang_ref, hng_ref, g2_ref, b2_ref):
    oa = _rms_norm(oa, ang_ref[...])
    hng = hng_ref[...]
    gate = jax.nn.silu(proj[:, OFF_HG:])
    ohn = [_rms_norm(oh[h], hng[:, HGRN_DV * h:HGRN_DV * (h + 1)]) for h in range(HGRN_HEADS)]
    mix = jnp.concatenate([oa, jnp.concatenate(ohn, axis=1) * gate], axis=1).astype(BF16)
    y = _dot(mix, w_out_ref[...])
    return _layer_norm(ALPHA * x + y, g2_ref[...], b2_ref[...])


_DONE = object()


def _round_robin(streams):
    live = list(streams)
    rounds = 0
    while live:
        live = [g for g in live if next(g, _DONE) is not _DONE]
        rounds += 1
        assert rounds < 100000, "a stream waits for something no other stream produces"


def _proj_stream(ctx, x, w_in_ref, may_start):
    while not may_start():
        yield
    xb = x.astype(BF16)
    chunks = []
    for c in range(D_IN // PROJ_CHUNK):
        chunks.append(_dot(xb, w_in_ref[:, PROJ_CHUNK * c:PROJ_CHUNK * (c + 1)]))
        yield
    ctx["proj"] = jnp.concatenate(chunks, axis=1)


def _prepare_stream(ctx, lb, tri):
    while "proj" not in ctx:
        yield
    proj = ctx["proj"]
    ctx["qz"] = _head_queries(proj[:, :ATTN_WIDTH])
    yield
    ctx["q"], logf, ctx["kin"], ctx["vin"] = _hgrn_gates(proj, lb)
    yield
    ctx["b"] = [_block_cumsum(tri, logf[BLK * blk:BLK * (blk + 1)]) for blk in range(PROMPT_BLOCKS)]
    ctx["ready"] = True


def _attention_stream(ctx, blk, prev_kv, limit, sink_ref):
    while "ready" not in ctx:
        yield
    rows = slice(BLK * blk, BLK * (blk + 1))
    k_prev, v_prev = prev_kv()
    k_all = jnp.concatenate([k_prev, ctx["proj"][rows, OFF_K:OFF_V]], axis=0)
    v_all = jnp.concatenate([v_prev, ctx["proj"][rows, OFF_V:OFF_HQ]], axis=0)
    out = {}
    yield from _attend_steps([z[rows] for z in ctx["qz"]], k_all, v_all, sink_ref, limit, out)
    ctx["oa"][blk] = out["oa"]


def _hgrn_stream(ctx, blk, h, state, turn, xor):
    while "ready" not in ctx:
        yield
    rows = slice(BLK * blk, BLK * (blk + 1))
    sl = slice(HGRN_DK * h, HGRN_DK * (h + 1))
    qh, kh, bh = ctx["q"][rows, sl], ctx["kin"][rows, sl], ctx["b"][blk][:, sl]
    vh = ctx["vin"][rows, sl].astype(BF16)
    scores = {}
    yield from _intra_steps(qh, kh, bh, BLK // 2, xor, scores)
    o_intra = _dot(scores["a"].astype(BF16), vh)
    yield
    b_last = bh[BLK - 1:BLK, :]
    qe = (qh * jnp.exp(bh)).astype(BF16)
    kd = (kh * jnp.exp(b_last - bh)).astype(BF16)
    yield
    while state["turn"][h] != turn:
        yield
    st = state["st"][h]
    o_inter = _dot_nt(qe, st.astype(BF16))
    update = _dot_tn(vh, kd)
    yield
    state["st"][h] = st * jnp.exp(b_last) + update
    state["turn"][h] = turn + 1
    ctx["oh"][blk][h] = o_intra + o_inter


def _mix_out_stream(ctx, x, y_store, w_out_ref, ang_ref, hng_ref, g2_ref, b2_ref):
    while any(o is None for o in ctx["oa"]) or any(o is None for blk in ctx["oh"] for o in blk):
        yield
    oa = jnp.concatenate(ctx["oa"], axis=0)
    oh = [jnp.concatenate([ctx["oh"][blk][h] for blk in range(PROMPT_BLOCKS)], axis=0)
          for h in range(HGRN_HEADS)]
    y_store(_mix_out(x, oa, oh, ctx["proj"], w_out_ref, ang_ref, hng_ref, g2_ref, b2_ref))


def _mixer_prompt_kernel(sink_ref, x_ref, w_in_ref, w_out_ref, lbp_ref, ang_ref, hng_ref, g2_ref,
                         b2_ref, y_ref, s_out_ref, k_out_ref, v_out_ref, st_ref, kv_ref):
    j = pl.program_id(1)

    @pl.when(j == 0)
    def _():
        st_ref[...] = jnp.zeros_like(st_ref)
        kv_ref[...] = jnp.zeros_like(kv_ref)

    lb = _lower_bound(lbp_ref[...])
    tri = _tri_matrix(BLK)
    xor = (lax.broadcasted_iota(jnp.int32, (BLK, BLK), 0)
           ^ lax.broadcasted_iota(jnp.int32, (BLK, BLK), 1))
    trow = lax.broadcasted_iota(jnp.int32, (BLK, 2 * WINDOW), 0)
    full = jnp.full(trow.shape, WINDOW, jnp.int32)
    tile_rows = PROMPT_BLOCKS * BLK

    state = {"st": [st_ref[h] for h in range(HGRN_HEADS)], "turn": [0] * HGRN_HEADS}
    ctxs = [{"oa": [None] * PROMPT_BLOCKS,
             "oh": [[None] * HGRN_HEADS for _ in range(PROMPT_BLOCKS)]} for _ in range(PROMPT_TILES)]

    def kv_block(tile, blk):
        rows = slice(BLK * blk, BLK * (blk + 1))
        proj = ctxs[tile]["proj"]
        return proj[rows, OFF_K:OFF_V], proj[rows, OFF_V:OFF_HQ]

    streams = []
    for tile in range(PROMPT_TILES):
        ctx = ctxs[tile]
        trows = slice(tile_rows * tile, tile_rows * (tile + 1))
        x = x_ref[0, trows, :]
        may_start = (lambda: True) if tile == 0 else (lambda prev=ctxs[tile - 1]: "ready" in prev)
        streams.append(_proj_stream(ctx, x, w_in_ref, may_start))
        streams.append(_prepare_stream(ctx, lb, tri))
        for blk in range(PROMPT_BLOCKS):
            if tile == 0 and blk == 0:
                prev_kv = lambda: (kv_ref[0], kv_ref[1])
                limit = jnp.where(j > 0, WINDOW, trow + 1)
            else:
                g = tile * PROMPT_BLOCKS + blk - 1
                prev_kv = lambda g=g: kv_block(g // PROMPT_BLOCKS, g % PROMPT_BLOCKS)
                limit = full
            streams.append(_attention_stream(ctx, blk, prev_kv, limit, sink_ref))
            for h in range(HGRN_HEADS):
                streams.append(_hgrn_stream(ctx, blk, h, state, tile * PROMPT_BLOCKS + blk, xor))

        def y_store(y, trows=trows):
            y_ref[0, trows, :] = y

        streams.append(_mix_out_stream(ctx, x, y_store, w_out_ref, ang_ref, hng_ref, g2_ref, b2_ref))
    _round_robin(streams)

    k_last, v_last = kv_block(PROMPT_TILES - 1, PROMPT_BLOCKS - 1)
    kv_ref[0] = k_last
    kv_ref[1] = v_last
    for h in range(HGRN_HEADS):
        st_ref[h] = state["st"][h]

    @pl.when(j == pl.num_programs(1) - 1)
    def _():
        k_out_ref[0] = k_last
        v_out_ref[0] = v_last
        for h in range(HGRN_HEADS):
            s_out_ref[0, h] = state["st"][h].T


def _mixer_prompt(x, sink, w_in, w_out, lbp, ang, hng, g2, b2):
    nb, seq, _ = x.shape
    rows = PROMPT_TILES * PROMPT_BLOCKS * BLK
    return pl.pallas_call(
        _mixer_prompt_kernel,
        grid=(nb, seq // rows),
        in_specs=[
            pl.BlockSpec(memory_space=pltpu.SMEM),
            pl.BlockSpec((1, rows, D_MODEL), lambda b, j: (b, j, 0)),
            _const_spec((D_MODEL, D_IN)),
            _const_spec((D_MIX, D_MODEL)),
            _const_spec((DEPTH + 1, HGRN_KW)),
            _const_spec((1, ATTN_WIDTH)),
            _const_spec((1, HGRN_VW)),
            _const_spec((1, D_MODEL)),
            _const_spec((1, D_MODEL)),
        ],
        out_specs=[
            pl.BlockSpec((1, rows, D_MODEL), lambda b, j: (b, j, 0)),
            pl.BlockSpec((1, HGRN_HEADS, HGRN_DK, HGRN_DV), lambda b, j: (b, 0, 0, 0)),
            pl.BlockSpec((1, WINDOW, KV_WIDTH), lambda b, j: (b, 0, 0)),
            pl.BlockSpec((1, WINDOW, KV_WIDTH), lambda b, j: (b, 0, 0)),
        ],
        out_shape=[
            jax.ShapeDtypeStruct((nb, seq, D_MODEL), F32),
            jax.ShapeDtypeStruct((nb, HGRN_HEADS, HGRN_DK, HGRN_DV), F32),
            jax.ShapeDtypeStruct((nb, WINDOW, KV_WIDTH), F32),
            jax.ShapeDtypeStruct((nb, WINDOW, KV_WIDTH), F32),
        ],
        scratch_shapes=[
            pltpu.VMEM((HGRN_HEADS, HGRN_DV, HGRN_DK), F32),
            pltpu.VMEM((2, WINDOW, KV_WIDTH), F32),
        ],
        compiler_params=pltpu.CompilerParams(
            dimension_semantics=("arbitrary", "arbitrary"), vmem_limit_bytes=VMEM_LIMIT),
        name="mixer_prompt",
    )(sink, x, w_in, w_out, lbp, ang, hng, g2, b2)


def _mixer_sample_kernel(sink_ref, x_ref, s0_ref, ck_ref, cv_ref, w_in_ref, w_out_ref, lbp_ref,
                         ang_ref, hng_ref, g2_ref, b2_ref, y_ref, s_out_ref, k_out_ref, v_out_ref):
    x = x_ref[...]
    proj = _dot(x.astype(BF16), w_in_ref[...])
    k_new = proj[:, OFF_K:OFF_V]
    v_new = proj[:, OFF_V:OFF_HQ]

    oa = _attend_sample(_head_queries(proj[:, :ATTN_WIDTH]), k_new, v_new, ck_ref, cv_ref, sink_ref)
    for i in range(SEQ_PER_STEP):
        rows = slice(TOK * i, TOK * (i + 1))
        k_out_ref[i] = jnp.concatenate([ck_ref[i, TOK:, :], k_new[rows]], axis=0)
        v_out_ref[i] = jnp.concatenate([cv_ref[i, TOK:, :], v_new[rows]], axis=0)

    q, logf, kin, vin = _hgrn_gates(proj, _lower_bound(lbp_ref[...]))
    b = _block_cumsum(_tri_matrix(TOK), logf)
    xor = (lax.broadcasted_iota(jnp.int32, (BLK, BLK), 0)
           ^ lax.broadcasted_iota(jnp.int32, (BLK, BLK), 1))
    lane = lax.broadcasted_iota(jnp.int32, (HGRN_DK, BLK), 1)
    oh = []
    for h in range(HGRN_HEADS):
        sl = slice(HGRN_DK * h, HGRN_DK * (h + 1))
        qh, kh, bh = q[:, sl], kin[:, sl], b[:, sl]
        vh = vin[:, sl].astype(BF16)
        a = _intra_scores(qh, kh, bh, TOK // 2, xor)
        o_intra = _dot(a.astype(BF16), vh)
        qe = (qh * jnp.exp(bh)).astype(BF16)
        b3 = bh.reshape(SEQ_PER_STEP, TOK, HGRN_DK)
        b_last = jnp.broadcast_to(b3[:, TOK - 1:TOK, :], b3.shape).reshape(BLK, HGRN_DK)
        kd_t = (kh * jnp.exp(b_last - bh)).T
        dec_t = jnp.exp(b_last).T
        o_rows = []
        for i in range(SEQ_PER_STEP):
            s0 = s0_ref[i, h]
            o_rows.append(_dot(qe[TOK * i:TOK * (i + 1)], s0.astype(BF16)))
            kd_i = jnp.where((lane >= TOK * i) & (lane < TOK * (i + 1)), kd_t, 0.0).astype(BF16)
            s_out_ref[i, h] = s0 * dec_t[:, TOK * i:TOK * i + 1] + _dot(kd_i, vh)
        oh.append(o_intra + jnp.concatenate(o_rows, axis=0))

    y_ref[...] = _mix_out(x, oa, oh, proj, w_out_ref, ang_ref, hng_ref, g2_ref, b2_ref)


def _mixer_sample(x, s0, ck, cv, sink, w_in, w_out, lbp, ang, hng, g2, b2):
    n = x.shape[0]
    nseq = s0.shape[0]
    state_spec = pl.BlockSpec((SEQ_PER_STEP, HGRN_HEADS, HGRN_DK, HGRN_DV), lambda i: (i, 0, 0, 0))
    cache_spec = pl.BlockSpec((SEQ_PER_STEP, WINDOW, KV_WIDTH), lambda i: (i, 0, 0))
    return pl.pallas_call(
        _mixer_sample_kernel,
        grid=(n // BLK,),
        in_specs=[
            pl.BlockSpec(memory_space=pltpu.SMEM),
            pl.BlockSpec((BLK, D_MODEL), lambda i: (i, 0)),
            state_spec, cache_spec, cache_spec,
            _const_spec((D_MODEL, D_IN)),
            _const_spec((D_MIX, D_MODEL)),
            _const_spec((DEPTH + 1, HGRN_KW)),
            _const_spec((1, ATTN_WIDTH)),
            _const_spec((1, HGRN_VW)),
            _const_spec((1, D_MODEL)),
            _const_spec((1, D_MODEL)),
        ],
        out_specs=[
            pl.BlockSpec((BLK, D_MODEL), lambda i: (i, 0)),
            state_spec, cache_spec, cache_spec,
        ],
        out_shape=[
            jax.ShapeDtypeStruct((n, D_MODEL), F32),
            jax.ShapeDtypeStruct((nseq, HGRN_HEADS, HGRN_DK, HGRN_DV), F32),
            jax.ShapeDtypeStruct((nseq, WINDOW, KV_WIDTH), F32),
            jax.ShapeDtypeStruct((nseq, WINDOW, KV_WIDTH), F32),
        ],
        compiler_params=pltpu.CompilerParams(
            dimension_semantics=("arbitrary",), vmem_limit_bytes=VMEM_LIMIT),
        name="mixer_sample",
    )(sink, x, s0, ck, cv, w_in, w_out, lbp, ang, hng, g2, b2)


def kernel(x_prompt, x_sample, state_hgrn, cache_win_k, cache_win_v, ln1_g, ln1_b, ffn1_w13,
           ffn1_w2, w_in, lb_param, attn_sink, attn_norm_g, hgrn_norm_g, w_out, ln2_g, ln2_b,
           ffn2_w13, ffn2_w2, ln3_g, ln3_b):
    nb, seq, d = x_prompt.shape
    ns, tn, _ = x_sample.shape
    assert state_hgrn.shape[0] == DEPTH == 1 and d == D_MODEL
    assert seq % (PROMPT_TILES * PROMPT_BLOCKS * BLK) == 0 and tn == TOK and (ns * tn) % BLK == 0
    assert cache_win_k.shape[2] == WINDOW

    w13_1 = ffn1_w13[0].astype(BF16)
    w2_1 = ffn1_w2[0].astype(BF16)
    w13_2 = ffn2_w13[0].astype(BF16)
    w2_2 = ffn2_w2[0].astype(BF16)
    w_in_b = w_in[0].astype(BF16)
    w_out_b = w_out[0].astype(BF16)
    sink = attn_sink[0]
    lbp = lb_param.astype(F32)
    mix_params = (sink, w_in_b, w_out_b, lbp, attn_norm_g, hgrn_norm_g, ln2_g, ln2_b)

    xp = x_prompt.reshape(nb * seq, d)
    xs = x_sample.reshape(ns * tn, d)

    xp = _ffn_ln(xp, w13_1, w2_1, ln1_g, ln1_b, tm=256)
    xs = _ffn_ln(xs, w13_1, w2_1, ln1_g, ln1_b, tm=256)

    xp, sp, kp, vp = _mixer_prompt(xp.reshape(nb, seq, d), *mix_params)
    xs, ss, ks, vs = _mixer_sample(
        xs, state_hgrn[0], cache_win_k[0].reshape(ns, WINDOW, KV_WIDTH),
        cache_win_v[0].reshape(ns, WINDOW, KV_WIDTH), *mix_params)

    xp = _ffn_ln(xp.reshape(nb * seq, d), w13_2, w2_2, ln3_g, ln3_b, tm=256)
    xs = _ffn_ln(xs, w13_2, w2_2, ln3_g, ln3_b, tm=256)

    cache_tail = (WINDOW, ATTN_KV_HEADS, HEAD_DIM)
    return (xp.reshape(nb, seq, d), xs.reshape(ns, tn, d),
            sp[None], kp.reshape((DEPTH, nb) + cache_tail), vp.reshape((DEPTH, nb) + cache_tail),
            ss[None], ks.reshape((DEPTH, ns) + cache_tail), vs.reshape((DEPTH, ns) + cache_tail))
```

```python
import functools

import jax
import jax.numpy as jnp
from jax import lax
from jax.experimental import pallas as pl
from jax.experimental.pallas import tpu as pltpu

F32 = jnp.float32
BF16 = jnp.bfloat16

D_MODEL = 1024
DEPTH = 1
ATTN_HEADS = 8
ATTN_KV_HEADS = 2
GROUP = ATTN_HEADS // ATTN_KV_HEADS
HEAD_DIM = 64
ATTN_WIDTH = ATTN_HEADS * HEAD_DIM
KV_WIDTH = ATTN_KV_HEADS * HEAD_DIM
WINDOW = 128
ATTN_SCALE = HEAD_DIM ** -0.5
HGRN_HEADS = 4
HGRN_DK = 128
HGRN_DV = 128
HGRN_KW = HGRN_HEADS * HGRN_DK
HGRN_VW = HGRN_HEADS * HGRN_DV
D_MIX = ATTN_WIDTH + HGRN_VW
D_IN = ATTN_WIDTH + 2 * KV_WIDTH + 2 * HGRN_KW + 2 * HGRN_VW
D_FF = 2816
ALPHA = (2.0 * DEPTH) ** 0.25
LN_EPS = 1e-5
RMS_EPS = 1e-6

OFF_K = ATTN_WIDTH
OFF_V = OFF_K + KV_WIDTH
OFF_HQ = OFF_V + KV_WIDTH
OFF_HF = OFF_HQ + HGRN_KW
OFF_HI = OFF_HF + HGRN_KW
OFF_HG = OFF_HI + HGRN_VW

LANES = 128
SUBLANES = 8
BLK = 128
PROMPT_BLOCKS = 2
PROMPT_TILES = 2
PROJ_CHUNK = 256
TOK = SUBLANES
SEQ_PER_STEP = BLK // TOK
VMEM_LIMIT = 56 * 1024 * 1024

NT_DIMS = (((1,), (1,)), ((), ()))
TN_DIMS = (((0,), (0,)), ((), ()))


def _dot(a, b):
    return jnp.dot(a, b, preferred_element_type=F32)


def _dot_nt(a, b):
    return lax.dot_general(a, b, NT_DIMS, preferred_element_type=F32)


def _dot_tn(a, b):
    return lax.dot_general(a, b, TN_DIMS, preferred_element_type=F32)


def _layer_norm(r, g, b):
    mu = jnp.mean(r, axis=-1, keepdims=True)
    c = r - mu
    var = jnp.mean(c * c, axis=-1, keepdims=True)
    return c * lax.rsqrt(var + LN_EPS) * g + b


def _rms_norm(x, g):
    return x * lax.rsqrt(jnp.mean(x * x, axis=-1, keepdims=True) + RMS_EPS) * g


def _ffn_ln_kernel(x_ref, w13_ref, w2_ref, g_ref, b_ref, o_ref):
    x = x_ref[...]
    h = _dot(x.astype(BF16), w13_ref[...])
    gate = h[:, :D_FF]
    up = h[:, D_FF:]
    act = (jax.nn.silu(gate) * up).astype(BF16)
    y = _dot(act, w2_ref[...])
    o_ref[...] = _layer_norm(ALPHA * x + 0.5 * y, g_ref[...], b_ref[...])


def _const_spec(shape):
    return pl.BlockSpec(shape, lambda *_: (0,) * len(shape), pipeline_mode=pl.Buffered(1))


def _ffn_ln(x, w13, w2, g, b, tm):
    n = x.shape[0]
    return pl.pallas_call(
        _ffn_ln_kernel,
        grid=(n // tm,),
        in_specs=[
            pl.BlockSpec((tm, D_MODEL), lambda i: (i, 0)),
            _const_spec((D_MODEL, 2 * D_FF)),
            _const_spec((D_FF, D_MODEL)),
            _const_spec((1, D_MODEL)),
            _const_spec((1, D_MODEL)),
        ],
        out_specs=pl.BlockSpec((tm, D_MODEL), lambda i: (i, 0)),
        out_shape=jax.ShapeDtypeStruct((n, D_MODEL), F32),
        compiler_params=pltpu.CompilerParams(
            dimension_semantics=("arbitrary",), vmem_limit_bytes=VMEM_LIMIT),
        name="ffn_ln",
    )(x, w13, w2, g, b)


def _lower_bound(lbp):
    mx = jnp.max(lbp, axis=0, keepdims=True)
    e = jnp.exp(lbp - mx)
    return e[0:1] / jnp.sum(e, axis=0, keepdims=True)


def _split3(x):
    hi = x.astype(BF16)
    r = x - hi.astype(F32)
    mid = r.astype(BF16)
    lo = (r - mid.astype(F32)).astype(BF16)
    return hi, mid, lo


def _block_cumsum(tri, g):
    w = g.shape[1]
    parts = jnp.concatenate(_split3(g), axis=1)
    s = _dot(tri, parts)
    return s[:, :w] + s[:, w:2 * w] + s[:, 2 * w:]


def _tri_matrix(group):
    r = lax.broadcasted_iota(jnp.int32, (BLK, BLK), 0)
    c = lax.broadcasted_iota(jnp.int32, (BLK, BLK), 1)
    keep = (c <= r) & ((r ^ c) < group)
    return jnp.where(keep, 1.0, 0.0).astype(BF16)


def _pair_ref_rows(b, m):
    t, w = b.shape
    if m >= SUBLANES:
        b3 = b.reshape(t // (2 * m), 2 * m, w)
        return jnp.broadcast_to(b3[:, m - 1:m, :], b3.shape).reshape(t, w)
    b3 = b.reshape(t // SUBLANES, SUBLANES, w)
    sub = lax.broadcasted_iota(jnp.int32, b3.shape, 1)

    def row(i):
        return jnp.broadcast_to(b3[:, i:i + 1, :], b3.shape)

    if m == 4:
        out = row(3)
    elif m == 2:
        out = jnp.where(sub < 4, row(1), row(5))
    else:
        out = jnp.where(sub < 4, jnp.where(sub < 2, row(0), row(2)),
                        jnp.where(sub < 6, row(4), row(6)))
    return out.reshape(t, w)


def _level_operands(q, k, e, m):
    t, w = q.shape
    if m >= SUBLANES:
        zero = jnp.zeros((m, w), F32)
        qs, ks = [], []
        for blk in range(t // m):
            rows = slice(blk * m, (blk + 1) * m)
            if blk % 2:
                qs.append(q[rows] * e[rows])
                ks.append(zero)
            else:
                qs.append(zero)
                ks.append(k[rows] * e[rows])
        return jnp.concatenate(qs, axis=0).astype(BF16), jnp.concatenate(ks, axis=0).astype(BF16)
    upper = (lax.broadcasted_iota(jnp.int32, q.shape, 0) & m) != 0
    return (jnp.where(upper, q * e, 0.0).astype(BF16), jnp.where(upper, 0.0, k * e).astype(BF16))


def _intra_steps(q, k, b, max_m, xor, out):
    a = jnp.where(xor == 0, jnp.sum(q * k, axis=-1, keepdims=True), 0.0)
    yield
    m = 1
    while m <= max_m:
        e = jnp.exp(-jnp.abs(b - _pair_ref_rows(b, m)))
        qs, ks = _level_operands(q, k, e, m)
        yield
        p = _dot_nt(qs, ks)
        yield
        a = jnp.where(xor >= m, p, a)
        m *= 2
    if 2 * max_m < q.shape[0]:
        a = jnp.where(xor < 2 * max_m, a, 0.0)
    out["a"] = a


def _intra_scores(q, k, b, max_m, xor):
    out = {}
    for _ in _intra_steps(q, k, b, max_m, xor, out):
        pass
    return out["a"]


def _hgrn_gates(proj, lb):
    q = jax.nn.silu(proj[:, OFF_HQ:OFF_HF])
    f = lb + (1.0 - lb) * jax.nn.sigmoid(proj[:, OFF_HF:OFF_HI])
    return q, jnp.log(f), 1.0 - f, proj[:, OFF_HI:OFF_HG]


def _head_queries(qa):
    low = lax.broadcasted_iota(jnp.int32, (qa.shape[0], LANES), 1) < HEAD_DIM
    out = []
    for pair in range(ATTN_HEADS // 2):
        piece = qa[:, LANES * pair:LANES * (pair + 1)] * ATTN_SCALE
        rolled = pltpu.roll(piece, HEAD_DIM, axis=1)
        for half in range(2):
            kv = (2 * pair + half) // GROUP
            src = piece if half == kv else rolled
            z = jnp.where(low, src, 0.0) if kv == 0 else jnp.where(low, 0.0, src)
            out.append(z.astype(BF16))
    return out


def _merge_heads(o):
    low = lax.broadcasted_iota(jnp.int32, o[0].shape, 1) < HEAD_DIM
    cols = []
    for pair in range(ATTN_HEADS // 2):
        kv = (2 * pair) // GROUP
        even, odd = o[2 * pair], o[2 * pair + 1]
        if kv == 0:
            odd = pltpu.roll(odd, HEAD_DIM, axis=1)
        else:
            even = pltpu.roll(even, HEAD_DIM, axis=1)
        cols.append(jnp.where(low, even, odd))
    return jnp.concatenate(cols, axis=1)


def _slope(head):
    return 2.0 ** (-8.0 * (head + 1) / ATTN_HEADS)


def _attend_steps(qz, k_all, v_all, sink_ref, limit, out):
    tq = qz[0].shape[0]
    nk = k_all.shape[0]
    s_all = _dot_nt(jnp.concatenate(qz, axis=0), k_all.astype(BF16))
    trow = lax.broadcasted_iota(jnp.int32, (tq, nk), 0)
    ccol = lax.broadcasted_iota(jnp.int32, (tq, nk), 1)
    dist = trow + WINDOW - ccol
    visible = dist.astype(jnp.uint32) < limit.astype(jnp.uint32)
    distf = dist.astype(F32)
    yield
    ps, dens = [], []
    for head in range(ATTN_HEADS):
        sink = sink_ref[head]
        s = s_all[tq * head:tq * (head + 1)]
        s = jnp.where(visible, s - _slope(head) * distf, -jnp.inf)
        mx = jnp.maximum(jnp.max(s, axis=-1, keepdims=True), sink)
        yield
        p = jnp.exp(s - mx)
        dens.append(jnp.sum(p, axis=-1, keepdims=True) + jnp.exp(sink - mx))
        ps.append(p.astype(BF16))
        yield
    o_all = _dot(jnp.concatenate(ps, axis=0), v_all.astype(BF16))
    yield
    out["oa"] = _merge_heads([o_all[tq * h:tq * (h + 1)] / dens[h] for h in range(ATTN_HEADS)])


def _attend_sample(qz, k_new, v_new, ck_ref, cv_ref, sink_ref):
    rows = qz[0].shape[0]
    k_new = k_new.astype(BF16)
    v_new = v_new.astype(BF16)
    sn_all = _dot_nt(jnp.concatenate(qz, axis=0), k_new)
    sc_seq = []
    for i in range(SEQ_PER_STEP):
        lhs = jnp.concatenate([z[TOK * i:TOK * (i + 1)] for z in qz], axis=0)
        sc_seq.append(_dot_nt(lhs, ck_ref[i].astype(BF16)))

    r = lax.broadcasted_iota(jnp.int32, (rows, rows), 0)
    c = lax.broadcasted_iota(jnp.int32, (rows, rows), 1)
    new_visible = ((r ^ c) < TOK) & (c <= r)
    new_dist = (r - c).astype(F32)
    tok = lax.broadcasted_iota(jnp.int32, (rows, WINDOW), 0) & (TOK - 1)
    key = lax.broadcasted_iota(jnp.int32, (rows, WINDOW), 1)
    old_visible = key > tok
    old_dist = (WINDOW + tok - key).astype(F32)

    pn, pc, dens = [], [], []
    for head in range(ATTN_HEADS):
        sink = sink_ref[head]
        slope = _slope(head)
        sn = sn_all[rows * head:rows * (head + 1)]
        sc = jnp.concatenate([s[TOK * head:TOK * (head + 1)] for s in sc_seq], axis=0)
        sn = jnp.where(new_visible, sn - slope * new_dist, -jnp.inf)
        sc = jnp.where(old_visible, sc - slope * old_dist, -jnp.inf)
        mx = jnp.maximum(jnp.maximum(jnp.max(sn, axis=-1, keepdims=True),
                                     jnp.max(sc, axis=-1, keepdims=True)), sink)
        en = jnp.exp(sn - mx)
        ec = jnp.exp(sc - mx)
        dens.append(jnp.sum(en, axis=-1, keepdims=True) + jnp.sum(ec, axis=-1, keepdims=True)
                    + jnp.exp(sink - mx))
        pn.append(en.astype(BF16))
        pc.append(ec.astype(BF16))

    on_all = _dot(jnp.concatenate(pn, axis=0), v_new)
    oc_seq = []
    for i in range(SEQ_PER_STEP):
        lhs = jnp.concatenate([p[TOK * i:TOK * (i + 1)] for p in pc], axis=0)
        oc_seq.append(_dot(lhs, cv_ref[i].astype(BF16)))
    o = []
    for head in range(ATTN_HEADS):
        oc = jnp.concatenate([s[TOK * head:TOK * (head + 1)] for s in oc_seq], axis=0)
        o.append((on_all[rows * head:rows * (head + 1)] + oc) / dens[head])
    return _merge_heads(o)


def _mix_out(x, oa, oh, proj, w_out_ref, ang_ref, hng_ref, g2_ref, b2_ref):
    oa = _rms_norm(oa, ang_ref[...])
    hng = hng_ref[...]
    gate = jax.nn.silu(proj[:, OFF_HG:])
    ohn = [_rms_norm(oh[h], hng[:, HGRN_DV * h:HGRN_DV * (h + 1)]) for h in range(HGRN_HEADS)]
    mix = jnp.concatenate([oa, jnp.concatenate(ohn, axis=1) * gate], axis=1).astype(BF16)
    y = _dot(mix, w_out_ref[...])
    return _layer_norm(ALPHA * x + y, g2_ref[...], b2_ref[...])


_DONE = object()


def _round_robin(streams):
    live = list(streams)
    rounds = 0
    while live:
        live = [g for g in live if next(g, _DONE) is not _DONE]
        rounds += 1
        assert rounds < 100000, "a stream waits for something no other stream produces"


def _proj_stream(ctx, x, w_in_ref, may_start):
    while not may_start():
        yield
    xb = x.astype(BF16)
    chunks = []
    for c in range(D_IN // PROJ_CHUNK):
        chunks.append(_dot(xb, w_in_ref[:, PROJ_CHUNK * c:PROJ_CHUNK * (c + 1)]))
        yield
    ctx["proj"] = jnp.concatenate(chunks, axis=1)


def _prepare_stream(ctx, lb, tri):
    while "proj" not in ctx:
        yield
    proj = ctx["proj"]
    ctx["qz"] = _head_queries(proj[:, :ATTN_WIDTH])
    yield
    ctx["q"], logf, ctx["kin"], ctx["vin"] = _hgrn_gates(proj, lb)
    yield
    ctx["b"] = [_block_cumsum(tri, logf[BLK * blk:BLK * (blk + 1)]) for blk in range(PROMPT_BLOCKS)]
    ctx["ready"] = True


def _attention_stream(ctx, blk, prev_kv, limit, sink_ref):
    while "ready" not in ctx:
        yield
    rows = slice(BLK * blk, BLK * (blk + 1))
    k_prev, v_prev = prev_kv()
    k_all = jnp.concatenate([k_prev, ctx["proj"][rows, OFF_K:OFF_V]], axis=0)
    v_all = jnp.concatenate([v_prev, ctx["proj"][rows, OFF_V:OFF_HQ]], axis=0)
    out = {}
    yield from _attend_steps([z[rows] for z in ctx["qz"]], k_all, v_all, sink_ref, limit, out)
    ctx["oa"][blk] = out["oa"]


def _hgrn_stream(ctx, blk, h, state, turn, xor):
    while "ready" not in ctx:
        yield
    rows = slice(BLK * blk, BLK * (blk + 1))
    sl = slice(HGRN_DK * h, HGRN_DK * (h + 1))
    qh, kh, bh = ctx["q"][rows, sl], ctx["kin"][rows, sl], ctx["b"][blk][:, sl]
    vh = ctx["vin"][rows, sl].astype(BF16)
    scores = {}
    yield from _intra_steps(qh, kh, bh, BLK // 2, xor, scores)
    o_intra = _dot(scores["a"].astype(BF16), vh)
    yield
    b_last = bh[BLK - 1:BLK, :]
    qe = (qh * jnp.exp(bh)).astype(BF16)
    kd = (kh * jnp.exp(b_last - bh)).astype(BF16)
    yield
    while state["turn"][h] != turn:
        yield
    st = state["st"][h]
    o_inter = _dot_nt(qe, st.astype(BF16))
    update = _dot_tn(vh, kd)
    yield
    state["st"][h] = st * jnp.exp(b_last) + update
    state["turn"][h] = turn + 1
    ctx["oh"][blk][h] = o_intra + o_inter


def _mix_out_stream(ctx, x, y_store, w_out_ref, ang_ref, hng_ref, g2_ref, b2_ref):
    while any(o is None for o in ctx["oa"]) or any(o is None for blk in ctx["oh"] for o in blk):
        yield
    oa = jnp.concatenate(ctx["oa"], axis=0)
    oh = [jnp.concatenate([ctx["oh"][blk][h] for blk in range(PROMPT_BLOCKS)], axis=0)
          for h in range(HGRN_HEADS)]
    y_store(_mix_out(x, oa, oh, ctx["proj"], w_out_ref, ang_ref, hng_ref, g2_ref, b2_ref))


def _mixer_prompt_kernel(steps_per_seq, sink_ref, x_ref, x_next_ref, w_in_ref, w_out_ref, lbp_ref,
                         ang_ref, hng_ref, g2_ref, b2_ref, y_ref, s_out_ref, k_out_ref, v_out_ref,
                         st_ref, kv_ref, proj_ref):
    i = pl.program_id(0)
    j = lax.rem(i, steps_per_seq)

    @pl.when(j == 0)
    def _():
        st_ref[...] = jnp.zeros_like(st_ref)
        kv_ref[...] = jnp.zeros_like(kv_ref)

    @pl.when(i == 0)
    def _():
        proj_ref[...] = _dot(x_ref[0].astype(BF16), w_in_ref[...])

    lb = _lower_bound(lbp_ref[...])
    tri = _tri_matrix(BLK)
    xor = (lax.broadcasted_iota(jnp.int32, (BLK, BLK), 0)
           ^ lax.broadcasted_iota(jnp.int32, (BLK, BLK), 1))
    trow = lax.broadcasted_iota(jnp.int32, (BLK, 2 * WINDOW), 0)
    full = jnp.full(trow.shape, WINDOW, jnp.int32)

    state = {"st": [st_ref[h] for h in range(HGRN_HEADS)], "turn": [0] * HGRN_HEADS}
    ctxs = [{"oa": [None] * PROMPT_BLOCKS,
             "oh": [[None] * HGRN_HEADS for _ in range(PROMPT_BLOCKS)]} for _ in range(PROMPT_TILES)]
    ctxs[0]["proj"] = proj_ref[...]
    next_ctx = {}

    def kv_block(tile, blk):
        rows = slice(BLK * blk, BLK * (blk + 1))
        proj = ctxs[tile]["proj"]
        return proj[rows, OFF_K:OFF_V], proj[rows, OFF_V:OFF_HQ]

    streams = []
    for tile in range(PROMPT_TILES):
        ctx = ctxs[tile]
        x = x_ref[tile]
        if tile > 0:
            streams.append(_proj_stream(ctx, x, w_in_ref, lambda prev=ctxs[tile - 1]: "ready" in prev))
        streams.append(_prepare_stream(ctx, lb, tri))
        for blk in range(PROMPT_BLOCKS):
            if tile == 0 and blk == 0:
                prev_kv = lambda: (kv_ref[0], kv_ref[1])
                limit = jnp.where(j > 0, WINDOW, trow + 1)
            else:
                g = tile * PROMPT_BLOCKS + blk - 1
                prev_kv = lambda g=g: kv_block(g // PROMPT_BLOCKS, g % PROMPT_BLOCKS)
                limit = full
            streams.append(_attention_stream(ctx, blk, prev_kv, limit, sink_ref))
            for h in range(HGRN_HEADS):
                streams.append(_hgrn_stream(ctx, blk, h, state, tile * PROMPT_BLOCKS + blk, xor))

        def y_store(y, tile=tile):
            y_ref[tile] = y

        streams.append(_mix_out_stream(ctx, x, y_store, w_out_ref, ang_ref, hng_ref, g2_ref, b2_ref))
    streams.append(_proj_stream(next_ctx, x_next_ref[0], w_in_ref, lambda: "ready" in ctxs[-1]))
    _round_robin(streams)

    proj_ref[...] = next_ctx["proj"]
    k_last, v_last = kv_block(PROMPT_TILES - 1, PROMPT_BLOCKS - 1)
    kv_ref[0] = k_last
    kv_ref[1] = v_last
    for h in range(HGRN_HEADS):
        st_ref[h] = state["st"][h]

    @pl.when(j == steps_per_seq - 1)
    def _():
        k_out_ref[0] = k_last
        v_out_ref[0] = v_last
        for h in range(HGRN_HEADS):
            s_out_ref[0, h] = state["st"][h].T


def _mixer_prompt(x, sink, w_in, w_out, lbp, ang, hng, g2, b2):
    nb, seq, _ = x.shape
    tile_rows = PROMPT_BLOCKS * BLK
    steps_per_seq = seq // (PROMPT_TILES * tile_rows)
    n_tiles = nb * seq // tile_rows
    xt = x.reshape(n_tiles, tile_rows, D_MODEL)
    y, s, k, v = pl.pallas_call(
        functools.partial(_mixer_prompt_kernel, steps_per_seq),
        grid=(nb * steps_per_seq,),
        in_specs=[
            pl.BlockSpec(memory_space=pltpu.SMEM),
            pl.BlockSpec((PROMPT_TILES, tile_rows, D_MODEL), lambda i: (i, 0, 0)),
            pl.BlockSpec((1, tile_rows, D_MODEL),
                         lambda i: (jnp.minimum(PROMPT_TILES * (i + 1), n_tiles - 1), 0, 0)),
            _const_spec((D_MODEL, D_IN)),
            _const_spec((D_MIX, D_MODEL)),
            _const_spec((DEPTH + 1, HGRN_KW)),
            _const_spec((1, ATTN_WIDTH)),
            _const_spec((1, HGRN_VW)),
            _const_spec((1, D_MODEL)),
            _const_spec((1, D_MODEL)),
        ],
        out_specs=[
            pl.BlockSpec((PROMPT_TILES, tile_rows, D_MODEL), lambda i: (i, 0, 0)),
            pl.BlockSpec((1, HGRN_HEADS, HGRN_DK, HGRN_DV), lambda i: (i // steps_per_seq, 0, 0, 0)),
            pl.BlockSpec((1, WINDOW, KV_WIDTH), lambda i: (i // steps_per_seq, 0, 0)),
            pl.BlockSpec((1, WINDOW, KV_WIDTH), lambda i: (i // steps_per_seq, 0, 0)),
        ],
        out_shape=[
            jax.ShapeDtypeStruct((n_tiles, tile_rows, D_MODEL), F32),
            jax.ShapeDtypeStruct((nb, HGRN_HEADS, HGRN_DK, HGRN_DV), F32),
            jax.ShapeDtypeStruct((nb, WINDOW, KV_WIDTH), F32),
            jax.ShapeDtypeStruct((nb, WINDOW, KV_WIDTH), F32),
        ],
        scratch_shapes=[
            pltpu.VMEM((HGRN_HEADS, HGRN_DV, HGRN_DK), F32),
            pltpu.VMEM((2, WINDOW, KV_WIDTH), F32),
            pltpu.VMEM((tile_rows, D_IN), F32),
        ],
        compiler_params=pltpu.CompilerParams(
            dimension_semantics=("arbitrary",), vmem_limit_bytes=VMEM_LIMIT),
        name="mixer_prompt",
    )(sink, xt, xt, w_in, w_out, lbp, ang, hng, g2, b2)
    return y.reshape(nb, seq, D_MODEL), s, k, v


def _mixer_sample_kernel(sink_ref, x_ref, s0_ref, ck_ref, cv_ref, w_in_ref, w_out_ref, lbp_ref,
                         ang_ref, hng_ref, g2_ref, b2_ref, y_ref, s_out_ref, k_out_ref, v_out_ref):
    x = x_ref[...]
    proj = _dot(x.astype(BF16), w_in_ref[...])
    k_new = proj[:, OFF_K:OFF_V]
    v_new = proj[:, OFF_V:OFF_HQ]

    oa = _attend_sample(_head_queries(proj[:, :ATTN_WIDTH]), k_new, v_new, ck_ref, cv_ref, sink_ref)
    for i in range(SEQ_PER_STEP):
        rows = slice(TOK * i, TOK * (i + 1))
        k_out_ref[i] = jnp.concatenate([ck_ref[i, TOK:, :], k_new[rows]], axis=0)
        v_out_ref[i] = jnp.concatenate([cv_ref[i, TOK:, :], v_new[rows]], axis=0)

    q, logf, kin, vin = _hgrn_gates(proj, _lower_bound(lbp_ref[...]))
    b = _block_cumsum(_tri_matrix(TOK), logf)
    xor = (lax.broadcasted_iota(jnp.int32, (BLK, BLK), 0)
           ^ lax.broadcasted_iota(jnp.int32, (BLK, BLK), 1))
    lane = lax.broadcasted_iota(jnp.int32, (HGRN_DK, BLK), 1)
    oh = []
    for h in range(HGRN_HEADS):
        sl = slice(HGRN_DK * h, HGRN_DK * (h + 1))
        qh, kh, bh = q[:, sl], kin[:, sl], b[:, sl]
        vh = vin[:, sl].astype(BF16)
        a = _intra_scores(qh, kh, bh, TOK // 2, xor)
        o_intra = _dot(a.astype(BF16), vh)
        qe = (qh * jnp.exp(bh)).astype(BF16)
        b3 = bh.reshape(SEQ_PER_STEP, TOK, HGRN_DK)
        b_last = jnp.broadcast_to(b3[:, TOK - 1:TOK, :], b3.shape).reshape(BLK, HGRN_DK)
        kd_t = (kh * jnp.exp(b_last - bh)).T
        dec_t = jnp.exp(b_last).T
        o_rows = []
        for i in range(SEQ_PER_STEP):
            s0 = s0_ref[i, h]
            o_rows.append(_dot(qe[TOK * i:TOK * (i + 1)], s0.astype(BF16)))
            kd_i = jnp.where((lane >= TOK * i) & (lane < TOK * (i + 1)), kd_t, 0.0).astype(BF16)
            s_out_ref[i, h] = s0 * dec_t[:, TOK * i:TOK * i + 1] + _dot(kd_i, vh)
        oh.append(o_intra + jnp.concatenate(o_rows, axis=0))

    y_ref[...] = _mix_out(x, oa, oh, proj, w_out_ref, ang_ref, hng_ref, g2_ref, b2_ref)


def _mixer_sample(x, s0, ck, cv, sink, w_in, w_out, lbp, ang, hng, g2, b2):
    n = x.shape[0]
    nseq = s0.shape[0]
    state_spec = pl.BlockSpec((SEQ_PER_STEP, HGRN_HEADS, HGRN_DK, HGRN_DV), lambda i: (i, 0, 0, 0))
    cache_spec = pl.BlockSpec((SEQ_PER_STEP, WINDOW, KV_WIDTH), lambda i: (i, 0, 0))
    return pl.pallas_call(
        _mixer_sample_kernel,
        grid=(n // BLK,),
        in_specs=[
            pl.BlockSpec(memory_space=pltpu.SMEM),
            pl.BlockSpec((BLK, D_MODEL), lambda i: (i, 0)),
            state_spec, cache_spec, cache_spec,
            _const_spec((D_MODEL, D_IN)),
            _const_spec((D_MIX, D_MODEL)),
            _const_spec((DEPTH + 1, HGRN_KW)),
            _const_spec((1, ATTN_WIDTH)),
            _const_spec((1, HGRN_VW)),
            _const_spec((1, D_MODEL)),
            _const_spec((1, D_MODEL)),
        ],
        out_specs=[
            pl.BlockSpec((BLK, D_MODEL), lambda i: (i, 0)),
            state_spec, cache_spec, cache_spec,
        ],
        out_shape=[
            jax.ShapeDtypeStruct((n, D_MODEL), F32),
            jax.ShapeDtypeStruct((nseq, HGRN_HEADS, HGRN_DK, HGRN_DV), F32),
            jax.ShapeDtypeStruct((nseq, WINDOW, KV_WIDTH), F32),
            jax.ShapeDtypeStruct((nseq, WINDOW, KV_WIDTH), F32),
        ],
        compiler_params=pltpu.CompilerParams(
            dimension_semantics=("arbitrary",), vmem_limit_bytes=VMEM_LIMIT),
        name="mixer_sample",
    )(sink, x, s0, ck, cv, w_in, w_out, lbp, ang, hng, g2, b2)


def kernel(x_prompt, x_sample, state_hgrn, cache_win_k, cache_win_v, ln1_g, ln1_b, ffn1_w13,
           ffn1_w2, w_in, lb_param, attn_sink, attn_norm_g, hgrn_norm_g, w_out, ln2_g, ln2_b,
           ffn2_w13, ffn2_w2, ln3_g, ln3_b):
    nb, seq, d = x_prompt.shape
    ns, tn, _ = x_sample.shape
    assert state_hgrn.shape[0] == DEPTH == 1 and d == D_MODEL
    assert seq % (PROMPT_TILES * PROMPT_BLOCKS * BLK) == 0 and tn == TOK and (ns * tn) % BLK == 0
    assert cache_win_k.shape[2] == WINDOW

    w13_1 = ffn1_w13[0].astype(BF16)
    w2_1 = ffn1_w2[0].astype(BF16)
    w13_2 = ffn2_w13[0].astype(BF16)
    w2_2 = ffn2_w2[0].astype(BF16)
    w_in_b = w_in[0].astype(BF16)
    w_out_b = w_out[0].astype(BF16)
    sink = attn_sink[0]
    lbp = lb_param.astype(F32)
    mix_params = (sink, w_in_b, w_out_b, lbp, attn_norm_g, hgrn_norm_g, ln2_g, ln2_b)

    xp = x_prompt.reshape(nb * seq, d)
    xs = x_sample.reshape(ns * tn, d)

    xp = _ffn_ln(xp, w13_1, w2_1, ln1_g, ln1_b, tm=256)
    xs = _ffn_ln(xs, w13_1, w2_1, ln1_g, ln1_b, tm=256)

    xp, sp, kp, vp = _mixer_prompt(xp.reshape(nb, seq, d), *mix_params)
    xs, ss, ks, vs = _mixer_sample(
        xs, state_hgrn[0], cache_win_k[0].reshape(ns, WINDOW, KV_WIDTH),
        cache_win_v[0].reshape(ns, WINDOW, KV_WIDTH), *mix_params)

    xp = _ffn_ln(xp.reshape(nb * seq, d), w13_2, w2_2, ln3_g, ln3_b, tm=256)
    xs = _ffn_ln(xs, w13_2, w2_2, ln3_g, ln3_b, tm=256)

    cache_tail = (WINDOW, ATTN_KV_HEADS, HEAD_DIM)
    return (xp.reshape(nb, seq, d), xs.reshape(ns, tn, d),
            sp[None], kp.reshape((DEPTH, nb) + cache_tail), vp.reshape((DEPTH, nb) + cache_tail),
            ss[None], ks.reshape((DEPTH, ns) + cache_tail), vs.reshape((DEPTH, ns) + cache_tail))
```

```python
import jax
import jax.numpy as jnp
from jax import lax
from jax.experimental import pallas as pl
from jax.experimental.pallas import tpu as pltpu

F32 = jnp.float32
BF16 = jnp.bfloat16

D_MODEL = 1024
DEPTH = 1
ATTN_HEADS = 8
ATTN_KV_HEADS = 2
GROUP = ATTN_HEADS // ATTN_KV_HEADS
HEAD_DIM = 64
ATTN_WIDTH = ATTN_HEADS * HEAD_DIM
KV_WIDTH = ATTN_KV_HEADS * HEAD_DIM
WINDOW = 128
ATTN_SCALE = HEAD_DIM ** -0.5
HGRN_HEADS = 4
HGRN_DK = 128
HGRN_DV = 128
HGRN_KW = HGRN_HEADS * HGRN_DK
HGRN_VW = HGRN_HEADS * HGRN_DV
D_MIX = ATTN_WIDTH + HGRN_VW
D_IN = ATTN_WIDTH + 2 * KV_WIDTH + 2 * HGRN_KW + 2 * HGRN_VW
D_FF = 2816
ALPHA = (2.0 * DEPTH) ** 0.25
LN_EPS = 1e-5
RMS_EPS = 1e-6

OFF_K = ATTN_WIDTH
OFF_V = OFF_K + KV_WIDTH
OFF_HQ = OFF_V + KV_WIDTH
OFF_HF = OFF_HQ + HGRN_KW
OFF_HI = OFF_HF + HGRN_KW
OFF_HG = OFF_HI + HGRN_VW

LANES = 128
SUBLANES = 8
BLK = 128
PROMPT_BLOCKS = 2
PROMPT_TILES = 4
PROJ_CHUNK = 256
FFN_TILE = 256
FFN_TILES = 4
FFN_CHUNK = 256
TOK = SUBLANES
SEQ_PER_STEP = BLK // TOK
V7X_VMEM_BYTES = 64 * 1024 * 1024
F32_BYTES = 4
BF16_BYTES = 2

NT_DIMS = (((1,), (1,)), ((), ()))
TN_DIMS = (((0,), (0,)), ((), ()))


def _vmem_limit(resident_bytes, streamed_bytes, value_bytes):
    need = resident_bytes + 2 * streamed_bytes + value_bytes
    assert need <= V7X_VMEM_BYTES, need
    return need


_DONE = object()


def _round_robin(streams):
    live = list(streams)
    rounds = 0
    while live:
        live = [g for g in live if next(g, _DONE) is not _DONE]
        rounds += 1
        assert rounds < 100000, "a stream waits for something no other stream produces"


def _dot(a, b):
    return jnp.dot(a, b, preferred_element_type=F32)


def _dot_nt(a, b):
    return lax.dot_general(a, b, NT_DIMS, preferred_element_type=F32)


def _dot_tn(a, b):
    return lax.dot_general(a, b, TN_DIMS, preferred_element_type=F32)


def _layer_norm(r, g, b):
    mu = jnp.mean(r, axis=-1, keepdims=True)
    c = r - mu
    var = jnp.mean(c * c, axis=-1, keepdims=True)
    return c * lax.rsqrt(var + LN_EPS) * g + b


def _rms_norm(x, g):
    return x * lax.rsqrt(jnp.mean(x * x, axis=-1, keepdims=True) + RMS_EPS) * g


def _ffn_steps(x, w13_ref, w2_ref, g, b, store, on_hidden=lambda: None):
    xb = x.astype(BF16)
    acts = []
    for c in range(D_FF // FFN_CHUNK):
        gate = _dot(xb, w13_ref[:, FFN_CHUNK * c:FFN_CHUNK * (c + 1)])
        up = _dot(xb, w13_ref[:, D_FF + FFN_CHUNK * c:D_FF + FFN_CHUNK * (c + 1)])
        acts.append((jax.nn.silu(gate) * up).astype(BF16))
        yield
    act = jnp.concatenate(acts, axis=1)
    on_hidden()
    ys = []
    for c in range(D_MODEL // FFN_CHUNK):
        ys.append(_dot(act, w2_ref[:, FFN_CHUNK * c:FFN_CHUNK * (c + 1)]))
        yield
    store(_layer_norm(ALPHA * x + 0.5 * jnp.concatenate(ys, axis=1), g, b))


def _ffn_stream(x_ref, o_ref, tile, w13_ref, w2_ref, g, b, flags, may_start):
    while not may_start():
        yield
    rows = slice(FFN_TILE * tile, FFN_TILE * (tile + 1))

    def store(y):
        o_ref[rows, :] = y

    def on_hidden():
        flags[tile] = True

    yield from _ffn_steps(x_ref[rows, :], w13_ref, w2_ref, g, b, store, on_hidden)


def _ffn_ln_kernel(x_ref, w13_ref, w2_ref, g_ref, b_ref, o_ref):
    n_tiles = x_ref.shape[0] // FFN_TILE
    g, b = g_ref[...], b_ref[...]
    flags = [False] * n_tiles
    _round_robin(
        _ffn_stream(x_ref, o_ref, t, w13_ref, w2_ref, g, b, flags,
                    (lambda: True) if t == 0 else (lambda t=t: flags[t - 1]))
        for t in range(n_tiles))


def _const_spec(shape):
    return pl.BlockSpec(shape, lambda *_: (0,) * len(shape), pipeline_mode=pl.Buffered(1))


def _ffn_ln(x, w13, w2, g, b):
    n = x.shape[0]
    tm = FFN_TILE * FFN_TILES
    assert n % tm == 0
    weights = (D_MODEL * 2 * D_FF + D_FF * D_MODEL) * BF16_BYTES
    blocks = 2 * tm * D_MODEL * F32_BYTES
    values = 2 * FFN_TILE * ((D_MODEL + D_FF) * BF16_BYTES
                             + (2 * FFN_CHUNK + 2 * D_MODEL) * F32_BYTES)
    vmem = _vmem_limit(weights, blocks, values)
    return pl.pallas_call(
        _ffn_ln_kernel,
        grid=(n // tm,),
        in_specs=[
            pl.BlockSpec((tm, D_MODEL), lambda i: (i, 0)),
            _const_spec((D_MODEL, 2 * D_FF)),
            _const_spec((D_FF, D_MODEL)),
            _const_spec((1, D_MODEL)),
            _const_spec((1, D_MODEL)),
        ],
        out_specs=pl.BlockSpec((tm, D_MODEL), lambda i: (i, 0)),
        out_shape=jax.ShapeDtypeStruct((n, D_MODEL), F32),
        compiler_params=pltpu.CompilerParams(
            dimension_semantics=("arbitrary",), vmem_limit_bytes=vmem),
        name="ffn_ln",
    )(x, w13, w2, g, b)


def _lower_bound(lbp):
    mx = jnp.max(lbp, axis=0, keepdims=True)
    e = jnp.exp(lbp - mx)
    return e[0:1] / jnp.sum(e, axis=0, keepdims=True)


def _split3(x):
    hi = x.astype(BF16)
    r = x - hi.astype(F32)
    mid = r.astype(BF16)
    lo = (r - mid.astype(F32)).astype(BF16)
    return hi, mid, lo


def _block_cumsum(tri, g):
    w = g.shape[1]
    parts = jnp.concatenate(_split3(g), axis=1)
    s = _dot(tri, parts)
    return s[:, :w] + s[:, w:2 * w] + s[:, 2 * w:]


def _tri_matrix(group):
    r = lax.broadcasted_iota(jnp.int32, (BLK, BLK), 0)
    c = lax.broadcasted_iota(jnp.int32, (BLK, BLK), 1)
    keep = (c <= r) & ((r ^ c) < group)
    return jnp.where(keep, 1.0, 0.0).astype(BF16)


def _pair_ref_rows(b, m):
    t, w = b.shape
    if m >= SUBLANES:
        b3 = b.reshape(t // (2 * m), 2 * m, w)
        return jnp.broadcast_to(b3[:, m - 1:m, :], b3.shape).reshape(t, w)
    b3 = b.reshape(t // SUBLANES, SUBLANES, w)
    sub = lax.broadcasted_iota(jnp.int32, b3.shape, 1)

    def row(i):
        return jnp.broadcast_to(b3[:, i:i + 1, :], b3.shape)

    if m == 4:
        out = row(3)
    elif m == 2:
        out = jnp.where(sub < 4, row(1), row(5))
    else:
        out = jnp.where(sub < 4, jnp.where(sub < 2, row(0), row(2)),
                        jnp.where(sub < 6, row(4), row(6)))
    return out.reshape(t, w)


def _level_operands(q, k, e, m):
    t, w = q.shape
    if m >= SUBLANES:
        zero = jnp.zeros((m, w), F32)
        qs, ks = [], []
        for blk in range(t // m):
            rows = slice(blk * m, (blk + 1) * m)
            if blk % 2:
                qs.append(q[rows] * e[rows])
                ks.append(zero)
            else:
                qs.append(zero)
                ks.append(k[rows] * e[rows])
        return jnp.concatenate(qs, axis=0).astype(BF16), jnp.concatenate(ks, axis=0).astype(BF16)
    upper = (lax.broadcasted_iota(jnp.int32, q.shape, 0) & m) != 0
    return (jnp.where(upper, q * e, 0.0).astype(BF16), jnp.where(upper, 0.0, k * e).astype(BF16))


def _intra_steps(q, k, b, max_m, xor, out):
    a = jnp.where(xor == 0, jnp.sum(q * k, axis=-1, keepdims=True), 0.0)
    yield
    m = 1
    while m <= max_m:
        e = jnp.exp(-jnp.abs(b - _pair_ref_rows(b, m)))
        qs, ks = _level_operands(q, k, e, m)
        yield
        p = _dot_nt(qs, ks)
        yield
        a = jnp.where(xor >= m, p, a)
        m *= 2
    if 2 * max_m < q.shape[0]:
        a = jnp.where(xor < 2 * max_m, a, 0.0)
    out["a"] = a


def _intra_scores(q, k, b, max_m, xor):
    out = {}
    for _ in _intra_steps(q, k, b, max_m, xor, out):
        pass
    return out["a"]


def _hgrn_gates(proj, lb):
    q = jax.nn.silu(proj[:, OFF_HQ:OFF_HF])
    f = lb + (1.0 - lb) * jax.nn.sigmoid(proj[:, OFF_HF:OFF_HI])
    return q, jnp.log(f), 1.0 - f, proj[:, OFF_HI:OFF_HG]


def _head_queries(qa):
    low = lax.broadcasted_iota(jnp.int32, (qa.shape[0], LANES), 1) < HEAD_DIM
    out = []
    for pair in range(ATTN_HEADS // 2):
        piece = qa[:, LANES * pair:LANES * (pair + 1)] * ATTN_SCALE
        rolled = pltpu.roll(piece, HEAD_DIM, axis=1)
        for half in range(2):
            kv = (2 * pair + half) // GROUP
            src = piece if half == kv else rolled
            z = jnp.where(low, src, 0.0) if kv == 0 else jnp.where(low, 0.0, src)
            out.append(z.astype(BF16))
    return out


def _merge_heads(o):
    low = lax.broadcasted_iota(jnp.int32, o[0].shape, 1) < HEAD_DIM
    cols = []
    for pair in range(ATTN_HEADS // 2):
        kv = (2 * pair) // GROUP
        even, odd = o[2 * pair], o[2 * pair + 1]
        if kv == 0:
            odd = pltpu.roll(odd, HEAD_DIM, axis=1)
        else:
            even = pltpu.roll(even, HEAD_DIM, axis=1)
        cols.append(jnp.where(low, even, odd))
    return jnp.concatenate(cols, axis=1)


def _slope(head):
    return 2.0 ** (-8.0 * (head + 1) / ATTN_HEADS)


def _attend_steps(qz, k_all, v_all, sink_ref, limit, out):
    tq = qz[0].shape[0]
    nk = k_all.shape[0]
    s_all = _dot_nt(jnp.concatenate(qz, axis=0), k_all.astype(BF16))
    trow = lax.broadcasted_iota(jnp.int32, (tq, nk), 0)
    ccol = lax.broadcasted_iota(jnp.int32, (tq, nk), 1)
    dist = trow + WINDOW - ccol
    visible = dist.astype(jnp.uint32) < limit.astype(jnp.uint32)
    distf = dist.astype(F32)
    yield
    ps, dens = [], []
    for head in range(ATTN_HEADS):
        sink = sink_ref[head]
        s = s_all[tq * head:tq * (head + 1)]
        s = jnp.where(visible, s - _slope(head) * distf, -jnp.inf)
        mx = jnp.maximum(jnp.max(s, axis=-1, keepdims=True), sink)
        yield
        p = jnp.exp(s - mx)
        dens.append(jnp.sum(p, axis=-1, keepdims=True) + jnp.exp(sink - mx))
        ps.append(p.astype(BF16))
        yield
    o_all = _dot(jnp.concatenate(ps, axis=0), v_all.astype(BF16))
    yield
    out["oa"] = _merge_heads([o_all[tq * h:tq * (h + 1)] / dens[h] for h in range(ATTN_HEADS)])


def _attend_sample(qz, k_new, v_new, ck_ref, cv_ref, sink_ref):
    rows = qz[0].shape[0]
    k_new = k_new.astype(BF16)
    v_new = v_new.astype(BF16)
    sn_all = _dot_nt(jnp.concatenate(qz, axis=0), k_new)
    sc_seq = []
    for i in range(SEQ_PER_STEP):
        lhs = jnp.concatenate([z[TOK * i:TOK * (i + 1)] for z in qz], axis=0)
        sc_seq.append(_dot_nt(lhs, ck_ref[i].astype(BF16)))

    r = lax.broadcasted_iota(jnp.int32, (rows, rows), 0)
    c = lax.broadcasted_iota(jnp.int32, (rows, rows), 1)
    new_visible = ((r ^ c) < TOK) & (c <= r)
    new_dist = (r - c).astype(F32)
    tok = lax.broadcasted_iota(jnp.int32, (rows, WINDOW), 0) & (TOK - 1)
    key = lax.broadcasted_iota(jnp.int32, (rows, WINDOW), 1)
    old_visible = key > tok
    old_dist = (WINDOW + tok - key).astype(F32)

    pn, pc, dens = [], [], []
    for head in range(ATTN_HEADS):
        sink = sink_ref[head]
        slope = _slope(head)
        sn = sn_all[rows * head:rows * (head + 1)]
        sc = jnp.concatenate([s[TOK * head:TOK * (head + 1)] for s in sc_seq], axis=0)
        sn = jnp.where(new_visible, sn - slope * new_dist, -jnp.inf)
        sc = jnp.where(old_visible, sc - slope * old_dist, -jnp.inf)
        mx = jnp.maximum(jnp.maximum(jnp.max(sn, axis=-1, keepdims=True),
                                     jnp.max(sc, axis=-1, keepdims=True)), sink)
        en = jnp.exp(sn - mx)
        ec = jnp.exp(sc - mx)
        dens.append(jnp.sum(en, axis=-1, keepdims=True) + jnp.sum(ec, axis=-1, keepdims=True)
                    + jnp.exp(sink - mx))
        pn.append(en.astype(BF16))
        pc.append(ec.astype(BF16))

    on_all = _dot(jnp.concatenate(pn, axis=0), v_new)
    oc_seq = []
    for i in range(SEQ_PER_STEP):
        lhs = jnp.concatenate([p[TOK * i:TOK * (i + 1)] for p in pc], axis=0)
        oc_seq.append(_dot(lhs, cv_ref[i].astype(BF16)))
    o = []
    for head in range(ATTN_HEADS):
        oc = jnp.concatenate([s[TOK * head:TOK * (head + 1)] for s in oc_seq], axis=0)
        o.append((on_all[rows * head:rows * (head + 1)] + oc) / dens[head])
    return _merge_heads(o)


def _mix_out(x, oa, oh, proj, w_out_ref, ang_ref, hng_ref, g2_ref, b2_ref):
    oa = _rms_norm(oa, ang_ref[...])
    hng = hng_ref[...]
    gate = jax.nn.silu(proj[:, OFF_HG:])
    ohn = [_rms_norm(oh[h], hng[:, HGRN_DV * h:HGRN_DV * (h + 1)]) for h in range(HGRN_HEADS)]
    mix = jnp.concatenate([oa, jnp.concatenate(ohn, axis=1) * gate], axis=1).astype(BF16)
    y = _dot(mix, w_out_ref[...])
    return _layer_norm(ALPHA * x + y, g2_ref[...], b2_ref[...])


def _proj_stream(ctx, x, w_in_ref, may_start):
    while not may_start():
        yield
    xb = x.astype(BF16)
    chunks = []
    for c in range(D_IN // PROJ_CHUNK):
        chunks.append(_dot(xb, w_in_ref[:, PROJ_CHUNK * c:PROJ_CHUNK * (c + 1)]))
        yield
    ctx["proj"] = jnp.concatenate(chunks, axis=1)


def _prepare_stream(ctx, lb, tri):
    while "proj" not in ctx:
        yield
    proj = ctx["proj"]
    ctx["qz"] = _head_queries(proj[:, :ATTN_WIDTH])
    yield
    ctx["q"], logf, ctx["kin"], ctx["vin"] = _hgrn_gates(proj, lb)
    yield
    ctx["b"] = [_block_cumsum(tri, logf[BLK * blk:BLK * (blk + 1)]) for blk in range(PROMPT_BLOCKS)]
    ctx["ready"] = True


def _attention_stream(ctx, blk, prev_kv, limit, sink_ref):
    while "ready" not in ctx:
        yield
    rows = slice(BLK * blk, BLK * (blk + 1))
    k_prev, v_prev = prev_kv()
    k_all = jnp.concatenate([k_prev, ctx["proj"][rows, OFF_K:OFF_V]], axis=0)
    v_all = jnp.concatenate([v_prev, ctx["proj"][rows, OFF_V:OFF_HQ]], axis=0)
    out = {}
    yield from _attend_steps([z[rows] for z in ctx["qz"]], k_all, v_all, sink_ref, limit, out)
    ctx["oa"][blk] = out["oa"]


def _hgrn_stream(ctx, blk, h, state, turn, xor):
    while "ready" not in ctx:
        yield
    rows = slice(BLK * blk, BLK * (blk + 1))
    sl = slice(HGRN_DK * h, HGRN_DK * (h + 1))
    qh, kh, bh = ctx["q"][rows, sl], ctx["kin"][rows, sl], ctx["b"][blk][:, sl]
    vh = ctx["vin"][rows, sl].astype(BF16)
    scores = {}
    yield from _intra_steps(qh, kh, bh, BLK // 2, xor, scores)
    o_intra = _dot(scores["a"].astype(BF16), vh)
    yield
    b_last = bh[BLK - 1:BLK, :]
    qe = (qh * jnp.exp(bh)).astype(BF16)
    kd = (kh * jnp.exp(b_last - bh)).astype(BF16)
    yield
    while state["turn"][h] != turn:
        yield
    st = state["st"][h]
    o_inter = _dot_nt(qe, st.astype(BF16))
    update = _dot_tn(vh, kd)
    yield
    state["st"][h] = st * jnp.exp(b_last) + update
    state["turn"][h] = turn + 1
    ctx["oh"][blk][h] = o_intra + o_inter


def _mix_out_stream(ctx, x, y_store, w_out_ref, ang_ref, hng_ref, g2_ref, b2_ref):
    while any(o is None for o in ctx["oa"]) or any(o is None for blk in ctx["oh"] for o in blk):
        yield
    oa = jnp.concatenate(ctx["oa"], axis=0)
    oh = [jnp.concatenate([ctx["oh"][blk][h] for blk in range(PROMPT_BLOCKS)], axis=0)
          for h in range(HGRN_HEADS)]
    y_store(_mix_out(x, oa, oh, ctx["proj"], w_out_ref, ang_ref, hng_ref, g2_ref, b2_ref))


def _mixer_prompt_kernel(sink_ref, x_ref, w_in_ref, w_out_ref, lbp_ref, ang_ref, hng_ref, g2_ref,
                         b2_ref, y_ref, s_out_ref, k_out_ref, v_out_ref, st_ref, kv_ref):
    j = pl.program_id(1)

    @pl.when(j == 0)
    def _():
        st_ref[...] = jnp.zeros_like(st_ref)
        kv_ref[...] = jnp.zeros_like(kv_ref)

    lb = _lower_bound(lbp_ref[...])
    tri = _tri_matrix(BLK)
    xor = (lax.broadcasted_iota(jnp.int32, (BLK, BLK), 0)
           ^ lax.broadcasted_iota(jnp.int32, (BLK, BLK), 1))
    trow = lax.broadcasted_iota(jnp.int32, (BLK, 2 * WINDOW), 0)
    full = jnp.full(trow.shape, WINDOW, jnp.int32)

    state = {"st": [st_ref[h] for h in range(HGRN_HEADS)], "turn": [0] * HGRN_HEADS}
    ctxs = [{"oa": [None] * PROMPT_BLOCKS,
             "oh": [[None] * HGRN_HEADS for _ in range(PROMPT_BLOCKS)]} for _ in range(PROMPT_TILES)]

    def kv_block(tile, blk):
        rows = slice(BLK * blk, BLK * (blk + 1))
        proj = ctxs[tile]["proj"]
        return proj[rows, OFF_K:OFF_V], proj[rows, OFF_V:OFF_HQ]

    streams = []
    for tile in range(PROMPT_TILES):
        ctx = ctxs[tile]
        x = x_ref[tile]
        may_start = (lambda: True) if tile == 0 else (lambda prev=ctxs[tile - 1]: "ready" in prev)
        streams.append(_proj_stream(ctx, x, w_in_ref, may_start))
        streams.append(_prepare_stream(ctx, lb, tri))
        for blk in range(PROMPT_BLOCKS):
            if tile == 0 and blk == 0:
                prev_kv = lambda: (kv_ref[0], kv_ref[1])
                limit = jnp.where(j > 0, WINDOW, trow + 1)
            else:
                g = tile * PROMPT_BLOCKS + blk - 1
                prev_kv = lambda g=g: kv_block(g // PROMPT_BLOCKS, g % PROMPT_BLOCKS)
                limit = full
            streams.append(_attention_stream(ctx, blk, prev_kv, limit, sink_ref))
            for h in range(HGRN_HEADS):
                streams.append(_hgrn_stream(ctx, blk, h, state, tile * PROMPT_BLOCKS + blk, xor))

        def y_store(y, tile=tile):
            y_ref[tile] = y

        streams.append(_mix_out_stream(ctx, x, y_store, w_out_ref, ang_ref, hng_ref, g2_ref, b2_ref))
    _round_robin(streams)

    k_last, v_last = kv_block(PROMPT_TILES - 1, PROMPT_BLOCKS - 1)
    kv_ref[0] = k_last
    kv_ref[1] = v_last
    for h in range(HGRN_HEADS):
        st_ref[h] = state["st"][h]

    @pl.when(j == pl.num_programs(1) - 1)
    def _():
        k_out_ref[0] = k_last
        v_out_ref[0] = v_last
        for h in range(HGRN_HEADS):
            s_out_ref[0, h] = state["st"][h].T


MIXER_WEIGHT_BYTES = (D_MODEL * D_IN + D_MIX * D_MODEL) * BF16_BYTES
STATE_BYTES = HGRN_HEADS * HGRN_DK * HGRN_DV * F32_BYTES
WINDOW_BYTES = WINDOW * KV_WIDTH * F32_BYTES


def _mixer_value_bytes(rows, keys):
    per_row = ((D_IN + 4 * HGRN_KW + 2 * D_MODEL) * F32_BYTES
               + (ATTN_HEADS * LANES + D_MIX) * BF16_BYTES)
    scores = ATTN_HEADS * BLK * keys * (F32_BYTES + BF16_BYTES)
    return rows * per_row + (rows // BLK) * scores


def _mixer_prompt(x, sink, w_in, w_out, lbp, ang, hng, g2, b2):
    nb, seq, _ = x.shape
    tile_rows = PROMPT_BLOCKS * BLK
    steps_per_seq = seq // (PROMPT_TILES * tile_rows)
    n_tiles = nb * seq // tile_rows
    xt = x.reshape(n_tiles, tile_rows, D_MODEL)
    blocks = (2 * PROMPT_TILES * tile_rows * D_MODEL * F32_BYTES
              + STATE_BYTES + 2 * WINDOW_BYTES)
    vmem = _vmem_limit(MIXER_WEIGHT_BYTES + STATE_BYTES + 2 * WINDOW_BYTES, blocks,
                       _mixer_value_bytes(2 * tile_rows, 2 * WINDOW))
    tile_spec = pl.BlockSpec((PROMPT_TILES, tile_rows, D_MODEL),
                             lambda b, j: (b * steps_per_seq + j, 0, 0))
    y, s, k, v = pl.pallas_call(
        _mixer_prompt_kernel,
        grid=(nb, steps_per_seq),
        in_specs=[
            pl.BlockSpec(memory_space=pltpu.SMEM),
            tile_spec,
            _const_spec((D_MODEL, D_IN)),
            _const_spec((D_MIX, D_MODEL)),
            _const_spec((DEPTH + 1, HGRN_KW)),
            _const_spec((1, ATTN_WIDTH)),
            _const_spec((1, HGRN_VW)),
            _const_spec((1, D_MODEL)),
            _const_spec((1, D_MODEL)),
        ],
        out_specs=[
            tile_spec,
            pl.BlockSpec((1, HGRN_HEADS, HGRN_DK, HGRN_DV), lambda b, j: (b, 0, 0, 0)),
            pl.BlockSpec((1, WINDOW, KV_WIDTH), lambda b, j: (b, 0, 0)),
            pl.BlockSpec((1, WINDOW, KV_WIDTH), lambda b, j: (b, 0, 0)),
        ],
        out_shape=[
            jax.ShapeDtypeStruct((n_tiles, tile_rows, D_MODEL), F32),
            jax.ShapeDtypeStruct((nb, HGRN_HEADS, HGRN_DK, HGRN_DV), F32),
            jax.ShapeDtypeStruct((nb, WINDOW, KV_WIDTH), F32),
            jax.ShapeDtypeStruct((nb, WINDOW, KV_WIDTH), F32),
        ],
        scratch_shapes=[
            pltpu.VMEM((HGRN_HEADS, HGRN_DV, HGRN_DK), F32),
            pltpu.VMEM((2, WINDOW, KV_WIDTH), F32),
        ],
        compiler_params=pltpu.CompilerParams(
            dimension_semantics=("arbitrary", "arbitrary"), vmem_limit_bytes=vmem),
        name="mixer_prompt",
    )(sink, xt, w_in, w_out, lbp, ang, hng, g2, b2)
    return y.reshape(nb, seq, D_MODEL), s, k, v


def _mixer_sample_kernel(sink_ref, x_ref, s0_ref, ck_ref, cv_ref, w_in_ref, w_out_ref, lbp_ref,
                         ang_ref, hng_ref, g2_ref, b2_ref, y_ref, s_out_ref, k_out_ref, v_out_ref):
    x = x_ref[...]
    proj = _dot(x.astype(BF16), w_in_ref[...])
    k_new = proj[:, OFF_K:OFF_V]
    v_new = proj[:, OFF_V:OFF_HQ]

    oa = _attend_sample(_head_queries(proj[:, :ATTN_WIDTH]), k_new, v_new, ck_ref, cv_ref, sink_ref)
    for i in range(SEQ_PER_STEP):
        rows = slice(TOK * i, TOK * (i + 1))
        k_out_ref[i] = jnp.concatenate([ck_ref[i, TOK:, :], k_new[rows]], axis=0)
        v_out_ref[i] = jnp.concatenate([cv_ref[i, TOK:, :], v_new[rows]], axis=0)

    q, logf, kin, vin = _hgrn_gates(proj, _lower_bound(lbp_ref[...]))
    b = _block_cumsum(_tri_matrix(TOK), logf)
    xor = (lax.broadcasted_iota(jnp.int32, (BLK, BLK), 0)
           ^ lax.broadcasted_iota(jnp.int32, (BLK, BLK), 1))
    lane = lax.broadcasted_iota(jnp.int32, (HGRN_DK, BLK), 1)
    oh = []
    for h in range(HGRN_HEADS):
        sl = slice(HGRN_DK * h, HGRN_DK * (h + 1))
        qh, kh, bh = q[:, sl], kin[:, sl], b[:, sl]
        vh = vin[:, sl].astype(BF16)
        a = _intra_scores(qh, kh, bh, TOK // 2, xor)
        o_intra = _dot(a.astype(BF16), vh)
        qe = (qh * jnp.exp(bh)).astype(BF16)
        b3 = bh.reshape(SEQ_PER_STEP, TOK, HGRN_DK)
        b_last = jnp.broadcast_to(b3[:, TOK - 1:TOK, :], b3.shape).reshape(BLK, HGRN_DK)
        kd_t = (kh * jnp.exp(b_last - bh)).T
        dec_t = jnp.exp(b_last).T
        o_rows = []
        for i in range(SEQ_PER_STEP):
            s0 = s0_ref[i, h]
            o_rows.append(_dot(qe[TOK * i:TOK * (i + 1)], s0.astype(BF16)))
            kd_i = jnp.where((lane >= TOK * i) & (lane < TOK * (i + 1)), kd_t, 0.0).astype(BF16)
            s_out_ref[i, h] = s0 * dec_t[:, TOK * i:TOK * i + 1] + _dot(kd_i, vh)
        oh.append(o_intra + jnp.concatenate(o_rows, axis=0))

    y_ref[...] = _mix_out(x, oa, oh, proj, w_out_ref, ang_ref, hng_ref, g2_ref, b2_ref)


def _mixer_sample(x, s0, ck, cv, sink, w_in, w_out, lbp, ang, hng, g2, b2):
    n = x.shape[0]
    nseq = s0.shape[0]
    state_spec = pl.BlockSpec((SEQ_PER_STEP, HGRN_HEADS, HGRN_DK, HGRN_DV), lambda i: (i, 0, 0, 0))
    cache_spec = pl.BlockSpec((SEQ_PER_STEP, WINDOW, KV_WIDTH), lambda i: (i, 0, 0))
    blocks = 2 * BLK * D_MODEL * F32_BYTES + 2 * SEQ_PER_STEP * (STATE_BYTES + 2 * WINDOW_BYTES)
    values = (_mixer_value_bytes(BLK, WINDOW + BLK)
              + HGRN_HEADS * 2 * HGRN_DK * BLK * F32_BYTES)
    vmem = _vmem_limit(MIXER_WEIGHT_BYTES, blocks, values)
    return pl.pallas_call(
        _mixer_sample_kernel,
        grid=(n // BLK,),
        in_specs=[
            pl.BlockSpec(memory_space=pltpu.SMEM),
            pl.BlockSpec((BLK, D_MODEL), lambda i: (i, 0)),
            state_spec, cache_spec, cache_spec,
            _const_spec((D_MODEL, D_IN)),
            _const_spec((D_MIX, D_MODEL)),
            _const_spec((DEPTH + 1, HGRN_KW)),
            _const_spec((1, ATTN_WIDTH)),
            _const_spec((1, HGRN_VW)),
            _const_spec((1, D_MODEL)),
            _const_spec((1, D_MODEL)),
        ],
        out_specs=[
            pl.BlockSpec((BLK, D_MODEL), lambda i: (i, 0)),
            state_spec, cache_spec, cache_spec,
        ],
        out_shape=[
            jax.ShapeDtypeStruct((n, D_MODEL), F32),
            jax.ShapeDtypeStruct((nseq, HGRN_HEADS, HGRN_DK, HGRN_DV), F32),
            jax.ShapeDtypeStruct((nseq, WINDOW, KV_WIDTH), F32),
            jax.ShapeDtypeStruct((nseq, WINDOW, KV_WIDTH), F32),
        ],
        compiler_params=pltpu.CompilerParams(
            dimension_semantics=("arbitrary",), vmem_limit_bytes=vmem),
        name="mixer_sample",
    )(sink, x, s0, ck, cv, w_in, w_out, lbp, ang, hng, g2, b2)


def kernel(x_prompt, x_sample, state_hgrn, cache_win_k, cache_win_v, ln1_g, ln1_b, ffn1_w13,
           ffn1_w2, w_in, lb_param, attn_sink, attn_norm_g, hgrn_norm_g, w_out, ln2_g, ln2_b,
           ffn2_w13, ffn2_w2, ln3_g, ln3_b):
    nb, seq, d = x_prompt.shape
    ns, tn, _ = x_sample.shape
    assert state_hgrn.shape[0] == DEPTH == 1 and d == D_MODEL
    assert seq % (PROMPT_TILES * PROMPT_BLOCKS * BLK) == 0 and tn == TOK and (ns * tn) % BLK == 0
    assert cache_win_k.shape[2] == WINDOW

    w13_1 = ffn1_w13[0].astype(BF16)
    w2_1 = ffn1_w2[0].astype(BF16)
    w13_2 = ffn2_w13[0].astype(BF16)
    w2_2 = ffn2_w2[0].astype(BF16)
    w_in_b = w_in[0].astype(BF16)
    w_out_b = w_out[0].astype(BF16)
    sink = attn_sink[0]
    lbp = lb_param.astype(F32)
    mix_params = (sink, w_in_b, w_out_b, lbp, attn_norm_g, hgrn_norm_g, ln2_g, ln2_b)

    xp = x_prompt.reshape(nb * seq, d)
    xs = x_sample.reshape(ns * tn, d)

    xp = _ffn_ln(xp, w13_1, w2_1, ln1_g, ln1_b)
    xs = _ffn_ln(xs, w13_1, w2_1, ln1_g, ln1_b)

    xp, sp, kp, vp = _mixer_prompt(xp.reshape(nb, seq, d), *mix_params)
    xs, ss, ks, vs = _mixer_sample(
        xs, state_hgrn[0], cache_win_k[0].reshape(ns, WINDOW, KV_WIDTH),
        cache_win_v[0].reshape(ns, WINDOW, KV_WIDTH), *mix_params)

    xp = _ffn_ln(xp.reshape(nb * seq, d), w13_2, w2_2, ln3_g, ln3_b)
    xs = _ffn_ln(xs, w13_2, w2_2, ln3_g, ln3_b)

    cache_tail = (WINDOW, ATTN_KV_HEADS, HEAD_DIM)
    return (xp.reshape(nb, seq, d), xs.reshape(ns, tn, d),
            sp[None], kp.reshape((DEPTH, nb) + cache_tail), vp.reshape((DEPTH, nb) + cache_tail),
            ss[None], ks.reshape((DEPTH, ns) + cache_tail), vs.reshape((DEPTH, ns) + cache_tail))
```

```python
import jax
import jax.numpy as jnp
from jax import lax
from jax.experimental import pallas as pl
from jax.experimental.pallas import tpu as pltpu

F32 = jnp.float32
BF16 = jnp.bfloat16

D_MODEL = 1024
DEPTH = 1
ATTN_HEADS = 8
ATTN_KV_HEADS = 2
GROUP = ATTN_HEADS // ATTN_KV_HEADS
HEAD_DIM = 64
ATTN_WIDTH = ATTN_HEADS * HEAD_DIM
KV_WIDTH = ATTN_KV_HEADS * HEAD_DIM
WINDOW = 128
ATTN_SCALE = HEAD_DIM ** -0.5
HGRN_HEADS = 4
HGRN_DK = 128
HGRN_DV = 128
HGRN_KW = HGRN_HEADS * HGRN_DK
HGRN_VW = HGRN_HEADS * HGRN_DV
D_MIX = ATTN_WIDTH + HGRN_VW
D_IN = ATTN_WIDTH + 2 * KV_WIDTH + 2 * HGRN_KW + 2 * HGRN_VW
D_FF = 2816
ALPHA = (2.0 * DEPTH) ** 0.25
LN_EPS = 1e-5
RMS_EPS = 1e-6

OFF_K = ATTN_WIDTH
OFF_V = OFF_K + KV_WIDTH
OFF_HQ = OFF_V + KV_WIDTH
OFF_HF = OFF_HQ + HGRN_KW
OFF_HI = OFF_HF + HGRN_KW
OFF_HG = OFF_HI + HGRN_VW

LANES = 128
SUBLANES = 8
BLK = 128
PROMPT_BLOCKS = 2
PROMPT_TILES = 2
PROJ_CHUNK = 256
FFN_TILE = 256
FFN_TILES = 4
FFN_CHUNK = 256
TOK = SUBLANES
SEQ_PER_STEP = BLK // TOK
V7X_VMEM_BYTES = 64 * 1024 * 1024
F32_BYTES = 4
BF16_BYTES = 2

NT_DIMS = (((1,), (1,)), ((), ()))
TN_DIMS = (((0,), (0,)), ((), ()))


def _vmem_limit(resident_bytes, streamed_bytes, value_bytes):
    need = resident_bytes + 2 * streamed_bytes + value_bytes
    assert need <= V7X_VMEM_BYTES, need
    return need


_DONE = object()


def _round_robin(streams):
    live = list(streams)
    rounds = 0
    while live:
        live = [g for g in live if next(g, _DONE) is not _DONE]
        rounds += 1
        assert rounds < 100000, "a stream waits for something no other stream produces"


def _dot(a, b):
    return jnp.dot(a, b, preferred_element_type=F32)


def _dot_nt(a, b):
    return lax.dot_general(a, b, NT_DIMS, preferred_element_type=F32)


def _dot_tn(a, b):
    return lax.dot_general(a, b, TN_DIMS, preferred_element_type=F32)


def _layer_norm(r, g, b):
    mu = jnp.mean(r, axis=-1, keepdims=True)
    c = r - mu
    var = jnp.mean(c * c, axis=-1, keepdims=True)
    return c * lax.rsqrt(var + LN_EPS) * g + b


def _rms_norm(x, g):
    return x * lax.rsqrt(jnp.mean(x * x, axis=-1, keepdims=True) + RMS_EPS) * g


def _ffn_steps(x, w13_ref, w2_ref, g, b, store, on_hidden=lambda: None):
    xb = x.astype(BF16)
    acts = []
    for c in range(D_FF // FFN_CHUNK):
        gate = _dot(xb, w13_ref[:, FFN_CHUNK * c:FFN_CHUNK * (c + 1)])
        up = _dot(xb, w13_ref[:, D_FF + FFN_CHUNK * c:D_FF + FFN_CHUNK * (c + 1)])
        acts.append((jax.nn.silu(gate) * up).astype(BF16))
        yield
    act = jnp.concatenate(acts, axis=1)
    on_hidden()
    ys = []
    for c in range(D_MODEL // FFN_CHUNK):
        ys.append(_dot(act, w2_ref[:, FFN_CHUNK * c:FFN_CHUNK * (c + 1)]))
        yield
    store(_layer_norm(ALPHA * x + 0.5 * jnp.concatenate(ys, axis=1), g, b))


def _ffn_stream(x_ref, o_ref, tile, w13_ref, w2_ref, g, b, flags, may_start):
    while not may_start():
        yield
    rows = slice(FFN_TILE * tile, FFN_TILE * (tile + 1))

    def store(y):
        o_ref[rows, :] = y

    def on_hidden():
        flags[tile] = True

    yield from _ffn_steps(x_ref[rows, :], w13_ref, w2_ref, g, b, store, on_hidden)


def _ffn_ln_kernel(x_ref, w13_ref, w2_ref, g_ref, b_ref, o_ref):
    n_tiles = x_ref.shape[0] // FFN_TILE
    g, b = g_ref[...], b_ref[...]
    flags = [False] * n_tiles
    _round_robin(
        _ffn_stream(x_ref, o_ref, t, w13_ref, w2_ref, g, b, flags,
                    (lambda: True) if t == 0 else (lambda t=t: flags[t - 1]))
        for t in range(n_tiles))


def _const_spec(shape):
    return pl.BlockSpec(shape, lambda *_: (0,) * len(shape), pipeline_mode=pl.Buffered(1))


def _ffn_ln(x, w13, w2, g, b):
    n = x.shape[0]
    tm = FFN_TILE * FFN_TILES
    assert n % tm == 0
    weights = (D_MODEL * 2 * D_FF + D_FF * D_MODEL) * BF16_BYTES
    blocks = 2 * tm * D_MODEL * F32_BYTES
    values = 2 * FFN_TILE * ((D_MODEL + D_FF) * BF16_BYTES
                             + (2 * FFN_CHUNK + 2 * D_MODEL) * F32_BYTES)
    vmem = _vmem_limit(weights, blocks, values)
    return pl.pallas_call(
        _ffn_ln_kernel,
        grid=(n // tm,),
        in_specs=[
            pl.BlockSpec((tm, D_MODEL), lambda i: (i, 0)),
            _const_spec((D_MODEL, 2 * D_FF)),
            _const_spec((D_FF, D_MODEL)),
            _const_spec((1, D_MODEL)),
            _const_spec((1, D_MODEL)),
        ],
        out_specs=pl.BlockSpec((tm, D_MODEL), lambda i: (i, 0)),
        out_shape=jax.ShapeDtypeStruct((n, D_MODEL), F32),
        compiler_params=pltpu.CompilerParams(
            dimension_semantics=("arbitrary",), vmem_limit_bytes=vmem),
        name="ffn_ln",
    )(x, w13, w2, g, b)


def _lower_bound(lbp):
    mx = jnp.max(lbp, axis=0, keepdims=True)
    e = jnp.exp(lbp - mx)
    return e[0:1] / jnp.sum(e, axis=0, keepdims=True)


def _split3(x):
    hi = x.astype(BF16)
    r = x - hi.astype(F32)
    mid = r.astype(BF16)
    lo = (r - mid.astype(F32)).astype(BF16)
    return hi, mid, lo


def _block_cumsum(tri, g):
    w = g.shape[1]
    parts = jnp.concatenate(_split3(g), axis=1)
    s = _dot(tri, parts)
    return s[:, :w] + s[:, w:2 * w] + s[:, 2 * w:]


def _tri_matrix(group):
    r = lax.broadcasted_iota(jnp.int32, (BLK, BLK), 0)
    c = lax.broadcasted_iota(jnp.int32, (BLK, BLK), 1)
    keep = (c <= r) & ((r ^ c) < group)
    return jnp.where(keep, 1.0, 0.0).astype(BF16)


def _pair_ref_rows(b, m):
    t, w = b.shape
    if m >= SUBLANES:
        b3 = b.reshape(t // (2 * m), 2 * m, w)
        return jnp.broadcast_to(b3[:, m - 1:m, :], b3.shape).reshape(t, w)
    b3 = b.reshape(t // SUBLANES, SUBLANES, w)
    sub = lax.broadcasted_iota(jnp.int32, b3.shape, 1)

    def row(i):
        return jnp.broadcast_to(b3[:, i:i + 1, :], b3.shape)

    if m == 4:
        out = row(3)
    elif m == 2:
        out = jnp.where(sub < 4, row(1), row(5))
    else:
        out = jnp.where(sub < 4, jnp.where(sub < 2, row(0), row(2)),
                        jnp.where(sub < 6, row(4), row(6)))
    return out.reshape(t, w)


def _level_operands(q, k, e, m):
    t, w = q.shape
    if m >= SUBLANES:
        zero = jnp.zeros((m, w), F32)
        qs, ks = [], []
        for blk in range(t // m):
            rows = slice(blk * m, (blk + 1) * m)
            if blk % 2:
                qs.append(q[rows] * e[rows])
                ks.append(zero)
            else:
                qs.append(zero)
                ks.append(k[rows] * e[rows])
        return jnp.concatenate(qs, axis=0).astype(BF16), jnp.concatenate(ks, axis=0).astype(BF16)
    upper = (lax.broadcasted_iota(jnp.int32, q.shape, 0) & m) != 0
    return (jnp.where(upper, q * e, 0.0).astype(BF16), jnp.where(upper, 0.0, k * e).astype(BF16))


def _intra_steps(q, k, b, max_m, xor, out):
    a = jnp.where(xor == 0, jnp.sum(q * k, axis=-1, keepdims=True), 0.0)
    yield
    m = 1
    while m <= max_m:
        e = jnp.exp(-jnp.abs(b - _pair_ref_rows(b, m)))
        qs, ks = _level_operands(q, k, e, m)
        yield
        p = _dot_nt(qs, ks)
        yield
        a = jnp.where(xor >= m, p, a)
        m *= 2
    if 2 * max_m < q.shape[0]:
        a = jnp.where(xor < 2 * max_m, a, 0.0)
    out["a"] = a


def _intra_scores(q, k, b, max_m, xor):
    out = {}
    for _ in _intra_steps(q, k, b, max_m, xor, out):
        pass
    return out["a"]


def _hgrn_gates(proj, lb):
    q = jax.nn.silu(proj[:, OFF_HQ:OFF_HF])
    f = lb + (1.0 - lb) * jax.nn.sigmoid(proj[:, OFF_HF:OFF_HI])
    return q, jnp.log(f), 1.0 - f, proj[:, OFF_HI:OFF_HG]


def _head_queries(qa):
    low = lax.broadcasted_iota(jnp.int32, (qa.shape[0], LANES), 1) < HEAD_DIM
    out = []
    for pair in range(ATTN_HEADS // 2):
        piece = qa[:, LANES * pair:LANES * (pair + 1)] * ATTN_SCALE
        rolled = pltpu.roll(piece, HEAD_DIM, axis=1)
        for half in range(2):
            kv = (2 * pair + half) // GROUP
            src = piece if half == kv else rolled
            z = jnp.where(low, src, 0.0) if kv == 0 else jnp.where(low, 0.0, src)
            out.append(z.astype(BF16))
    return out


def _merge_heads(o):
    low = lax.broadcasted_iota(jnp.int32, o[0].shape, 1) < HEAD_DIM
    cols = []
    for pair in range(ATTN_HEADS // 2):
        kv = (2 * pair) // GROUP
        even, odd = o[2 * pair], o[2 * pair + 1]
        if kv == 0:
            odd = pltpu.roll(odd, HEAD_DIM, axis=1)
        else:
            even = pltpu.roll(even, HEAD_DIM, axis=1)
        cols.append(jnp.where(low, even, odd))
    return jnp.concatenate(cols, axis=1)


def _slope(head):
    return 2.0 ** (-8.0 * (head + 1) / ATTN_HEADS)


def _attend_steps(qz, k_all, v_all, sink_ref, limit, out):
    tq = qz[0].shape[0]
    nk = k_all.shape[0]
    s_all = _dot_nt(jnp.concatenate(qz, axis=0), k_all.astype(BF16))
    trow = lax.broadcasted_iota(jnp.int32, (tq, nk), 0)
    ccol = lax.broadcasted_iota(jnp.int32, (tq, nk), 1)
    dist = trow + WINDOW - ccol
    visible = dist.astype(jnp.uint32) < limit.astype(jnp.uint32)
    distf = dist.astype(F32)
    yield
    ps, dens = [], []
    for head in range(ATTN_HEADS):
        sink = sink_ref[head]
        s = s_all[tq * head:tq * (head + 1)]
        s = jnp.where(visible, s - _slope(head) * distf, -jnp.inf)
        mx = jnp.maximum(jnp.max(s, axis=-1, keepdims=True), sink)
        yield
        p = jnp.exp(s - mx)
        dens.append(jnp.sum(p, axis=-1, keepdims=True) + jnp.exp(sink - mx))
        ps.append(p.astype(BF16))
        yield
    o_all = _dot(jnp.concatenate(ps, axis=0), v_all.astype(BF16))
    yield
    out["oa"] = _merge_heads([o_all[tq * h:tq * (h + 1)] / dens[h] for h in range(ATTN_HEADS)])


def _attend_sample(qz, k_new, v_new, ck_ref, cv_ref, sink_ref):
    rows = qz[0].shape[0]
    k_new = k_new.astype(BF16)
    v_new = v_new.astype(BF16)
    sn_all = _dot_nt(jnp.concatenate(qz, axis=0), k_new)
    sc_seq = []
    for i in range(SEQ_PER_STEP):
        lhs = jnp.concatenate([z[TOK * i:TOK * (i + 1)] for z in qz], axis=0)
        sc_seq.append(_dot_nt(lhs, ck_ref[i].astype(BF16)))

    r = lax.broadcasted_iota(jnp.int32, (rows, rows), 0)
    c = lax.broadcasted_iota(jnp.int32, (rows, rows), 1)
    new_visible = ((r ^ c) < TOK) & (c <= r)
    new_dist = (r - c).astype(F32)
    tok = lax.broadcasted_iota(jnp.int32, (rows, WINDOW), 0) & (TOK - 1)
    key = lax.broadcasted_iota(jnp.int32, (rows, WINDOW), 1)
    old_visible = key > tok
    old_dist = (WINDOW + tok - key).astype(F32)

    pn, pc, dens = [], [], []
    for head in range(ATTN_HEADS):
        sink = sink_ref[head]
        slope = _slope(head)
        sn = sn_all[rows * head:rows * (head + 1)]
        sc = jnp.concatenate([s[TOK * head:TOK * (head + 1)] for s in sc_seq], axis=0)
        sn = jnp.where(new_visible, sn - slope * new_dist, -jnp.inf)
        sc = jnp.where(old_visible, sc - slope * old_dist, -jnp.inf)
        mx = jnp.maximum(jnp.maximum(jnp.max(sn, axis=-1, keepdims=True),
                                     jnp.max(sc, axis=-1, keepdims=True)), sink)
        en = jnp.exp(sn - mx)
        ec = jnp.exp(sc - mx)
        dens.append(jnp.sum(en, axis=-1, keepdims=True) + jnp.sum(ec, axis=-1, keepdims=True)
                    + jnp.exp(sink - mx))
        pn.append(en.astype(BF16))
        pc.append(ec.astype(BF16))

    on_all = _dot(jnp.concatenate(pn, axis=0), v_new)
    oc_seq = []
    for i in range(SEQ_PER_STEP):
        lhs = jnp.concatenate([p[TOK * i:TOK * (i + 1)] for p in pc], axis=0)
        oc_seq.append(_dot(lhs, cv_ref[i].astype(BF16)))
    o = []
    for head in range(ATTN_HEADS):
        oc = jnp.concatenate([s[TOK * head:TOK * (head + 1)] for s in oc_seq], axis=0)
        o.append((on_all[rows * head:rows * (head + 1)] + oc) / dens[head])
    return _merge_heads(o)


def _mix_out(x, oa, oh, proj, w_out_ref, ang_ref, hng_ref, g2_ref, b2_ref):
    oa = _rms_norm(oa, ang_ref[...])
    hng = hng_ref[...]
    gate = jax.nn.silu(proj[:, OFF_HG:])
    ohn = [_rms_norm(oh[h], hng[:, HGRN_DV * h:HGRN_DV * (h + 1)]) for h in range(HGRN_HEADS)]
    mix = jnp.concatenate([oa, jnp.concatenate(ohn, axis=1) * gate], axis=1).astype(BF16)
    y = _dot(mix, w_out_ref[...])
    return _layer_norm(ALPHA * x + y, g2_ref[...], b2_ref[...])


def _proj_stream(ctx, x, w_in_ref, may_start):
    while not may_start():
        yield
    xb = x.astype(BF16)
    chunks = []
    for c in range(D_IN // PROJ_CHUNK):
        chunks.append(_dot(xb, w_in_ref[:, PROJ_CHUNK * c:PROJ_CHUNK * (c + 1)]))
        yield
    ctx["proj"] = jnp.concatenate(chunks, axis=1)


def _prepare_stream(ctx, lb, tri):
    while "proj" not in ctx:
        yield
    proj = ctx["proj"]
    ctx["qz"] = _head_queries(proj[:, :ATTN_WIDTH])
    yield
    ctx["q"], logf, ctx["kin"], ctx["vin"] = _hgrn_gates(proj, lb)
    yield
    ctx["b"] = [_block_cumsum(tri, logf[BLK * blk:BLK * (blk + 1)]) for blk in range(PROMPT_BLOCKS)]
    ctx["ready"] = True


def _attention_stream(ctx, blk, prev_kv, limit, sink_ref):
    while "ready" not in ctx:
        yield
    rows = slice(BLK * blk, BLK * (blk + 1))
    k_prev, v_prev = prev_kv()
    k_all = jnp.concatenate([k_prev, ctx["proj"][rows, OFF_K:OFF_V]], axis=0)
    v_all = jnp.concatenate([v_prev, ctx["proj"][rows, OFF_V:OFF_HQ]], axis=0)
    out = {}
    yield from _attend_steps([z[rows] for z in ctx["qz"]], k_all, v_all, sink_ref, limit, out)
    ctx["oa"][blk] = out["oa"]


def _hgrn_stream(ctx, blk, h, state, turn, xor):
    while "ready" not in ctx:
        yield
    rows = slice(BLK * blk, BLK * (blk + 1))
    sl = slice(HGRN_DK * h, HGRN_DK * (h + 1))
    qh, kh, bh = ctx["q"][rows, sl], ctx["kin"][rows, sl], ctx["b"][blk][:, sl]
    vh = ctx["vin"][rows, sl].astype(BF16)
    scores = {}
    yield from _intra_steps(qh, kh, bh, BLK // 2, xor, scores)
    o_intra = _dot(scores["a"].astype(BF16), vh)
    yield
    b_last = bh[BLK - 1:BLK, :]
    qe = (qh * jnp.exp(bh)).astype(BF16)
    kd = (kh * jnp.exp(b_last - bh)).astype(BF16)
    yield
    while state["turn"][h] != turn:
        yield
    st = state["st"][h]
    o_inter = _dot_nt(qe, st.astype(BF16))
    update = _dot_tn(vh, kd)
    yield
    state["st"][h] = st * jnp.exp(b_last) + update
    state["turn"][h] = turn + 1
    ctx["oh"][blk][h] = o_intra + o_inter


def _mix_out_stream(ctx, x, y_store, w_out_ref, ang_ref, hng_ref, g2_ref, b2_ref):
    while any(o is None for o in ctx["oa"]) or any(o is None for blk in ctx["oh"] for o in blk):
        yield
    oa = jnp.concatenate(ctx["oa"], axis=0)
    oh = [jnp.concatenate([ctx["oh"][blk][h] for blk in range(PROMPT_BLOCKS)], axis=0)
          for h in range(HGRN_HEADS)]
    y_store(_mix_out(x, oa, oh, ctx["proj"], w_out_ref, ang_ref, hng_ref, g2_ref, b2_ref))


def _mixer_prompt_kernel(sink_ref, x_ref, w_in_ref, w_out_ref, lbp_ref, ang_ref, hng_ref, g2_ref,
                         b2_ref, y_ref, s_out_ref, k_out_ref, v_out_ref, st_ref, kv_ref):
    j = pl.program_id(1)

    @pl.when(j == 0)
    def _():
        st_ref[...] = jnp.zeros_like(st_ref)
        kv_ref[...] = jnp.zeros_like(kv_ref)

    lb = _lower_bound(lbp_ref[...])
    tri = _tri_matrix(BLK)
    xor = (lax.broadcasted_iota(jnp.int32, (BLK, BLK), 0)
           ^ lax.broadcasted_iota(jnp.int32, (BLK, BLK), 1))
    trow = lax.broadcasted_iota(jnp.int32, (BLK, 2 * WINDOW), 0)
    full = jnp.full(trow.shape, WINDOW, jnp.int32)

    state = {"st": [st_ref[h] for h in range(HGRN_HEADS)], "turn": [0] * HGRN_HEADS}
    ctxs = [{"oa": [None] * PROMPT_BLOCKS,
             "oh": [[None] * HGRN_HEADS for _ in range(PROMPT_BLOCKS)]} for _ in range(PROMPT_TILES)]

    def kv_block(tile, blk):
        rows = slice(BLK * blk, BLK * (blk + 1))
        proj = ctxs[tile]["proj"]
        return proj[rows, OFF_K:OFF_V], proj[rows, OFF_V:OFF_HQ]

    streams = []
    for tile in range(PROMPT_TILES):
        ctx = ctxs[tile]
        x = x_ref[tile]
        may_start = (lambda: True) if tile == 0 else (lambda prev=ctxs[tile - 1]: "ready" in prev)
        streams.append(_proj_stream(ctx, x, w_in_ref, may_start))
        streams.append(_prepare_stream(ctx, lb, tri))
        for blk in range(PROMPT_BLOCKS):
            if tile == 0 and blk == 0:
                prev_kv = lambda: (kv_ref[0], kv_ref[1])
                limit = jnp.where(j > 0, WINDOW, trow + 1)
            else:
                g = tile * PROMPT_BLOCKS + blk - 1
                prev_kv = lambda g=g: kv_block(g // PROMPT_BLOCKS, g % PROMPT_BLOCKS)
                limit = full
            streams.append(_attention_stream(ctx, blk, prev_kv, limit, sink_ref))
            for h in range(HGRN_HEADS):
                streams.append(_hgrn_stream(ctx, blk, h, state, tile * PROMPT_BLOCKS + blk, xor))

        def y_store(y, tile=tile):
            y_ref[tile] = y

        streams.append(_mix_out_stream(ctx, x, y_store, w_out_ref, ang_ref, hng_ref, g2_ref, b2_ref))
    _round_robin(streams)

    k_last, v_last = kv_block(PROMPT_TILES - 1, PROMPT_BLOCKS - 1)
    kv_ref[0] = k_last
    kv_ref[1] = v_last
    for h in range(HGRN_HEADS):
        st_ref[h] = state["st"][h]

    @pl.when(j == pl.num_programs(1) - 1)
    def _():
        k_out_ref[0] = k_last
        v_out_ref[0] = v_last
        for h in range(HGRN_HEADS):
            s_out_ref[0, h] = state["st"][h].T


MIXER_WEIGHT_BYTES = (D_MODEL * D_IN + D_MIX * D_MODEL) * BF16_BYTES
STATE_BYTES = HGRN_HEADS * HGRN_DK * HGRN_DV * F32_BYTES
WINDOW_BYTES = WINDOW * KV_WIDTH * F32_BYTES


def _mixer_value_bytes(rows, keys):
    per_row = ((D_IN + 4 * HGRN_KW + 2 * D_MODEL) * F32_BYTES
               + (ATTN_HEADS * LANES + D_MIX) * BF16_BYTES)
    scores = ATTN_HEADS * BLK * keys * (F32_BYTES + BF16_BYTES)
    return rows * per_row + (rows // BLK) * scores


def _mixer_prompt(x, sink, w_in, w_out, lbp, ang, hng, g2, b2):
    nb, seq, _ = x.shape
    tile_rows = PROMPT_BLOCKS * BLK
    steps_per_seq = seq // (PROMPT_TILES * tile_rows)
    n_tiles = nb * seq // tile_rows
    xt = x.reshape(n_tiles, tile_rows, D_MODEL)
    blocks = (2 * PROMPT_TILES * tile_rows * D_MODEL * F32_BYTES
              + STATE_BYTES + 2 * WINDOW_BYTES)
    vmem = _vmem_limit(MIXER_WEIGHT_BYTES + STATE_BYTES + 2 * WINDOW_BYTES, blocks,
                       _mixer_value_bytes(2 * tile_rows, 2 * WINDOW))
    tile_spec = pl.BlockSpec((PROMPT_TILES, tile_rows, D_MODEL),
                             lambda b, j: (b * steps_per_seq + j, 0, 0))
    y, s, k, v = pl.pallas_call(
        _mixer_prompt_kernel,
        grid=(nb, steps_per_seq),
        in_specs=[
            pl.BlockSpec(memory_space=pltpu.SMEM),
            tile_spec,
            _const_spec((D_MODEL, D_IN)),
            _const_spec((D_MIX, D_MODEL)),
            _const_spec((DEPTH + 1, HGRN_KW)),
            _const_spec((1, ATTN_WIDTH)),
            _const_spec((1, HGRN_VW)),
            _const_spec((1, D_MODEL)),
            _const_spec((1, D_MODEL)),
        ],
        out_specs=[
            tile_spec,
            pl.BlockSpec((1, HGRN_HEADS, HGRN_DK, HGRN_DV), lambda b, j: (b, 0, 0, 0)),
            pl.BlockSpec((1, WINDOW, KV_WIDTH), lambda b, j: (b, 0, 0)),
            pl.BlockSpec((1, WINDOW, KV_WIDTH), lambda b, j: (b, 0, 0)),
        ],
        out_shape=[
            jax.ShapeDtypeStruct((n_tiles, tile_rows, D_MODEL), F32),
            jax.ShapeDtypeStruct((nb, HGRN_HEADS, HGRN_DK, HGRN_DV), F32),
            jax.ShapeDtypeStruct((nb, WINDOW, KV_WIDTH), F32),
            jax.ShapeDtypeStruct((nb, WINDOW, KV_WIDTH), F32),
        ],
        scratch_shapes=[
            pltpu.VMEM((HGRN_HEADS, HGRN_DV, HGRN_DK), F32),
            pltpu.VMEM((2, WINDOW, KV_WIDTH), F32),
        ],
        compiler_params=pltpu.CompilerParams(
            dimension_semantics=("arbitrary", "arbitrary"), vmem_limit_bytes=vmem),
        name="mixer_prompt",
    )(sink, xt, w_in, w_out, lbp, ang, hng, g2, b2)
    return y.reshape(nb, seq, D_MODEL), s, k, v


def _mixer_sample_kernel(sink_ref, x_ref, s0_ref, ck_ref, cv_ref, w_in_ref, w_out_ref, lbp_ref,
                         ang_ref, hng_ref, g2_ref, b2_ref, y_ref, s_out_ref, k_out_ref, v_out_ref):
    x = x_ref[...]
    proj = _dot(x.astype(BF16), w_in_ref[...])
    k_new = proj[:, OFF_K:OFF_V]
    v_new = proj[:, OFF_V:OFF_HQ]

    oa = _attend_sample(_head_queries(proj[:, :ATTN_WIDTH]), k_new, v_new, ck_ref, cv_ref, sink_ref)
    for i in range(SEQ_PER_STEP):
        rows = slice(TOK * i, TOK * (i + 1))
        k_out_ref[i] = jnp.concatenate([ck_ref[i, TOK:, :], k_new[rows]], axis=0)
        v_out_ref[i] = jnp.concatenate([cv_ref[i, TOK:, :], v_new[rows]], axis=0)

    q, logf, kin, vin = _hgrn_gates(proj, _lower_bound(lbp_ref[...]))
    b = _block_cumsum(_tri_matrix(TOK), logf)
    xor = (lax.broadcasted_iota(jnp.int32, (BLK, BLK), 0)
           ^ lax.broadcasted_iota(jnp.int32, (BLK, BLK), 1))
    lane = lax.broadcasted_iota(jnp.int32, (HGRN_DK, BLK), 1)
    oh = []
    for h in range(HGRN_HEADS):
        sl = slice(HGRN_DK * h, HGRN_DK * (h + 1))
        qh, kh, bh = q[:, sl], kin[:, sl], b[:, sl]
        vh = vin[:, sl].astype(BF16)
        a = _intra_scores(qh, kh, bh, TOK // 2, xor)
        o_intra = _dot(a.astype(BF16), vh)
        qe = (qh * jnp.exp(bh)).astype(BF16)
        b3 = bh.reshape(SEQ_PER_STEP, TOK, HGRN_DK)
        b_last = jnp.broadcast_to(b3[:, TOK - 1:TOK, :], b3.shape).reshape(BLK, HGRN_DK)
        kd_t = (kh * jnp.exp(b_last - bh)).T
        dec_t = jnp.exp(b_last).T
        o_rows = []
        for i in range(SEQ_PER_STEP):
            s0 = s0_ref[i, h]
            o_rows.append(_dot(qe[TOK * i:TOK * (i + 1)], s0.astype(BF16)))
            kd_i = jnp.where((lane >= TOK * i) & (lane < TOK * (i + 1)), kd_t, 0.0).astype(BF16)
            s_out_ref[i, h] = s0 * dec_t[:, TOK * i:TOK * i + 1] + _dot(kd_i, vh)
        oh.append(o_intra + jnp.concatenate(o_rows, axis=0))

    y_ref[...] = _mix_out(x, oa, oh, proj, w_out_ref, ang_ref, hng_ref, g2_ref, b2_ref)


def _mixer_sample(x, s0, ck, cv, sink, w_in, w_out, lbp, ang, hng, g2, b2):
    n = x.shape[0]
    nseq = s0.shape[0]
    state_spec = pl.BlockSpec((SEQ_PER_STEP, HGRN_HEADS, HGRN_DK, HGRN_DV), lambda i: (i, 0, 0, 0))
    cache_spec = pl.BlockSpec((SEQ_PER_STEP, WINDOW, KV_WIDTH), lambda i: (i, 0, 0))
    blocks = 2 * BLK * D_MODEL * F32_BYTES + 2 * SEQ_PER_STEP * (STATE_BYTES + 2 * WINDOW_BYTES)
    values = (_mixer_value_bytes(BLK, WINDOW + BLK)
              + HGRN_HEADS * 2 * HGRN_DK * BLK * F32_BYTES)
    vmem = _vmem_limit(MIXER_WEIGHT_BYTES, blocks, values)
    return pl.pallas_call(
        _mixer_sample_kernel,
        grid=(n // BLK,),
        in_specs=[
            pl.BlockSpec(memory_space=pltpu.SMEM),
            pl.BlockSpec((BLK, D_MODEL), lambda i: (i, 0)),
            state_spec, cache_spec, cache_spec,
            _const_spec((D_MODEL, D_IN)),
            _const_spec((D_MIX, D_MODEL)),
            _const_spec((DEPTH + 1, HGRN_KW)),
            _const_spec((1, ATTN_WIDTH)),
            _const_spec((1, HGRN_VW)),
            _const_spec((1, D_MODEL)),
            _const_spec((1, D_MODEL)),
        ],
        out_specs=[
            pl.BlockSpec((BLK, D_MODEL), lambda i: (i, 0)),
            state_spec, cache_spec, cache_spec,
        ],
        out_shape=[
            jax.ShapeDtypeStruct((n, D_MODEL), F32),
            jax.ShapeDtypeStruct((nseq, HGRN_HEADS, HGRN_DK, HGRN_DV), F32),
            jax.ShapeDtypeStruct((nseq, WINDOW, KV_WIDTH), F32),
            jax.ShapeDtypeStruct((nseq, WINDOW, KV_WIDTH), F32),
        ],
        compiler_params=pltpu.CompilerParams(
            dimension_semantics=("arbitrary",), vmem_limit_bytes=vmem),
        name="mixer_sample",
    )(sink, x, s0, ck, cv, w_in, w_out, lbp, ang, hng, g2, b2)


def kernel(x_prompt, x_sample, state_hgrn, cache_win_k, cache_win_v, ln1_g, ln1_b, ffn1_w13,
           ffn1_w2, w_in, lb_param, attn_sink, attn_norm_g, hgrn_norm_g, w_out, ln2_g, ln2_b,
           ffn2_w13, ffn2_w2, ln3_g, ln3_b):
    nb, seq, d = x_prompt.shape
    ns, tn, _ = x_sample.shape
    assert state_hgrn.shape[0] == DEPTH == 1 and d == D_MODEL
    assert seq % (PROMPT_TILES * PROMPT_BLOCKS * BLK) == 0 and tn == TOK and (ns * tn) % BLK == 0
    assert cache_win_k.shape[2] == WINDOW

    w13_1 = ffn1_w13[0].astype(BF16)
    w2_1 = ffn1_w2[0].astype(BF16)
    w13_2 = ffn2_w13[0].astype(BF16)
    w2_2 = ffn2_w2[0].astype(BF16)
    w_in_b = w_in[0].astype(BF16)
    w_out_b = w_out[0].astype(BF16)
    sink = attn_sink[0]
    lbp = lb_param.astype(F32)
    mix_params = (sink, w_in_b, w_out_b, lbp, attn_norm_g, hgrn_norm_g, ln2_g, ln2_b)

    xp = x_prompt.reshape(nb * seq, d)
    xs = x_sample.reshape(ns * tn, d)

    xp = _ffn_ln(xp, w13_1, w2_1, ln1_g, ln1_b)
    xs = _ffn_ln(xs, w13_1, w2_1, ln1_g, ln1_b)

    xp, sp, kp, vp = _mixer_prompt(xp.reshape(nb, seq, d), *mix_params)
    xs, ss, ks, vs = _mixer_sample(
        xs, state_hgrn[0], cache_win_k[0].reshape(ns, WINDOW, KV_WIDTH),
        cache_win_v[0].reshape(ns, WINDOW, KV_WIDTH), *mix_params)

    xp = _ffn_ln(xp.reshape(nb * seq, d), w13_2, w2_2, ln3_g, ln3_b)
    xs = _ffn_ln(xs, w13_2, w2_2, ln3_g, ln3_b)

    cache_tail = (WINDOW, ATTN_KV_HEADS, HEAD_DIM)
    return (xp.reshape(nb, seq, d), xs.reshape(ns, tn, d),
            sp[None], kp.reshape((DEPTH, nb) + cache_tail), vp.reshape((DEPTH, nb) + cache_tail),
            ss[None], ks.reshape((DEPTH, ns) + cache_tail), vs.reshape((DEPTH, ns) + cache_tail))
```

```python
import functools

import jax
import jax.numpy as jnp
from jax import lax
from jax.experimental import pallas as pl
from jax.experimental.pallas import tpu as pltpu

F32 = jnp.float32
BF16 = jnp.bfloat16

D_MODEL = 1024
DEPTH = 1
ATTN_HEADS = 8
ATTN_KV_HEADS = 2
GROUP = ATTN_HEADS // ATTN_KV_HEADS
HEAD_DIM = 64
ATTN_WIDTH = ATTN_HEADS * HEAD_DIM
KV_WIDTH = ATTN_KV_HEADS * HEAD_DIM
WINDOW = 128
ATTN_SCALE = HEAD_DIM ** -0.5
HGRN_HEADS = 4
HGRN_DK = 128
HGRN_DV = 128
HGRN_KW = HGRN_HEADS * HGRN_DK
HGRN_VW = HGRN_HEADS * HGRN_DV
D_MIX = ATTN_WIDTH + HGRN_VW
D_IN = ATTN_WIDTH + 2 * KV_WIDTH + 2 * HGRN_KW + 2 * HGRN_VW
D_FF = 2816
ALPHA = (2.0 * DEPTH) ** 0.25
LN_EPS = 1e-5
RMS_EPS = 1e-6

OFF_K = ATTN_WIDTH
OFF_V = OFF_K + KV_WIDTH
OFF_HQ = OFF_V + KV_WIDTH
OFF_HF = OFF_HQ + HGRN_KW
OFF_HI = OFF_HF + HGRN_KW
OFF_HG = OFF_HI + HGRN_VW

LANES = 128
SUBLANES = 8
BLK = 128
PROMPT_BLOCKS = 2
PROMPT_TILES = 2
PROJ_CHUNK = 256
FFN_TILE = 256
FFN_TILES = 4
FFN_CHUNK = 256
TOK = SUBLANES
SEQ_PER_STEP = BLK // TOK
V7X_VMEM_BYTES = 64 * 1024 * 1024
F32_BYTES = 4
BF16_BYTES = 2

NT_DIMS = (((1,), (1,)), ((), ()))
TN_DIMS = (((0,), (0,)), ((), ()))


def _vmem_limit(resident_bytes, streamed_bytes, value_bytes):
    need = resident_bytes + 2 * streamed_bytes + value_bytes
    assert need <= V7X_VMEM_BYTES, need
    return need


_DONE = object()


def _round_robin(streams):
    live = list(streams)
    rounds = 0
    while live:
        live = [g for g in live if next(g, _DONE) is not _DONE]
        rounds += 1
        assert rounds < 100000, "a stream waits for something no other stream produces"


def _dot(a, b):
    return jnp.dot(a, b, preferred_element_type=F32)


def _dot_nt(a, b):
    return lax.dot_general(a, b, NT_DIMS, preferred_element_type=F32)


def _dot_tn(a, b):
    return lax.dot_general(a, b, TN_DIMS, preferred_element_type=F32)


def _layer_norm(r, g, b):
    mu = jnp.mean(r, axis=-1, keepdims=True)
    c = r - mu
    var = jnp.mean(c * c, axis=-1, keepdims=True)
    return c * lax.rsqrt(var + LN_EPS) * g + b


def _rms_norm(x, g):
    return x * lax.rsqrt(jnp.mean(x * x, axis=-1, keepdims=True) + RMS_EPS) * g


def _ffn_steps(x, w13_ref, w2_ref, g, b, store, on_hidden=lambda: None):
    xb = x.astype(BF16)
    acts = []
    for c in range(D_FF // FFN_CHUNK):
        gate = _dot(xb, w13_ref[:, FFN_CHUNK * c:FFN_CHUNK * (c + 1)])
        up = _dot(xb, w13_ref[:, D_FF + FFN_CHUNK * c:D_FF + FFN_CHUNK * (c + 1)])
        acts.append((jax.nn.silu(gate) * up).astype(BF16))
        yield
    act = jnp.concatenate(acts, axis=1)
    on_hidden()
    ys = []
    for c in range(D_MODEL // FFN_CHUNK):
        ys.append(_dot(act, w2_ref[:, FFN_CHUNK * c:FFN_CHUNK * (c + 1)]))
        yield
    store(_layer_norm(ALPHA * x + 0.5 * jnp.concatenate(ys, axis=1), g, b))


def _ffn_stream(x_ref, o_ref, tile, w13_ref, w2_ref, g, b, flags, may_start):
    while not may_start():
        yield
    rows = slice(FFN_TILE * tile, FFN_TILE * (tile + 1))

    def store(y):
        o_ref[rows, :] = y

    def on_hidden():
        flags[tile] = True

    yield from _ffn_steps(x_ref[rows, :], w13_ref, w2_ref, g, b, store, on_hidden)


def _cast_stream(src_refs, dst_refs):
    for src, dst in zip(src_refs, dst_refs):
        dst[...] = src[...].astype(BF16)
        yield


def _ffn_ln_kernel(n_cast, x_ref, w13_ref, w2_ref, g_ref, b_ref, *refs):
    cast_src, o_ref, cast_dst = refs[:n_cast], refs[n_cast], refs[n_cast + 1:]
    n_tiles = x_ref.shape[0] // FFN_TILE
    g, b = g_ref[...], b_ref[...]
    flags = [False] * n_tiles
    streams = [_ffn_stream(x_ref, o_ref, t, w13_ref, w2_ref, g, b, flags,
                           (lambda: True) if t == 0 else (lambda t=t: flags[t - 1]))
               for t in range(n_tiles)]
    _round_robin(streams + [_cast_stream(cast_src, cast_dst)])


def _const_spec(shape):
    return pl.BlockSpec(shape, lambda *_: (0,) * len(shape), pipeline_mode=pl.Buffered(1))


def _slab_spec(shape, steps):
    rows, cols = shape
    count = max(c for c in range(1, steps + 1)
                if rows % c == 0 and (rows // c) % (2 * SUBLANES) == 0)
    return pl.BlockSpec((rows // count, cols), lambda i: (jnp.minimum(i, count - 1), 0))


def _ffn_ln(x, w13, w2, g, b, to_cast=()):
    n = x.shape[0]
    tm = FFN_TILE * FFN_TILES
    assert n % tm == 0
    steps = n // tm
    slabs = [_slab_spec(w.shape, steps) for w in to_cast]
    weights = (D_MODEL * 2 * D_FF + D_FF * D_MODEL) * BF16_BYTES
    blocks = (2 * tm * D_MODEL * F32_BYTES
              + sum(s.block_shape[0] * s.block_shape[1] for s in slabs) * (F32_BYTES + BF16_BYTES))
    values = 2 * FFN_TILE * ((D_MODEL + D_FF) * BF16_BYTES
                             + (2 * FFN_CHUNK + 2 * D_MODEL) * F32_BYTES)
    vmem = _vmem_limit(weights, blocks, values)
    out = pl.pallas_call(
        functools.partial(_ffn_ln_kernel, len(to_cast)),
        grid=(steps,),
        in_specs=[
            pl.BlockSpec((tm, D_MODEL), lambda i: (i, 0)),
            _const_spec((D_MODEL, 2 * D_FF)),
            _const_spec((D_FF, D_MODEL)),
            _const_spec((1, D_MODEL)),
            _const_spec((1, D_MODEL)),
        ] + slabs,
        out_specs=[pl.BlockSpec((tm, D_MODEL), lambda i: (i, 0))] + slabs,
        out_shape=[jax.ShapeDtypeStruct((n, D_MODEL), F32)]
        + [jax.ShapeDtypeStruct(w.shape, BF16) for w in to_cast],
        compiler_params=pltpu.CompilerParams(
            dimension_semantics=("arbitrary",), vmem_limit_bytes=vmem),
        name="ffn_ln",
    )(x, w13, w2, g, b, *to_cast)
    return tuple(out) if to_cast else out[0]


def _lower_bound(lbp):
    mx = jnp.max(lbp, axis=0, keepdims=True)
    e = jnp.exp(lbp - mx)
    return e[0:1] / jnp.sum(e, axis=0, keepdims=True)


def _split3(x):
    hi = x.astype(BF16)
    r = x - hi.astype(F32)
    mid = r.astype(BF16)
    lo = (r - mid.astype(F32)).astype(BF16)
    return hi, mid, lo


def _block_cumsum(tri, g):
    w = g.shape[1]
    parts = jnp.concatenate(_split3(g), axis=1)
    s = _dot(tri, parts)
    return s[:, :w] + s[:, w:2 * w] + s[:, 2 * w:]


def _tri_matrix(group):
    r = lax.broadcasted_iota(jnp.int32, (BLK, BLK), 0)
    c = lax.broadcasted_iota(jnp.int32, (BLK, BLK), 1)
    keep = (c <= r) & ((r ^ c) < group)
    return jnp.where(keep, 1.0, 0.0).astype(BF16)


def _pair_ref_rows(b, m):
    t, w = b.shape
    if m >= SUBLANES:
        b3 = b.reshape(t // (2 * m), 2 * m, w)
        return jnp.broadcast_to(b3[:, m - 1:m, :], b3.shape).reshape(t, w)
    b3 = b.reshape(t // SUBLANES, SUBLANES, w)
    sub = lax.broadcasted_iota(jnp.int32, b3.shape, 1)

    def row(i):
        return jnp.broadcast_to(b3[:, i:i + 1, :], b3.shape)

    if m == 4:
        out = row(3)
    elif m == 2:
        out = jnp.where(sub < 4, row(1), row(5))
    else:
        out = jnp.where(sub < 4, jnp.where(sub < 2, row(0), row(2)),
                        jnp.where(sub < 6, row(4), row(6)))
    return out.reshape(t, w)


def _level_operands(q, k, e, m):
    t, w = q.shape
    if m >= SUBLANES:
        zero = jnp.zeros((m, w), F32)
        qs, ks = [], []
        for blk in range(t // m):
            rows = slice(blk * m, (blk + 1) * m)
            if blk % 2:
                qs.append(q[rows] * e[rows])
                ks.append(zero)
            else:
                qs.append(zero)
                ks.append(k[rows] * e[rows])
        return jnp.concatenate(qs, axis=0).astype(BF16), jnp.concatenate(ks, axis=0).astype(BF16)
    upper = (lax.broadcasted_iota(jnp.int32, q.shape, 0) & m) != 0
    return (jnp.where(upper, q * e, 0.0).astype(BF16), jnp.where(upper, 0.0, k * e).astype(BF16))


def _intra_steps(q, k, b, max_m, xor, out):
    a = jnp.where(xor == 0, jnp.sum(q * k, axis=-1, keepdims=True), 0.0)
    yield
    m = 1
    while m <= max_m:
        e = jnp.exp(-jnp.abs(b - _pair_ref_rows(b, m)))
        qs, ks = _level_operands(q, k, e, m)
        yield
        p = _dot_nt(qs, ks)
        yield
        a = jnp.where(xor >= m, p, a)
        m *= 2
    if 2 * max_m < q.shape[0]:
        a = jnp.where(xor < 2 * max_m, a, 0.0)
    out["a"] = a


def _intra_scores(q, k, b, max_m, xor):
    out = {}
    for _ in _intra_steps(q, k, b, max_m, xor, out):
        pass
    return out["a"]


def _hgrn_gates(proj, lb):
    q = jax.nn.silu(proj[:, OFF_HQ:OFF_HF])
    f = lb + (1.0 - lb) * jax.nn.sigmoid(proj[:, OFF_HF:OFF_HI])
    return q, jnp.log(f), 1.0 - f, proj[:, OFF_HI:OFF_HG]


def _head_queries(qa):
    low = lax.broadcasted_iota(jnp.int32, (qa.shape[0], LANES), 1) < HEAD_DIM
    out = []
    for pair in range(ATTN_HEADS // 2):
        piece = qa[:, LANES * pair:LANES * (pair + 1)] * ATTN_SCALE
        rolled = pltpu.roll(piece, HEAD_DIM, axis=1)
        for half in range(2):
            kv = (2 * pair + half) // GROUP
            src = piece if half == kv else rolled
            z = jnp.where(low, src, 0.0) if kv == 0 else jnp.where(low, 0.0, src)
            out.append(z.astype(BF16))
    return out


def _merge_heads(o):
    low = lax.broadcasted_iota(jnp.int32, o[0].shape, 1) < HEAD_DIM
    cols = []
    for pair in range(ATTN_HEADS // 2):
        kv = (2 * pair) // GROUP
        even, odd = o[2 * pair], o[2 * pair + 1]
        if kv == 0:
            odd = pltpu.roll(odd, HEAD_DIM, axis=1)
        else:
            even = pltpu.roll(even, HEAD_DIM, axis=1)
        cols.append(jnp.where(low, even, odd))
    return jnp.concatenate(cols, axis=1)


def _slope(head):
    return 2.0 ** (-8.0 * (head + 1) / ATTN_HEADS)


def _attend_steps(qz, k_all, v_all, sink_ref, limit, out):
    tq = qz[0].shape[0]
    nk = k_all.shape[0]
    s_all = _dot_nt(jnp.concatenate(qz, axis=0), k_all.astype(BF16))
    trow = lax.broadcasted_iota(jnp.int32, (tq, nk), 0)
    ccol = lax.broadcasted_iota(jnp.int32, (tq, nk), 1)
    dist = trow + WINDOW - ccol
    visible = dist.astype(jnp.uint32) < limit.astype(jnp.uint32)
    distf = dist.astype(F32)
    yield
    ps, dens = [], []
    for head in range(ATTN_HEADS):
        sink = sink_ref[head]
        s = s_all[tq * head:tq * (head + 1)]
        s = jnp.where(visible, s - _slope(head) * distf, -jnp.inf)
        mx = jnp.maximum(jnp.max(s, axis=-1, keepdims=True), sink)
        yield
        p = jnp.exp(s - mx)
        dens.append(jnp.sum(p, axis=-1, keepdims=True) + jnp.exp(sink - mx))
        ps.append(p.astype(BF16))
        yield
    o_all = _dot(jnp.concatenate(ps, axis=0), v_all.astype(BF16))
    yield
    out["oa"] = _merge_heads([o_all[tq * h:tq * (h + 1)] / dens[h] for h in range(ATTN_HEADS)])


def _attend_sample(qz, k_new, v_new, ck_ref, cv_ref, sink_ref):
    rows = qz[0].shape[0]
    k_new = k_new.astype(BF16)
    v_new = v_new.astype(BF16)
    sn_all = _dot_nt(jnp.concatenate(qz, axis=0), k_new)
    sc_seq = []
    for i in range(SEQ_PER_STEP):
        lhs = jnp.concatenate([z[TOK * i:TOK * (i + 1)] for z in qz], axis=0)
        sc_seq.append(_dot_nt(lhs, ck_ref[i].astype(BF16)))

    r = lax.broadcasted_iota(jnp.int32, (rows, rows), 0)
    c = lax.broadcasted_iota(jnp.int32, (rows, rows), 1)
    new_visible = ((r ^ c) < TOK) & (c <= r)
    new_dist = (r - c).astype(F32)
    tok = lax.broadcasted_iota(jnp.int32, (rows, WINDOW), 0) & (TOK - 1)
    key = lax.broadcasted_iota(jnp.int32, (rows, WINDOW), 1)
    old_visible = key > tok
    old_dist = (WINDOW + tok - key).astype(F32)

    pn, pc, dens = [], [], []
    for head in range(ATTN_HEADS):
        sink = sink_ref[head]
        slope = _slope(head)
        sn = sn_all[rows * head:rows * (head + 1)]
        sc = jnp.concatenate([s[TOK * head:TOK * (head + 1)] for s in sc_seq], axis=0)
        sn = jnp.where(new_visible, sn - slope * new_dist, -jnp.inf)
        sc = jnp.where(old_visible, sc - slope * old_dist, -jnp.inf)
        mx = jnp.maximum(jnp.maximum(jnp.max(sn, axis=-1, keepdims=True),
                                     jnp.max(sc, axis=-1, keepdims=True)), sink)
        en = jnp.exp(sn - mx)
        ec = jnp.exp(sc - mx)
        dens.append(jnp.sum(en, axis=-1, keepdims=True) + jnp.sum(ec, axis=-1, keepdims=True)
                    + jnp.exp(sink - mx))
        pn.append(en.astype(BF16))
        pc.append(ec.astype(BF16))

    on_all = _dot(jnp.concatenate(pn, axis=0), v_new)
    oc_seq = []
    for i in range(SEQ_PER_STEP):
        lhs = jnp.concatenate([p[TOK * i:TOK * (i + 1)] for p in pc], axis=0)
        oc_seq.append(_dot(lhs, cv_ref[i].astype(BF16)))
    o = []
    for head in range(ATTN_HEADS):
        oc = jnp.concatenate([s[TOK * head:TOK * (head + 1)] for s in oc_seq], axis=0)
        o.append((on_all[rows * head:rows * (head + 1)] + oc) / dens[head])
    return _merge_heads(o)


def _mix_out(x, oa, oh, proj, w_out_ref, ang_ref, hng_ref, g2_ref, b2_ref):
    oa = _rms_norm(oa, ang_ref[...])
    hng = hng_ref[...]
    gate = jax.nn.silu(proj[:, OFF_HG:])
    ohn = [_rms_norm(oh[h], hng[:, HGRN_DV * h:HGRN_DV * (h + 1)]) for h in range(HGRN_HEADS)]
    mix = jnp.concatenate([oa, jnp.concatenate(ohn, axis=1) * gate], axis=1).astype(BF16)
    y = _dot(mix, w_out_ref[...])
    return _layer_norm(ALPHA * x + y, g2_ref[...], b2_ref[...])


def _proj_stream(ctx, x, w_in_ref, may_start):
    while not may_start():
        yield
    xb = x.astype(BF16)
    chunks = []
    for c in range(D_IN // PROJ_CHUNK):
        chunks.append(_dot(xb, w_in_ref[:, PROJ_CHUNK * c:PROJ_CHUNK * (c + 1)]))
        yield
    ctx["proj"] = jnp.concatenate(chunks, axis=1)


def _prepare_stream(ctx, lb, tri):
    while "proj" not in ctx:
        yield
    proj = ctx["proj"]
    ctx["qz"] = _head_queries(proj[:, :ATTN_WIDTH])
    yield
    ctx["q"], logf, ctx["kin"], ctx["vin"] = _hgrn_gates(proj, lb)
    yield
    ctx["b"] = [_block_cumsum(tri, logf[BLK * blk:BLK * (blk + 1)]) for blk in range(PROMPT_BLOCKS)]
    ctx["ready"] = True


def _attention_stream(ctx, blk, prev_kv, limit, sink_ref):
    while "ready" not in ctx:
        yield
    rows = slice(BLK * blk, BLK * (blk + 1))
    k_prev, v_prev = prev_kv()
    k_all = jnp.concatenate([k_prev, ctx["proj"][rows, OFF_K:OFF_V]], axis=0)
    v_all = jnp.concatenate([v_prev, ctx["proj"][rows, OFF_V:OFF_HQ]], axis=0)
    out = {}
    yield from _attend_steps([z[rows] for z in ctx["qz"]], k_all, v_all, sink_ref, limit, out)
    ctx["oa"][blk] = out["oa"]


def _hgrn_stream(ctx, blk, h, state, turn, xor):
    while "ready" not in ctx:
        yield
    rows = slice(BLK * blk, BLK * (blk + 1))
    sl = slice(HGRN_DK * h, HGRN_DK * (h + 1))
    qh, kh, bh = ctx["q"][rows, sl], ctx["kin"][rows, sl], ctx["b"][blk][:, sl]
    vh = ctx["vin"][rows, sl].astype(BF16)
    scores = {}
    yield from _intra_steps(qh, kh, bh, BLK // 2, xor, scores)
    o_intra = _dot(scores["a"].astype(BF16), vh)
    yield
    b_last = bh[BLK - 1:BLK, :]
    qe = (qh * jnp.exp(bh)).astype(BF16)
    kd = (kh * jnp.exp(b_last - bh)).astype(BF16)
    yield
    while state["turn"][h] != turn:
        yield
    st = state["st"][h]
    o_inter = _dot_nt(qe, st.astype(BF16))
    update = _dot_tn(vh, kd)
    yield
    state["st"][h] = st * jnp.exp(b_last) + update
    state["turn"][h] = turn + 1
    ctx["oh"][blk][h] = o_intra + o_inter


def _mix_out_stream(ctx, x, y_store, w_out_ref, ang_ref, hng_ref, g2_ref, b2_ref):
    while any(o is None for o in ctx["oa"]) or any(o is None for blk in ctx["oh"] for o in blk):
        yield
    oa = jnp.concatenate(ctx["oa"], axis=0)
    oh = [jnp.concatenate([ctx["oh"][blk][h] for blk in range(PROMPT_BLOCKS)], axis=0)
          for h in range(HGRN_HEADS)]
    y_store(_mix_out(x, oa, oh, ctx["proj"], w_out_ref, ang_ref, hng_ref, g2_ref, b2_ref))


def _mixer_prompt_kernel(sink_ref, x_ref, w_in_ref, w_out_ref, lbp_ref, ang_ref, hng_ref, g2_ref,
                         b2_ref, y_ref, s_out_ref, k_out_ref, v_out_ref, st_ref, kv_ref):
    j = pl.program_id(1)

    @pl.when(j == 0)
    def _():
        st_ref[...] = jnp.zeros_like(st_ref)
        kv_ref[...] = jnp.zeros_like(kv_ref)

    lb = _lower_bound(lbp_ref[...])
    tri = _tri_matrix(BLK)
    xor = (lax.broadcasted_iota(jnp.int32, (BLK, BLK), 0)
           ^ lax.broadcasted_iota(jnp.int32, (BLK, BLK), 1))
    trow = lax.broadcasted_iota(jnp.int32, (BLK, 2 * WINDOW), 0)
    full = jnp.full(trow.shape, WINDOW, jnp.int32)

    state = {"st": [st_ref[h] for h in range(HGRN_HEADS)], "turn": [0] * HGRN_HEADS}
    ctxs = [{"oa": [None] * PROMPT_BLOCKS,
             "oh": [[None] * HGRN_HEADS for _ in range(PROMPT_BLOCKS)]} for _ in range(PROMPT_TILES)]

    def kv_block(tile, blk):
        rows = slice(BLK * blk, BLK * (blk + 1))
        proj = ctxs[tile]["proj"]
        return proj[rows, OFF_K:OFF_V], proj[rows, OFF_V:OFF_HQ]

    streams = []
    for tile in range(PROMPT_TILES):
        ctx = ctxs[tile]
        x = x_ref[tile]
        may_start = (lambda: True) if tile == 0 else (lambda prev=ctxs[tile - 1]: "ready" in prev)
        streams.append(_proj_stream(ctx, x, w_in_ref, may_start))
        streams.append(_prepare_stream(ctx, lb, tri))
        for blk in range(PROMPT_BLOCKS):
            if tile == 0 and blk == 0:
                prev_kv = lambda: (kv_ref[0], kv_ref[1])
                limit = jnp.where(j > 0, WINDOW, trow + 1)
            else:
                g = tile * PROMPT_BLOCKS + blk - 1
                prev_kv = lambda g=g: kv_block(g // PROMPT_BLOCKS, g % PROMPT_BLOCKS)
                limit = full
            streams.append(_attention_stream(ctx, blk, prev_kv, limit, sink_ref))
            for h in range(HGRN_HEADS):
                streams.append(_hgrn_stream(ctx, blk, h, state, tile * PROMPT_BLOCKS + blk, xor))

        def y_store(y, tile=tile):
            y_ref[tile] = y

        streams.append(_mix_out_stream(ctx, x, y_store, w_out_ref, ang_ref, hng_ref, g2_ref, b2_ref))
    _round_robin(streams)

    k_last, v_last = kv_block(PROMPT_TILES - 1, PROMPT_BLOCKS - 1)
    kv_ref[0] = k_last
    kv_ref[1] = v_last
    for h in range(HGRN_HEADS):
        st_ref[h] = state["st"][h]

    @pl.when(j == pl.num_programs(1) - 1)
    def _():
        k_out_ref[0] = k_last
        v_out_ref[0] = v_last
        for h in range(HGRN_HEADS):
            s_out_ref[0, h] = state["st"][h].T


MIXER_WEIGHT_BYTES = (D_MODEL * D_IN + D_MIX * D_MODEL) * BF16_BYTES
STATE_BYTES = HGRN_HEADS * HGRN_DK * HGRN_DV * F32_BYTES
WINDOW_BYTES = WINDOW * KV_WIDTH * F32_BYTES


def _mixer_value_bytes(rows, keys):
    per_row = ((D_IN + 4 * HGRN_KW + 2 * D_MODEL) * F32_BYTES
               + (ATTN_HEADS * LANES + D_MIX) * BF16_BYTES)
    scores = ATTN_HEADS * BLK * keys * (F32_BYTES + BF16_BYTES)
    return rows * per_row + (rows // BLK) * scores


def _mixer_prompt(x, sink, w_in, w_out, lbp, ang, hng, g2, b2):
    nb, seq, _ = x.shape
    tile_rows = PROMPT_BLOCKS * BLK
    steps_per_seq = seq // (PROMPT_TILES * tile_rows)
    n_tiles = nb * seq // tile_rows
    xt = x.reshape(n_tiles, tile_rows, D_MODEL)
    blocks = (2 * PROMPT_TILES * tile_rows * D_MODEL * F32_BYTES
              + STATE_BYTES + 2 * WINDOW_BYTES)
    vmem = _vmem_limit(MIXER_WEIGHT_BYTES + STATE_BYTES + 2 * WINDOW_BYTES, blocks,
                       _mixer_value_bytes(2 * tile_rows, 2 * WINDOW))
    tile_spec = pl.BlockSpec((PROMPT_TILES, tile_rows, D_MODEL),
                             lambda b, j: (b * steps_per_seq + j, 0, 0))
    y, s, k, v = pl.pallas_call(
        _mixer_prompt_kernel,
        grid=(nb, steps_per_seq),
        in_specs=[
            pl.BlockSpec(memory_space=pltpu.SMEM),
            tile_spec,
            _const_spec((D_MODEL, D_IN)),
            _const_spec((D_MIX, D_MODEL)),
            _const_spec((DEPTH + 1, HGRN_KW)),
            _const_spec((1, ATTN_WIDTH)),
            _const_spec((1, HGRN_VW)),
            _const_spec((1, D_MODEL)),
            _const_spec((1, D_MODEL)),
        ],
        out_specs=[
            tile_spec,
            pl.BlockSpec((1, HGRN_HEADS, HGRN_DK, HGRN_DV), lambda b, j: (b, 0, 0, 0)),
            pl.BlockSpec((1, WINDOW, KV_WIDTH), lambda b, j: (b, 0, 0)),
            pl.BlockSpec((1, WINDOW, KV_WIDTH), lambda b, j: (b, 0, 0)),
        ],
        out_shape=[
            jax.ShapeDtypeStruct((n_tiles, tile_rows, D_MODEL), F32),
            jax.ShapeDtypeStruct((nb, HGRN_HEADS, HGRN_DK, HGRN_DV), F32),
            jax.ShapeDtypeStruct((nb, WINDOW, KV_WIDTH), F32),
            jax.ShapeDtypeStruct((nb, WINDOW, KV_WIDTH), F32),
        ],
        scratch_shapes=[
            pltpu.VMEM((HGRN_HEADS, HGRN_DV, HGRN_DK), F32),
            pltpu.VMEM((2, WINDOW, KV_WIDTH), F32),
        ],
        compiler_params=pltpu.CompilerParams(
            dimension_semantics=("arbitrary", "arbitrary"), vmem_limit_bytes=vmem),
        name="mixer_prompt",
    )(sink, xt, w_in, w_out, lbp, ang, hng, g2, b2)
    return y.reshape(nb, seq, D_MODEL), s, k, v


def _mixer_sample_kernel(sink_ref, x_ref, s0_ref, ck_ref, cv_ref, w_in_ref, w_out_ref, lbp_ref,
                         ang_ref, hng_ref, g2_ref, b2_ref, y_ref, s_out_ref, k_out_ref, v_out_ref):
    x = x_ref[...]
    proj = _dot(x.astype(BF16), w_in_ref[...])
    k_new = proj[:, OFF_K:OFF_V]
    v_new = proj[:, OFF_V:OFF_HQ]

    oa = _attend_sample(_head_queries(proj[:, :ATTN_WIDTH]), k_new, v_new, ck_ref, cv_ref, sink_ref)
    for i in range(SEQ_PER_STEP):
        rows = slice(TOK * i, TOK * (i + 1))
        k_out_ref[i] = jnp.concatenate([ck_ref[i, TOK:, :], k_new[rows]], axis=0)
        v_out_ref[i] = jnp.concatenate([cv_ref[i, TOK:, :], v_new[rows]], axis=0)

    q, logf, kin, vin = _hgrn_gates(proj, _lower_bound(lbp_ref[...]))
    b = _block_cumsum(_tri_matrix(TOK), logf)
    xor = (lax.broadcasted_iota(jnp.int32, (BLK, BLK), 0)
           ^ lax.broadcasted_iota(jnp.int32, (BLK, BLK), 1))
    lane = lax.broadcasted_iota(jnp.int32, (HGRN_DK, BLK), 1)
    oh = []
    for h in range(HGRN_HEADS):
        sl = slice(HGRN_DK * h, HGRN_DK * (h + 1))
        qh, kh, bh = q[:, sl], kin[:, sl], b[:, sl]
        vh = vin[:, sl].astype(BF16)
        a = _intra_scores(qh, kh, bh, TOK // 2, xor)
        o_intra = _dot(a.astype(BF16), vh)
        qe = (qh * jnp.exp(bh)).astype(BF16)
        b3 = bh.reshape(SEQ_PER_STEP, TOK, HGRN_DK)
        b_last = jnp.broadcast_to(b3[:, TOK - 1:TOK, :], b3.shape).reshape(BLK, HGRN_DK)
        kd_t = (kh * jnp.exp(b_last - bh)).T
        dec_t = jnp.exp(b_last).T
        o_rows = []
        for i in range(SEQ_PER_STEP):
            s0 = s0_ref[i, h]
            o_rows.append(_dot(qe[TOK * i:TOK * (i + 1)], s0.astype(BF16)))
            kd_i = jnp.where((lane >= TOK * i) & (lane < TOK * (i + 1)), kd_t, 0.0).astype(BF16)
            s_out_ref[i, h] = s0 * dec_t[:, TOK * i:TOK * i + 1] + _dot(kd_i, vh)
        oh.append(o_intra + jnp.concatenate(o_rows, axis=0))

    y_ref[...] = _mix_out(x, oa, oh, proj, w_out_ref, ang_ref, hng_ref, g2_ref, b2_ref)


def _mixer_sample(x, s0, ck, cv, sink, w_in, w_out, lbp, ang, hng, g2, b2):
    n = x.shape[0]
    nseq = s0.shape[0]
    state_spec = pl.BlockSpec((SEQ_PER_STEP, HGRN_HEADS, HGRN_DK, HGRN_DV), lambda i: (i, 0, 0, 0))
    cache_spec = pl.BlockSpec((SEQ_PER_STEP, WINDOW, KV_WIDTH), lambda i: (i, 0, 0))
    blocks = 2 * BLK * D_MODEL * F32_BYTES + 2 * SEQ_PER_STEP * (STATE_BYTES + 2 * WINDOW_BYTES)
    values = (_mixer_value_bytes(BLK, WINDOW + BLK)
              + HGRN_HEADS * 2 * HGRN_DK * BLK * F32_BYTES)
    vmem = _vmem_limit(MIXER_WEIGHT_BYTES, blocks, values)
    return pl.pallas_call(
        _mixer_sample_kernel,
        grid=(n // BLK,),
        in_specs=[
            pl.BlockSpec(memory_space=pltpu.SMEM),
            pl.BlockSpec((BLK, D_MODEL), lambda i: (i, 0)),
            state_spec, cache_spec, cache_spec,
            _const_spec((D_MODEL, D_IN)),
            _const_spec((D_MIX, D_MODEL)),
            _const_spec((DEPTH + 1, HGRN_KW)),
            _const_spec((1, ATTN_WIDTH)),
            _const_spec((1, HGRN_VW)),
            _const_spec((1, D_MODEL)),
            _const_spec((1, D_MODEL)),
        ],
        out_specs=[
            pl.BlockSpec((BLK, D_MODEL), lambda i: (i, 0)),
            state_spec, cache_spec, cache_spec,
        ],
        out_shape=[
            jax.ShapeDtypeStruct((n, D_MODEL), F32),
            jax.ShapeDtypeStruct((nseq, HGRN_HEADS, HGRN_DK, HGRN_DV), F32),
            jax.ShapeDtypeStruct((nseq, WINDOW, KV_WIDTH), F32),
            jax.ShapeDtypeStruct((nseq, WINDOW, KV_WIDTH), F32),
        ],
        compiler_params=pltpu.CompilerParams(
            dimension_semantics=("arbitrary",), vmem_limit_bytes=vmem),
        name="mixer_sample",
    )(sink, x, s0, ck, cv, w_in, w_out, lbp, ang, hng, g2, b2)


def kernel(x_prompt, x_sample, state_hgrn, cache_win_k, cache_win_v, ln1_g, ln1_b, ffn1_w13,
           ffn1_w2, w_in, lb_param, attn_sink, attn_norm_g, hgrn_norm_g, w_out, ln2_g, ln2_b,
           ffn2_w13, ffn2_w2, ln3_g, ln3_b):
    nb, seq, d = x_prompt.shape
    ns, tn, _ = x_sample.shape
    assert state_hgrn.shape[0] == DEPTH == 1 and d == D_MODEL
    assert seq % (PROMPT_TILES * PROMPT_BLOCKS * BLK) == 0 and tn == TOK and (ns * tn) % BLK == 0
    assert cache_win_k.shape[2] == WINDOW

    w13_1 = ffn1_w13[0].astype(BF16)
    w2_1 = ffn1_w2[0].astype(BF16)
    sink = attn_sink[0]
    lbp = lb_param.astype(F32)

    xp = x_prompt.reshape(nb * seq, d)
    xs = x_sample.reshape(ns * tn, d)

    xp, w13_2, w2_2, w_in_b, w_out_b = _ffn_ln(
        xp, w13_1, w2_1, ln1_g, ln1_b, to_cast=(ffn2_w13[0], ffn2_w2[0], w_in[0], w_out[0]))
    xs = _ffn_ln(xs, w13_1, w2_1, ln1_g, ln1_b)
    mix_params = (sink, w_in_b, w_out_b, lbp, attn_norm_g, hgrn_norm_g, ln2_g, ln2_b)

    xp, sp, kp, vp = _mixer_prompt(xp.reshape(nb, seq, d), *mix_params)
    xs, ss, ks, vs = _mixer_sample(
        xs, state_hgrn[0], cache_win_k[0].reshape(ns, WINDOW, KV_WIDTH),
        cache_win_v[0].reshape(ns, WINDOW, KV_WIDTH), *mix_params)

    xp = _ffn_ln(xp.reshape(nb * seq, d), w13_2, w2_2, ln3_g, ln3_b)
    xs = _ffn_ln(xs, w13_2, w2_2, ln3_g, ln3_b)

    cache_tail = (WINDOW, ATTN_KV_HEADS, HEAD_DIM)
    return (xp.reshape(nb, seq, d), xs.reshape(ns, tn, d),
            sp[None], kp.reshape((DEPTH, nb) + cache_tail), vp.reshape((DEPTH, nb) + cache_tail),
            ss[None], ks.reshape((DEPTH, ns) + cache_tail), vs.reshape((DEPTH, ns) + cache_tail))
```

```python
import functools

import jax
import jax.numpy as jnp
from jax import lax
from jax.experimental import pallas as pl
from jax.experimental.pallas import tpu as pltpu

F32 = jnp.float32
BF16 = jnp.bfloat16

D_MODEL = 1024
DEPTH = 1
ATTN_HEADS = 8
ATTN_KV_HEADS = 2
GROUP = ATTN_HEADS // ATTN_KV_HEADS
HEAD_DIM = 64
ATTN_WIDTH = ATTN_HEADS * HEAD_DIM
KV_WIDTH = ATTN_KV_HEADS * HEAD_DIM
WINDOW = 128
ATTN_SCALE = HEAD_DIM ** -0.5
HGRN_HEADS = 4
HGRN_DK = 128
HGRN_DV = 128
HGRN_KW = HGRN_HEADS * HGRN_DK
HGRN_VW = HGRN_HEADS * HGRN_DV
D_MIX = ATTN_WIDTH + HGRN_VW
D_IN = ATTN_WIDTH + 2 * KV_WIDTH + 2 * HGRN_KW + 2 * HGRN_VW
D_FF = 2816
ALPHA = (2.0 * DEPTH) ** 0.25
LN_EPS = 1e-5
RMS_EPS = 1e-6

OFF_K = ATTN_WIDTH
OFF_V = OFF_K + KV_WIDTH
OFF_HQ = OFF_V + KV_WIDTH
OFF_HF = OFF_HQ + HGRN_KW
OFF_HI = OFF_HF + HGRN_KW
OFF_HG = OFF_HI + HGRN_VW

LANES = 128
SUBLANES = 8
BLK = 128
PROMPT_BLOCKS = 2
PROMPT_TILES = 2
PROJ_CHUNK = 256
FFN_TILE = 256
FFN_TILES = 4
FFN_CHUNK = 256
TOK = SUBLANES
SEQ_PER_STEP = BLK // TOK
V7X_VMEM_BYTES = 64 * 1024 * 1024
F32_BYTES = 4
BF16_BYTES = 2

NT_DIMS = (((1,), (1,)), ((), ()))
TN_DIMS = (((0,), (0,)), ((), ()))


def _vmem_limit(resident_bytes, streamed_bytes, value_bytes):
    need = resident_bytes + 2 * streamed_bytes + value_bytes
    assert need <= V7X_VMEM_BYTES, need
    return need


_DONE = object()


def _round_robin(streams):
    live = list(streams)
    rounds = 0
    while live:
        live = [g for g in live if next(g, _DONE) is not _DONE]
        rounds += 1
        assert rounds < 100000, "a stream waits for something no other stream produces"


def _dot(a, b):
    return jnp.dot(a, b, preferred_element_type=F32)


def _dot_nt(a, b):
    return lax.dot_general(a, b, NT_DIMS, preferred_element_type=F32)


def _dot_tn(a, b):
    return lax.dot_general(a, b, TN_DIMS, preferred_element_type=F32)


def _layer_norm(r, g, b):
    mu = jnp.mean(r, axis=-1, keepdims=True)
    c = r - mu
    var = jnp.mean(c * c, axis=-1, keepdims=True)
    return c * lax.rsqrt(var + LN_EPS) * g + b


def _rms_norm(x, g):
    return x * lax.rsqrt(jnp.mean(x * x, axis=-1, keepdims=True) + RMS_EPS) * g


def _ffn_steps(x, w13_ref, w2_ref, g, b, store, on_hidden=lambda: None):
    xb = x.astype(BF16)
    acts = []
    for c in range(D_FF // FFN_CHUNK):
        gate = _dot(xb, w13_ref[:, FFN_CHUNK * c:FFN_CHUNK * (c + 1)])
        up = _dot(xb, w13_ref[:, D_FF + FFN_CHUNK * c:D_FF + FFN_CHUNK * (c + 1)])
        acts.append((jax.nn.silu(gate) * up).astype(BF16))
        yield
    act = jnp.concatenate(acts, axis=1)
    on_hidden()
    ys = []
    for c in range(D_MODEL // FFN_CHUNK):
        ys.append(_dot(act, w2_ref[:, FFN_CHUNK * c:FFN_CHUNK * (c + 1)]))
        yield
    store(_layer_norm(ALPHA * x + 0.5 * jnp.concatenate(ys, axis=1), g, b))


def _ffn_stream(x_ref, o_ref, tile, w13_ref, w2_ref, g, b, flags, may_start):
    while not may_start():
        yield
    rows = slice(FFN_TILE * tile, FFN_TILE * (tile + 1))

    def store(y):
        o_ref[rows, :] = y

    def on_hidden():
        flags[tile] = True

    yield from _ffn_steps(x_ref[rows, :], w13_ref, w2_ref, g, b, store, on_hidden)


def _cast_stream(src_refs, dst_refs):
    for src, dst in zip(src_refs, dst_refs):
        dst[...] = src[...].astype(BF16)
        yield


def _ffn_ln_kernel(n_cast, x_ref, w13_ref, w2_ref, g_ref, b_ref, *refs):
    cast_src, o_ref, cast_dst = refs[:n_cast], refs[n_cast], refs[n_cast + 1:]
    n_tiles = x_ref.shape[0] // FFN_TILE
    g, b = g_ref[...], b_ref[...]
    flags = [False] * n_tiles
    streams = [_ffn_stream(x_ref, o_ref, t, w13_ref, w2_ref, g, b, flags,
                           (lambda: True) if t == 0 else (lambda t=t: flags[t - 1]))
               for t in range(n_tiles)]
    _round_robin(streams + [_cast_stream(cast_src, cast_dst)])


def _const_spec(shape):
    return pl.BlockSpec(shape, lambda *_: (0,) * len(shape), pipeline_mode=pl.Buffered(1))


def _slab_spec(shape, steps):
    rows, cols = shape
    count = max(c for c in range(1, steps + 1)
                if rows % c == 0 and (rows // c) % (2 * SUBLANES) == 0)
    return pl.BlockSpec((rows // count, cols), lambda i: (jnp.minimum(i, count - 1), 0))


def _ffn_ln(x, w13, w2, g, b, to_cast=()):
    n = x.shape[0]
    tm = FFN_TILE * FFN_TILES
    assert n % tm == 0
    steps = n // tm
    slabs = [_slab_spec(w.shape, steps) for w in to_cast]
    weights = (D_MODEL * 2 * D_FF + D_FF * D_MODEL) * BF16_BYTES
    blocks = (2 * tm * D_MODEL * F32_BYTES
              + sum(s.block_shape[0] * s.block_shape[1] for s in slabs) * (F32_BYTES + BF16_BYTES))
    values = 2 * FFN_TILE * ((D_MODEL + D_FF) * BF16_BYTES
                             + (2 * FFN_CHUNK + 2 * D_MODEL) * F32_BYTES)
    vmem = _vmem_limit(weights, blocks, values)
    out = pl.pallas_call(
        functools.partial(_ffn_ln_kernel, len(to_cast)),
        grid=(steps,),
        in_specs=[
            pl.BlockSpec((tm, D_MODEL), lambda i: (i, 0)),
            _const_spec((D_MODEL, 2 * D_FF)),
            _const_spec((D_FF, D_MODEL)),
            _const_spec((1, D_MODEL)),
            _const_spec((1, D_MODEL)),
        ] + slabs,
        out_specs=[pl.BlockSpec((tm, D_MODEL), lambda i: (i, 0))] + slabs,
        out_shape=[jax.ShapeDtypeStruct((n, D_MODEL), F32)]
        + [jax.ShapeDtypeStruct(w.shape, BF16) for w in to_cast],
        compiler_params=pltpu.CompilerParams(
            dimension_semantics=("arbitrary",), vmem_limit_bytes=vmem),
        name="ffn_ln",
    )(x, w13, w2, g, b, *to_cast)
    return tuple(out) if to_cast else out[0]


def _lower_bound(lbp):
    mx = jnp.max(lbp, axis=0, keepdims=True)
    e = jnp.exp(lbp - mx)
    return e[0:1] / jnp.sum(e, axis=0, keepdims=True)


def _split3(x):
    hi = x.astype(BF16)
    r = x - hi.astype(F32)
    mid = r.astype(BF16)
    lo = (r - mid.astype(F32)).astype(BF16)
    return hi, mid, lo


def _block_cumsum(tri, g):
    w = g.shape[1]
    parts = jnp.concatenate(_split3(g), axis=1)
    s = _dot(tri, parts)
    return s[:, :w] + s[:, w:2 * w] + s[:, 2 * w:]


def _tri_matrix(group):
    r = lax.broadcasted_iota(jnp.int32, (BLK, BLK), 0)
    c = lax.broadcasted_iota(jnp.int32, (BLK, BLK), 1)
    keep = (c <= r) & ((r ^ c) < group)
    return jnp.where(keep, 1.0, 0.0).astype(BF16)


def _pair_ref_rows(b, m):
    t, w = b.shape
    if m >= SUBLANES:
        b3 = b.reshape(t // (2 * m), 2 * m, w)
        return jnp.broadcast_to(b3[:, m - 1:m, :], b3.shape).reshape(t, w)
    b3 = b.reshape(t // SUBLANES, SUBLANES, w)
    sub = lax.broadcasted_iota(jnp.int32, b3.shape, 1)

    def row(i):
        return jnp.broadcast_to(b3[:, i:i + 1, :], b3.shape)

    if m == 4:
        out = row(3)
    elif m == 2:
        out = jnp.where(sub < 4, row(1), row(5))
    else:
        out = jnp.where(sub < 4, jnp.where(sub < 2, row(0), row(2)),
                        jnp.where(sub < 6, row(4), row(6)))
    return out.reshape(t, w)


def _level_operands(q, k, b, m):
    t, w = q.shape
    ref = _pair_ref_rows(b, m)
    if m >= SUBLANES:
        zero = jnp.zeros((m, w), F32)
        qs, ks = [], []
        for blk in range(t // m):
            rows = slice(blk * m, (blk + 1) * m)
            if blk % 2:
                qs.append(q[rows] * jnp.exp(b[rows] - ref[rows]))
                ks.append(zero)
            else:
                qs.append(zero)
                ks.append(k[rows] * jnp.exp(ref[rows] - b[rows]))
        return jnp.concatenate(qs, axis=0).astype(BF16), jnp.concatenate(ks, axis=0).astype(BF16)
    e = jnp.exp(-jnp.abs(b - ref))
    upper = (lax.broadcasted_iota(jnp.int32, q.shape, 0) & m) != 0
    return (jnp.where(upper, q * e, 0.0).astype(BF16), jnp.where(upper, 0.0, k * e).astype(BF16))


def _intra_steps(q, k, b, max_m, xor, out):
    a = jnp.where(xor == 0, jnp.sum(q * k, axis=-1, keepdims=True), 0.0)
    yield
    m = 1
    while m <= max_m:
        qs, ks = _level_operands(q, k, b, m)
        yield
        p = _dot_nt(qs, ks)
        yield
        a = jnp.where(xor >= m, p, a)
        m *= 2
    if 2 * max_m < q.shape[0]:
        a = jnp.where(xor < 2 * max_m, a, 0.0)
    out["a"] = a


def _intra_scores(q, k, b, max_m, xor):
    out = {}
    for _ in _intra_steps(q, k, b, max_m, xor, out):
        pass
    return out["a"]


def _hgrn_gates(proj, lb):
    q = jax.nn.silu(proj[:, OFF_HQ:OFF_HF])
    f = lb + (1.0 - lb) * jax.nn.sigmoid(proj[:, OFF_HF:OFF_HI])
    return q, jnp.log(f), 1.0 - f, proj[:, OFF_HI:OFF_HG]


def _head_queries(qa):
    low = lax.broadcasted_iota(jnp.int32, (qa.shape[0], LANES), 1) < HEAD_DIM
    out = []
    for pair in range(ATTN_HEADS // 2):
        piece = qa[:, LANES * pair:LANES * (pair + 1)] * ATTN_SCALE
        rolled = pltpu.roll(piece, HEAD_DIM, axis=1)
        for half in range(2):
            kv = (2 * pair + half) // GROUP
            src = piece if half == kv else rolled
            z = jnp.where(low, src, 0.0) if kv == 0 else jnp.where(low, 0.0, src)
            out.append(z.astype(BF16))
    return out


def _merge_heads(o):
    low = lax.broadcasted_iota(jnp.int32, o[0].shape, 1) < HEAD_DIM
    cols = []
    for pair in range(ATTN_HEADS // 2):
        kv = (2 * pair) // GROUP
        even, odd = o[2 * pair], o[2 * pair + 1]
        if kv == 0:
            odd = pltpu.roll(odd, HEAD_DIM, axis=1)
        else:
            even = pltpu.roll(even, HEAD_DIM, axis=1)
        cols.append(jnp.where(low, even, odd))
    return jnp.concatenate(cols, axis=1)


def _slope(head):
    return 2.0 ** (-8.0 * (head + 1) / ATTN_HEADS)


def _alibi_bias(limit):
    trow = lax.broadcasted_iota(jnp.int32, limit.shape, 0)
    ccol = lax.broadcasted_iota(jnp.int32, limit.shape, 1)
    dist = trow + WINDOW - ccol
    visible = dist.astype(jnp.uint32) < limit.astype(jnp.uint32)
    distf = dist.astype(F32)
    return [jnp.where(visible, -_slope(head) * distf, -jnp.inf) for head in range(ATTN_HEADS)]


def _attend_steps(qz, k_all, v_all, sink_ref, bias, out):
    tq = qz[0].shape[0]
    s_all = _dot_nt(jnp.concatenate(qz, axis=0), k_all.astype(BF16))
    yield
    ps, dens = [], []
    for head in range(ATTN_HEADS):
        sink = sink_ref[head]
        s = s_all[tq * head:tq * (head + 1)] + bias[head]
        mx = jnp.maximum(jnp.max(s, axis=-1, keepdims=True), sink)
        yield
        p = jnp.exp(s - mx)
        dens.append(jnp.sum(p, axis=-1, keepdims=True) + jnp.exp(sink - mx))
        ps.append(p.astype(BF16))
        yield
    o_all = _dot(jnp.concatenate(ps, axis=0), v_all.astype(BF16))
    yield
    out["oa"] = _merge_heads([o_all[tq * h:tq * (h + 1)] / dens[h] for h in range(ATTN_HEADS)])


def _attend_sample(qz, k_new, v_new, ck_ref, cv_ref, sink_ref):
    rows = qz[0].shape[0]
    k_new = k_new.astype(BF16)
    v_new = v_new.astype(BF16)
    sn_all = _dot_nt(jnp.concatenate(qz, axis=0), k_new)
    sc_seq = []
    for i in range(SEQ_PER_STEP):
        lhs = jnp.concatenate([z[TOK * i:TOK * (i + 1)] for z in qz], axis=0)
        sc_seq.append(_dot_nt(lhs, ck_ref[i].astype(BF16)))

    r = lax.broadcasted_iota(jnp.int32, (rows, rows), 0)
    c = lax.broadcasted_iota(jnp.int32, (rows, rows), 1)
    new_visible = ((r ^ c) < TOK) & (c <= r)
    new_dist = (r - c).astype(F32)
    tok = lax.broadcasted_iota(jnp.int32, (rows, WINDOW), 0) & (TOK - 1)
    key = lax.broadcasted_iota(jnp.int32, (rows, WINDOW), 1)
    old_visible = key > tok
    old_dist = (WINDOW + tok - key).astype(F32)

    pn, pc, dens = [], [], []
    for head in range(ATTN_HEADS):
        sink = sink_ref[head]
        slope = _slope(head)
        sn = sn_all[rows * head:rows * (head + 1)]
        sc = jnp.concatenate([s[TOK * head:TOK * (head + 1)] for s in sc_seq], axis=0)
        sn = jnp.where(new_visible, sn - slope * new_dist, -jnp.inf)
        sc = jnp.where(old_visible, sc - slope * old_dist, -jnp.inf)
        mx = jnp.maximum(jnp.maximum(jnp.max(sn, axis=-1, keepdims=True),
                                     jnp.max(sc, axis=-1, keepdims=True)), sink)
        en = jnp.exp(sn - mx)
        ec = jnp.exp(sc - mx)
        dens.append(jnp.sum(en, axis=-1, keepdims=True) + jnp.sum(ec, axis=-1, keepdims=True)
                    + jnp.exp(sink - mx))
        pn.append(en.astype(BF16))
        pc.append(ec.astype(BF16))

    on_all = _dot(jnp.concatenate(pn, axis=0), v_new)
    oc_seq = []
    for i in range(SEQ_PER_STEP):
        lhs = jnp.concatenate([p[TOK * i:TOK * (i + 1)] for p in pc], axis=0)
        oc_seq.append(_dot(lhs, cv_ref[i].astype(BF16)))
    o = []
    for head in range(ATTN_HEADS):
        oc = jnp.concatenate([s[TOK * head:TOK * (head + 1)] for s in oc_seq], axis=0)
        o.append((on_all[rows * head:rows * (head + 1)] + oc) / dens[head])
    return _merge_heads(o)


def _mix_out(x, oa, oh, proj, w_out_ref, ang_ref, hng_ref, g2_ref, b2_ref):
    oa = _rms_norm(oa, ang_ref[...])
    hng = hng_ref[...]
    gate = jax.nn.silu(proj[:, OFF_HG:])
    ohn = [_rms_norm(oh[h], hng[:, HGRN_DV * h:HGRN_DV * (h + 1)]) for h in range(HGRN_HEADS)]
    mix = jnp.concatenate([oa, jnp.concatenate(ohn, axis=1) * gate], axis=1).astype(BF16)
    y = _dot(mix, w_out_ref[...])
    return _layer_norm(ALPHA * x + y, g2_ref[...], b2_ref[...])


def _proj_stream(ctx, x, w_in_ref, may_start):
    while not may_start():
        yield
    xb = x.astype(BF16)
    chunks = []
    for c in range(D_IN // PROJ_CHUNK):
        chunks.append(_dot(xb, w_in_ref[:, PROJ_CHUNK * c:PROJ_CHUNK * (c + 1)]))
        yield
    ctx["proj"] = jnp.concatenate(chunks, axis=1)


def _prepare_stream(ctx, lb, tri):
    while "proj" not in ctx:
        yield
    proj = ctx["proj"]
    ctx["qz"] = _head_queries(proj[:, :ATTN_WIDTH])
    yield
    ctx["q"], logf, ctx["kin"], ctx["vin"] = _hgrn_gates(proj, lb)
    yield
    ctx["b"] = [_block_cumsum(tri, logf[BLK * blk:BLK * (blk + 1)]) for blk in range(PROMPT_BLOCKS)]
    ctx["ready"] = True


def _attention_stream(ctx, blk, prev_kv, bias, sink_ref):
    while "ready" not in ctx:
        yield
    rows = slice(BLK * blk, BLK * (blk + 1))
    k_prev, v_prev = prev_kv()
    k_all = jnp.concatenate([k_prev, ctx["proj"][rows, OFF_K:OFF_V]], axis=0)
    v_all = jnp.concatenate([v_prev, ctx["proj"][rows, OFF_V:OFF_HQ]], axis=0)
    out = {}
    yield from _attend_steps([z[rows] for z in ctx["qz"]], k_all, v_all, sink_ref, bias, out)
    ctx["oa"][blk] = out["oa"]


def _hgrn_stream(ctx, blk, h, state, turn, xor):
    while "ready" not in ctx:
        yield
    rows = slice(BLK * blk, BLK * (blk + 1))
    sl = slice(HGRN_DK * h, HGRN_DK * (h + 1))
    qh, kh, bh = ctx["q"][rows, sl], ctx["kin"][rows, sl], ctx["b"][blk][:, sl]
    vh = ctx["vin"][rows, sl].astype(BF16)
    scores = {}
    yield from _intra_steps(qh, kh, bh, BLK // 2, xor, scores)
    o_intra = _dot(scores["a"].astype(BF16), vh)
    yield
    b_last = bh[BLK - 1:BLK, :]
    qe = (qh * jnp.exp(bh)).astype(BF16)
    kd = (kh * jnp.exp(b_last - bh)).astype(BF16)
    yield
    while state["turn"][h] != turn:
        yield
    st = state["st"][h]
    o_inter = _dot_nt(qe, st.astype(BF16))
    update = _dot_tn(vh, kd)
    yield
    state["st"][h] = st * jnp.exp(b_last) + update
    state["turn"][h] = turn + 1
    ctx["oh"][blk][h] = o_intra + o_inter


def _mix_out_stream(ctx, x, y_store, w_out_ref, ang_ref, hng_ref, g2_ref, b2_ref):
    while any(o is None for o in ctx["oa"]) or any(o is None for blk in ctx["oh"] for o in blk):
        yield
    oa = jnp.concatenate(ctx["oa"], axis=0)
    oh = [jnp.concatenate([ctx["oh"][blk][h] for blk in range(PROMPT_BLOCKS)], axis=0)
          for h in range(HGRN_HEADS)]
    y_store(_mix_out(x, oa, oh, ctx["proj"], w_out_ref, ang_ref, hng_ref, g2_ref, b2_ref))


def _mixer_prompt_kernel(sink_ref, x_ref, w_in_ref, w_out_ref, lbp_ref, ang_ref, hng_ref, g2_ref,
                         b2_ref, y_ref, s_out_ref, k_out_ref, v_out_ref, st_ref, kv_ref):
    j = pl.program_id(1)

    @pl.when(j == 0)
    def _():
        st_ref[...] = jnp.zeros_like(st_ref)
        kv_ref[...] = jnp.zeros_like(kv_ref)

    lb = _lower_bound(lbp_ref[...])
    tri = _tri_matrix(BLK)
    xor = (lax.broadcasted_iota(jnp.int32, (BLK, BLK), 0)
           ^ lax.broadcasted_iota(jnp.int32, (BLK, BLK), 1))
    trow = lax.broadcasted_iota(jnp.int32, (BLK, 2 * WINDOW), 0)
    full_bias = _alibi_bias(jnp.full(trow.shape, WINDOW, jnp.int32))

    state = {"st": [st_ref[h] for h in range(HGRN_HEADS)], "turn": [0] * HGRN_HEADS}
    ctxs = [{"oa": [None] * PROMPT_BLOCKS,
             "oh": [[None] * HGRN_HEADS for _ in range(PROMPT_BLOCKS)]} for _ in range(PROMPT_TILES)]

    def kv_block(tile, blk):
        rows = slice(BLK * blk, BLK * (blk + 1))
        proj = ctxs[tile]["proj"]
        return proj[rows, OFF_K:OFF_V], proj[rows, OFF_V:OFF_HQ]

    streams = []
    for tile in range(PROMPT_TILES):
        ctx = ctxs[tile]
        x = x_ref[tile]
        may_start = (lambda: True) if tile == 0 else (lambda prev=ctxs[tile - 1]: "ready" in prev)
        streams.append(_proj_stream(ctx, x, w_in_ref, may_start))
        streams.append(_prepare_stream(ctx, lb, tri))
        for blk in range(PROMPT_BLOCKS):
            if tile == 0 and blk == 0:
                prev_kv = lambda: (kv_ref[0], kv_ref[1])
                bias = _alibi_bias(jnp.where(j > 0, WINDOW, trow + 1))
            else:
                g = tile * PROMPT_BLOCKS + blk - 1
                prev_kv = lambda g=g: kv_block(g // PROMPT_BLOCKS, g % PROMPT_BLOCKS)
                bias = full_bias
            streams.append(_attention_stream(ctx, blk, prev_kv, bias, sink_ref))
            for h in range(HGRN_HEADS):
                streams.append(_hgrn_stream(ctx, blk, h, state, tile * PROMPT_BLOCKS + blk, xor))

        def y_store(y, tile=tile):
            y_ref[tile] = y

        streams.append(_mix_out_stream(ctx, x, y_store, w_out_ref, ang_ref, hng_ref, g2_ref, b2_ref))
    _round_robin(streams)

    k_last, v_last = kv_block(PROMPT_TILES - 1, PROMPT_BLOCKS - 1)
    kv_ref[0] = k_last
    kv_ref[1] = v_last
    for h in range(HGRN_HEADS):
        st_ref[h] = state["st"][h]

    @pl.when(j == pl.num_programs(1) - 1)
    def _():
        k_out_ref[0] = k_last
        v_out_ref[0] = v_last
        for h in range(HGRN_HEADS):
            s_out_ref[0, h] = state["st"][h].T


MIXER_WEIGHT_BYTES = (D_MODEL * D_IN + D_MIX * D_MODEL) * BF16_BYTES
STATE_BYTES = HGRN_HEADS * HGRN_DK * HGRN_DV * F32_BYTES
WINDOW_BYTES = WINDOW * KV_WIDTH * F32_BYTES


def _mixer_value_bytes(rows, keys):
    per_row = ((D_IN + 4 * HGRN_KW + 2 * D_MODEL) * F32_BYTES
               + (ATTN_HEADS * LANES + D_MIX) * BF16_BYTES)
    scores = ATTN_HEADS * BLK * keys * (F32_BYTES + BF16_BYTES)
    return rows * per_row + (rows // BLK) * scores


def _mixer_prompt(x, sink, w_in, w_out, lbp, ang, hng, g2, b2):
    nb, seq, _ = x.shape
    tile_rows = PROMPT_BLOCKS * BLK
    steps_per_seq = seq // (PROMPT_TILES * tile_rows)
    n_tiles = nb * seq // tile_rows
    xt = x.reshape(n_tiles, tile_rows, D_MODEL)
    blocks = (2 * PROMPT_TILES * tile_rows * D_MODEL * F32_BYTES
              + STATE_BYTES + 2 * WINDOW_BYTES)
    vmem = _vmem_limit(MIXER_WEIGHT_BYTES + STATE_BYTES + 2 * WINDOW_BYTES, blocks,
                       _mixer_value_bytes(2 * tile_rows, 2 * WINDOW))
    tile_spec = pl.BlockSpec((PROMPT_TILES, tile_rows, D_MODEL),
                             lambda b, j: (b * steps_per_seq + j, 0, 0))
    y, s, k, v = pl.pallas_call(
        _mixer_prompt_kernel,
        grid=(nb, steps_per_seq),
        in_specs=[
            pl.BlockSpec(memory_space=pltpu.SMEM),
            tile_spec,
            _const_spec((D_MODEL, D_IN)),
            _const_spec((D_MIX, D_MODEL)),
            _const_spec((DEPTH + 1, HGRN_KW)),
            _const_spec((1, ATTN_WIDTH)),
            _const_spec((1, HGRN_VW)),
            _const_spec((1, D_MODEL)),
            _const_spec((1, D_MODEL)),
        ],
        out_specs=[
            tile_spec,
            pl.BlockSpec((1, HGRN_HEADS, HGRN_DK, HGRN_DV), lambda b, j: (b, 0, 0, 0)),
            pl.BlockSpec((1, WINDOW, KV_WIDTH), lambda b, j: (b, 0, 0)),
            pl.BlockSpec((1, WINDOW, KV_WIDTH), lambda b, j: (b, 0, 0)),
        ],
        out_shape=[
            jax.ShapeDtypeStruct((n_tiles, tile_rows, D_MODEL), F32),
            jax.ShapeDtypeStruct((nb, HGRN_HEADS, HGRN_DK, HGRN_DV), F32),
            jax.ShapeDtypeStruct((nb, WINDOW, KV_WIDTH), F32),
            jax.ShapeDtypeStruct((nb, WINDOW, KV_WIDTH), F32),
        ],
        scratch_shapes=[
            pltpu.VMEM((HGRN_HEADS, HGRN_DV, HGRN_DK), F32),
            pltpu.VMEM((2, WINDOW, KV_WIDTH), F32),
        ],
        compiler_params=pltpu.CompilerParams(
            dimension_semantics=("arbitrary", "arbitrary"), vmem_limit_bytes=vmem),
        name="mixer_prompt",
    )(sink, xt, w_in, w_out, lbp, ang, hng, g2, b2)
    return y.reshape(nb, seq, D_MODEL), s, k, v


def _mixer_sample_kernel(sink_ref, x_ref, s0_ref, ck_ref, cv_ref, w_in_ref, w_out_ref, lbp_ref,
                         ang_ref, hng_ref, g2_ref, b2_ref, y_ref, s_out_ref, k_out_ref, v_out_ref):
    x = x_ref[...]
    proj = _dot(x.astype(BF16), w_in_ref[...])
    k_new = proj[:, OFF_K:OFF_V]
    v_new = proj[:, OFF_V:OFF_HQ]

    oa = _attend_sample(_head_queries(proj[:, :ATTN_WIDTH]), k_new, v_new, ck_ref, cv_ref, sink_ref)
    for i in range(SEQ_PER_STEP):
        rows = slice(TOK * i, TOK * (i + 1))
        k_out_ref[i] = jnp.concatenate([ck_ref[i, TOK:, :], k_new[rows]], axis=0)
        v_out_ref[i] = jnp.concatenate([cv_ref[i, TOK:, :], v_new[rows]], axis=0)

    q, logf, kin, vin = _hgrn_gates(proj, _lower_bound(lbp_ref[...]))
    b = _block_cumsum(_tri_matrix(TOK), logf)
    xor = (lax.broadcasted_iota(jnp.int32, (BLK, BLK), 0)
           ^ lax.broadcasted_iota(jnp.int32, (BLK, BLK), 1))
    lane = lax.broadcasted_iota(jnp.int32, (HGRN_DK, BLK), 1)
    oh = []
    for h in range(HGRN_HEADS):
        sl = slice(HGRN_DK * h, HGRN_DK * (h + 1))
        qh, kh, bh = q[:, sl], kin[:, sl], b[:, sl]
        vh = vin[:, sl].astype(BF16)
        a = _intra_scores(qh, kh, bh, TOK // 2, xor)
        o_intra = _dot(a.astype(BF16), vh)
        qe = (qh * jnp.exp(bh)).astype(BF16)
        b3 = bh.reshape(SEQ_PER_STEP, TOK, HGRN_DK)
        b_last = jnp.broadcast_to(b3[:, TOK - 1:TOK, :], b3.shape).reshape(BLK, HGRN_DK)
        kd_t = (kh * jnp.exp(b_last - bh)).T
        dec_t = jnp.exp(b_last).T
        o_rows = []
        for i in range(SEQ_PER_STEP):
            s0 = s0_ref[i, h]
            o_rows.append(_dot(qe[TOK * i:TOK * (i + 1)], s0.astype(BF16)))
            kd_i = jnp.where((lane >= TOK * i) & (lane < TOK * (i + 1)), kd_t, 0.0).astype(BF16)
            s_out_ref[i, h] = s0 * dec_t[:, TOK * i:TOK * i + 1] + _dot(kd_i, vh)
        oh.append(o_intra + jnp.concatenate(o_rows, axis=0))

    y_ref[...] = _mix_out(x, oa, oh, proj, w_out_ref, ang_ref, hng_ref, g2_ref, b2_ref)


def _mixer_sample(x, s0, ck, cv, sink, w_in, w_out, lbp, ang, hng, g2, b2):
    n = x.shape[0]
    nseq = s0.shape[0]
    state_spec = pl.BlockSpec((SEQ_PER_STEP, HGRN_HEADS, HGRN_DK, HGRN_DV), lambda i: (i, 0, 0, 0))
    cache_spec = pl.BlockSpec((SEQ_PER_STEP, WINDOW, KV_WIDTH), lambda i: (i, 0, 0))
    blocks = 2 * BLK * D_MODEL * F32_BYTES + 2 * SEQ_PER_STEP * (STATE_BYTES + 2 * WINDOW_BYTES)
    values = (_mixer_value_bytes(BLK, WINDOW + BLK)
              + HGRN_HEADS * 2 * HGRN_DK * BLK * F32_BYTES)
    vmem = _vmem_limit(MIXER_WEIGHT_BYTES, blocks, values)
    return pl.pallas_call(
        _mixer_sample_kernel,
        grid=(n // BLK,),
        in_specs=[
            pl.BlockSpec(memory_space=pltpu.SMEM),
            pl.BlockSpec((BLK, D_MODEL), lambda i: (i, 0)),
            state_spec, cache_spec, cache_spec,
            _const_spec((D_MODEL, D_IN)),
            _const_spec((D_MIX, D_MODEL)),
            _const_spec((DEPTH + 1, HGRN_KW)),
            _const_spec((1, ATTN_WIDTH)),
            _const_spec((1, HGRN_VW)),
            _const_spec((1, D_MODEL)),
            _const_spec((1, D_MODEL)),
        ],
        out_specs=[
            pl.BlockSpec((BLK, D_MODEL), lambda i: (i, 0)),
            state_spec, cache_spec, cache_spec,
        ],
        out_shape=[
            jax.ShapeDtypeStruct((n, D_MODEL), F32),
            jax.ShapeDtypeStruct((nseq, HGRN_HEADS, HGRN_DK, HGRN_DV), F32),
            jax.ShapeDtypeStruct((nseq, WINDOW, KV_WIDTH), F32),
            jax.ShapeDtypeStruct((nseq, WINDOW, KV_WIDTH), F32),
        ],
        compiler_params=pltpu.CompilerParams(
            dimension_semantics=("arbitrary",), vmem_limit_bytes=vmem),
        name="mixer_sample",
    )(sink, x, s0, ck, cv, w_in, w_out, lbp, ang, hng, g2, b2)


def kernel(x_prompt, x_sample, state_hgrn, cache_win_k, cache_win_v, ln1_g, ln1_b, ffn1_w13,
           ffn1_w2, w_in, lb_param, attn_sink, attn_norm_g, hgrn_norm_g, w_out, ln2_g, ln2_b,
           ffn2_w13, ffn2_w2, ln3_g, ln3_b):
    nb, seq, d = x_prompt.shape
    ns, tn, _ = x_sample.shape
    assert state_hgrn.shape[0] == DEPTH == 1 and d == D_MODEL
    assert seq % (PROMPT_TILES * PROMPT_BLOCKS * BLK) == 0 and tn == TOK and (ns * tn) % BLK == 0
    assert cache_win_k.shape[2] == WINDOW

    w13_1 = ffn1_w13[0].astype(BF16)
    w2_1 = ffn1_w2[0].astype(BF16)
    sink = attn_sink[0]
    lbp = lb_param.astype(F32)

    xp = x_prompt.reshape(nb * seq, d)
    xs = x_sample.reshape(ns * tn, d)

    xp, w13_2, w2_2, w_in_b, w_out_b = _ffn_ln(
        xp, w13_1, w2_1, ln1_g, ln1_b, to_cast=(ffn2_w13[0], ffn2_w2[0], w_in[0], w_out[0]))
    xs = _ffn_ln(xs, w13_1, w2_1, ln1_g, ln1_b)
    mix_params = (sink, w_in_b, w_out_b, lbp, attn_norm_g, hgrn_norm_g, ln2_g, ln2_b)

    xp, sp, kp, vp = _mixer_prompt(xp.reshape(nb, seq, d), *mix_params)
    xs, ss, ks, vs = _mixer_sample(
        xs, state_hgrn[0], cache_win_k[0].reshape(ns, WINDOW, KV_WIDTH),
        cache_win_v[0].reshape(ns, WINDOW, KV_WIDTH), *mix_params)

    xp = _ffn_ln(xp.reshape(nb * seq, d), w13_2, w2_2, ln3_g, ln3_b)
    xs = _ffn_ln(xs, w13_2, w2_2, ln3_g, ln3_b)

    cache_tail = (WINDOW, ATTN_KV_HEADS, HEAD_DIM)
    return (xp.reshape(nb, seq, d), xs.reshape(ns, tn, d),
            sp[None], kp.reshape((DEPTH, nb) + cache_tail), vp.reshape((DEPTH, nb) + cache_tail),
            ss[None], ks.reshape((DEPTH, ns) + cache_tail), vs.reshape((DEPTH, ns) + cache_tail))
```

```python
import functools

import jax
import jax.numpy as jnp
from jax import lax
from jax.experimental import pallas as pl
from jax.experimental.pallas import tpu as pltpu

F32 = jnp.float32
BF16 = jnp.bfloat16

D_MODEL = 1024
DEPTH = 1
ATTN_HEADS = 8
ATTN_KV_HEADS = 2
GROUP = ATTN_HEADS // ATTN_KV_HEADS
HEAD_DIM = 64
ATTN_WIDTH = ATTN_HEADS * HEAD_DIM
KV_WIDTH = ATTN_KV_HEADS * HEAD_DIM
WINDOW = 128
ATTN_SCALE = HEAD_DIM ** -0.5
HGRN_HEADS = 4
HGRN_DK = 128
HGRN_DV = 128
HGRN_KW = HGRN_HEADS * HGRN_DK
HGRN_VW = HGRN_HEADS * HGRN_DV
D_MIX = ATTN_WIDTH + HGRN_VW
D_IN = ATTN_WIDTH + 2 * KV_WIDTH + 2 * HGRN_KW + 2 * HGRN_VW
D_FF = 2816
ALPHA = (2.0 * DEPTH) ** 0.25
LN_EPS = 1e-5
RMS_EPS = 1e-6

OFF_K = ATTN_WIDTH
OFF_V = OFF_K + KV_WIDTH
OFF_HQ = OFF_V + KV_WIDTH
OFF_HF = OFF_HQ + HGRN_KW
OFF_HI = OFF_HF + HGRN_KW
OFF_HG = OFF_HI + HGRN_VW

LANES = 128
SUBLANES = 8
BLK = 128
PROMPT_BLOCKS = 2
PROMPT_TILES = 2
PROJ_CHUNK = 256
FFN_TILE = 256
FFN_TILES = 4
FFN_CHUNK = 256
TOK = SUBLANES
SEQ_PER_STEP = BLK // TOK
V7X_VMEM_BYTES = 64 * 1024 * 1024
F32_BYTES = 4
BF16_BYTES = 2

NT_DIMS = (((1,), (1,)), ((), ()))
TN_DIMS = (((0,), (0,)), ((), ()))


def _vmem_limit(resident_bytes, streamed_bytes, value_bytes):
    need = resident_bytes + 2 * streamed_bytes + value_bytes
    assert need <= V7X_VMEM_BYTES, need
    return need


_DONE = object()


def _round_robin(streams):
    live = list(streams)
    rounds = 0
    while live:
        live = [g for g in live if next(g, _DONE) is not _DONE]
        rounds += 1
        assert rounds < 100000, "a stream waits for something no other stream produces"


def _dot(a, b):
    return jnp.dot(a, b, preferred_element_type=F32)


def _dot_nt(a, b):
    return lax.dot_general(a, b, NT_DIMS, preferred_element_type=F32)


def _dot_tn(a, b):
    return lax.dot_general(a, b, TN_DIMS, preferred_element_type=F32)


def _layer_norm(r, g, b):
    mu = jnp.mean(r, axis=-1, keepdims=True)
    c = r - mu
    var = jnp.mean(c * c, axis=-1, keepdims=True)
    return c * lax.rsqrt(var + LN_EPS) * g + b


def _rms_norm(x, g):
    return x * lax.rsqrt(jnp.mean(x * x, axis=-1, keepdims=True) + RMS_EPS) * g


def _ffn_steps(x, w13_ref, w2_ref, g, b, store, on_hidden=lambda: None):
    xb = x.astype(BF16)
    acts = []
    for c in range(D_FF // FFN_CHUNK):
        gate = _dot(xb, w13_ref[:, FFN_CHUNK * c:FFN_CHUNK * (c + 1)])
        up = _dot(xb, w13_ref[:, D_FF + FFN_CHUNK * c:D_FF + FFN_CHUNK * (c + 1)])
        acts.append((jax.nn.silu(gate) * up).astype(BF16))
        yield
    act = jnp.concatenate(acts, axis=1)
    on_hidden()
    ys = []
    for c in range(D_MODEL // FFN_CHUNK):
        ys.append(_dot(act, w2_ref[:, FFN_CHUNK * c:FFN_CHUNK * (c + 1)]))
        yield
    store(_layer_norm(ALPHA * x + 0.5 * jnp.concatenate(ys, axis=1), g, b))


def _ffn_stream(x_ref, o_ref, tile, w13_ref, w2_ref, g, b, flags, may_start):
    while not may_start():
        yield
    rows = slice(FFN_TILE * tile, FFN_TILE * (tile + 1))

    def store(y):
        o_ref[rows, :] = y

    def on_hidden():
        flags[tile] = True

    yield from _ffn_steps(x_ref[rows, :], w13_ref, w2_ref, g, b, store, on_hidden)


def _cast_stream(src_refs, dst_refs):
    for src, dst in zip(src_refs, dst_refs):
        dst[...] = src[...].astype(BF16)
        yield


def _ffn_ln_kernel(n_cast, x_ref, w13_ref, w2_ref, g_ref, b_ref, *refs):
    cast_src, o_ref, cast_dst = refs[:n_cast], refs[n_cast], refs[n_cast + 1:]
    n_tiles = x_ref.shape[0] // FFN_TILE
    g, b = g_ref[...], b_ref[...]
    flags = [False] * n_tiles
    streams = [_ffn_stream(x_ref, o_ref, t, w13_ref, w2_ref, g, b, flags,
                           (lambda: True) if t == 0 else (lambda t=t: flags[t - 1]))
               for t in range(n_tiles)]
    _round_robin(streams + [_cast_stream(cast_src, cast_dst)])


def _const_spec(shape):
    return pl.BlockSpec(shape, lambda *_: (0,) * len(shape), pipeline_mode=pl.Buffered(1))


def _slab_spec(shape, steps):
    rows, cols = shape
    count = max(c for c in range(1, steps + 1)
                if rows % c == 0 and (rows // c) % (2 * SUBLANES) == 0)
    return pl.BlockSpec((rows // count, cols), lambda i: (jnp.minimum(i, count - 1), 0))


def _ffn_ln(x, w13, w2, g, b, to_cast=()):
    n = x.shape[0]
    tm = FFN_TILE * FFN_TILES
    assert n % tm == 0
    steps = n // tm
    slabs = [_slab_spec(w.shape, steps) for w in to_cast]
    weights = (D_MODEL * 2 * D_FF + D_FF * D_MODEL) * BF16_BYTES
    blocks = (2 * tm * D_MODEL * F32_BYTES
              + sum(s.block_shape[0] * s.block_shape[1] for s in slabs) * (F32_BYTES + BF16_BYTES))
    values = 2 * FFN_TILE * ((D_MODEL + D_FF) * BF16_BYTES
                             + (2 * FFN_CHUNK + 2 * D_MODEL) * F32_BYTES)
    vmem = _vmem_limit(weights, blocks, values)
    out = pl.pallas_call(
        functools.partial(_ffn_ln_kernel, len(to_cast)),
        grid=(steps,),
        in_specs=[
            pl.BlockSpec((tm, D_MODEL), lambda i: (i, 0)),
            _const_spec((D_MODEL, 2 * D_FF)),
            _const_spec((D_FF, D_MODEL)),
            _const_spec((1, D_MODEL)),
            _const_spec((1, D_MODEL)),
        ] + slabs,
        out_specs=[pl.BlockSpec((tm, D_MODEL), lambda i: (i, 0))] + slabs,
        out_shape=[jax.ShapeDtypeStruct((n, D_MODEL), F32)]
        + [jax.ShapeDtypeStruct(w.shape, BF16) for w in to_cast],
        compiler_params=pltpu.CompilerParams(
            dimension_semantics=("arbitrary",), vmem_limit_bytes=vmem),
        name="ffn_ln",
    )(x, w13, w2, g, b, *to_cast)
    return tuple(out) if to_cast else out[0]


def _lower_bound(lbp):
    mx = jnp.max(lbp, axis=0, keepdims=True)
    e = jnp.exp(lbp - mx)
    return e[0:1] / jnp.sum(e, axis=0, keepdims=True)


def _split3(x):
    hi = x.astype(BF16)
    r = x - hi.astype(F32)
    mid = r.astype(BF16)
    lo = (r - mid.astype(F32)).astype(BF16)
    return hi, mid, lo


def _block_cumsum(tri, g):
    w = g.shape[1]
    parts = jnp.concatenate(_split3(g), axis=1)
    s = _dot(tri, parts)
    return s[:, :w] + s[:, w:2 * w] + s[:, 2 * w:]


def _tri_matrix(group):
    r = lax.broadcasted_iota(jnp.int32, (BLK, BLK), 0)
    c = lax.broadcasted_iota(jnp.int32, (BLK, BLK), 1)
    keep = (c <= r) & ((r ^ c) < group)
    return jnp.where(keep, 1.0, 0.0).astype(BF16)


def _pair_ref_rows(b, m):
    t, w = b.shape
    if m >= SUBLANES:
        b3 = b.reshape(t // (2 * m), 2 * m, w)
        return jnp.broadcast_to(b3[:, m - 1:m, :], b3.shape).reshape(t, w)
    b3 = b.reshape(t // SUBLANES, SUBLANES, w)
    sub = lax.broadcasted_iota(jnp.int32, b3.shape, 1)

    def row(i):
        return jnp.broadcast_to(b3[:, i:i + 1, :], b3.shape)

    if m == 4:
        out = row(3)
    elif m == 2:
        out = jnp.where(sub < 4, row(1), row(5))
    else:
        out = jnp.where(sub < 4, jnp.where(sub < 2, row(0), row(2)),
                        jnp.where(sub < 6, row(4), row(6)))
    return out.reshape(t, w)


def _level_operands(q, k, b, m):
    t, w = q.shape
    ref = _pair_ref_rows(b, m)
    if m >= SUBLANES:
        zero = jnp.zeros((m, w), F32)
        qs, ks = [], []
        for blk in range(t // m):
            rows = slice(blk * m, (blk + 1) * m)
            if blk % 2:
                qs.append(q[rows] * jnp.exp2(b[rows] - ref[rows]))
                ks.append(zero)
            else:
                qs.append(zero)
                ks.append(k[rows] * jnp.exp2(ref[rows] - b[rows]))
        return jnp.concatenate(qs, axis=0).astype(BF16), jnp.concatenate(ks, axis=0).astype(BF16)
    upper = (lax.broadcasted_iota(jnp.int32, q.shape, 0) & m) != 0
    e = jnp.exp2((b - ref) * jnp.where(upper, 1.0, -1.0))
    return (jnp.where(upper, q * e, 0.0).astype(BF16), jnp.where(upper, 0.0, k * e).astype(BF16))


def _intra_steps(q, k, b, max_m, xor, out):
    a = jnp.where(xor == 0, jnp.sum(q * k, axis=-1, keepdims=True), 0.0)
    yield
    m = 1
    while m <= max_m:
        qs, ks = _level_operands(q, k, b, m)
        yield
        p = _dot_nt(qs, ks)
        yield
        a = jnp.where(xor >= m, p, a)
        m *= 2
    if 2 * max_m < q.shape[0]:
        a = jnp.where(xor < 2 * max_m, a, 0.0)
    out["a"] = a


def _intra_scores(q, k, b, max_m, xor):
    out = {}
    for _ in _intra_steps(q, k, b, max_m, xor, out):
        pass
    return out["a"]


def _hgrn_gates(proj, lb):
    q = jax.nn.silu(proj[:, OFF_HQ:OFF_HF])
    f = lb + (1.0 - lb) * jax.nn.sigmoid(proj[:, OFF_HF:OFF_HI])
    return q, jnp.log2(f), 1.0 - f, proj[:, OFF_HI:OFF_HG]


def _head_queries(qa):
    low = lax.broadcasted_iota(jnp.int32, (qa.shape[0], LANES), 1) < HEAD_DIM
    out = []
    for pair in range(ATTN_HEADS // 2):
        piece = qa[:, LANES * pair:LANES * (pair + 1)] * ATTN_SCALE
        rolled = pltpu.roll(piece, HEAD_DIM, axis=1)
        for half in range(2):
            kv = (2 * pair + half) // GROUP
            src = piece if half == kv else rolled
            z = jnp.where(low, src, 0.0) if kv == 0 else jnp.where(low, 0.0, src)
            out.append(z.astype(BF16))
    return out


def _merge_heads(o):
    low = lax.broadcasted_iota(jnp.int32, o[0].shape, 1) < HEAD_DIM
    cols = []
    for pair in range(ATTN_HEADS // 2):
        kv = (2 * pair) // GROUP
        even, odd = o[2 * pair], o[2 * pair + 1]
        if kv == 0:
            odd = pltpu.roll(odd, HEAD_DIM, axis=1)
        else:
            even = pltpu.roll(even, HEAD_DIM, axis=1)
        cols.append(jnp.where(low, even, odd))
    return jnp.concatenate(cols, axis=1)


def _slope(head):
    return 2.0 ** (-8.0 * (head + 1) / ATTN_HEADS)


def _alibi_bias(limit):
    trow = lax.broadcasted_iota(jnp.int32, limit.shape, 0)
    ccol = lax.broadcasted_iota(jnp.int32, limit.shape, 1)
    dist = trow + WINDOW - ccol
    visible = dist.astype(jnp.uint32) < limit.astype(jnp.uint32)
    distf = dist.astype(F32)
    return [jnp.where(visible, -_slope(head) * distf, -jnp.inf) for head in range(ATTN_HEADS)]


def _attend_steps(qz, k_all, v_all, sink_ref, bias, out):
    tq = qz[0].shape[0]
    s_all = _dot_nt(jnp.concatenate(qz, axis=0), k_all.astype(BF16))
    yield
    ps, dens = [], []
    for head in range(ATTN_HEADS):
        sink = sink_ref[head]
        s = s_all[tq * head:tq * (head + 1)] + bias[head]
        mx = jnp.maximum(jnp.max(s, axis=-1, keepdims=True), sink)
        yield
        p = jnp.exp(s - mx)
        dens.append(jnp.sum(p, axis=-1, keepdims=True) + jnp.exp(sink - mx))
        ps.append(p.astype(BF16))
        yield
    o_all = _dot(jnp.concatenate(ps, axis=0), v_all.astype(BF16))
    yield
    out["oa"] = _merge_heads([o_all[tq * h:tq * (h + 1)] / dens[h] for h in range(ATTN_HEADS)])


def _attend_sample(qz, k_new, v_new, ck_ref, cv_ref, sink_ref):
    rows = qz[0].shape[0]
    k_new = k_new.astype(BF16)
    v_new = v_new.astype(BF16)
    sn_all = _dot_nt(jnp.concatenate(qz, axis=0), k_new)
    sc_seq = []
    for i in range(SEQ_PER_STEP):
        lhs = jnp.concatenate([z[TOK * i:TOK * (i + 1)] for z in qz], axis=0)
        sc_seq.append(_dot_nt(lhs, ck_ref[i].astype(BF16)))

    r = lax.broadcasted_iota(jnp.int32, (rows, rows), 0)
    c = lax.broadcasted_iota(jnp.int32, (rows, rows), 1)
    new_visible = ((r ^ c) < TOK) & (c <= r)
    new_dist = (r - c).astype(F32)
    tok = lax.broadcasted_iota(jnp.int32, (rows, WINDOW), 0) & (TOK - 1)
    key = lax.broadcasted_iota(jnp.int32, (rows, WINDOW), 1)
    old_visible = key > tok
    old_dist = (WINDOW + tok - key).astype(F32)

    pn, pc, dens = [], [], []
    for head in range(ATTN_HEADS):
        sink = sink_ref[head]
        slope = _slope(head)
        sn = sn_all[rows * head:rows * (head + 1)]
        sc = jnp.concatenate([s[TOK * head:TOK * (head + 1)] for s in sc_seq], axis=0)
        sn = jnp.where(new_visible, sn - slope * new_dist, -jnp.inf)
        sc = jnp.where(old_visible, sc - slope * old_dist, -jnp.inf)
        mx = jnp.maximum(jnp.maximum(jnp.max(sn, axis=-1, keepdims=True),
                                     jnp.max(sc, axis=-1, keepdims=True)), sink)
        en = jnp.exp(sn - mx)
        ec = jnp.exp(sc - mx)
        dens.append(jnp.sum(en, axis=-1, keepdims=True) + jnp.sum(ec, axis=-1, keepdims=True)
                    + jnp.exp(sink - mx))
        pn.append(en.astype(BF16))
        pc.append(ec.astype(BF16))

    on_all = _dot(jnp.concatenate(pn, axis=0), v_new)
    oc_seq = []
    for i in range(SEQ_PER_STEP):
        lhs = jnp.concatenate([p[TOK * i:TOK * (i + 1)] for p in pc], axis=0)
        oc_seq.append(_dot(lhs, cv_ref[i].astype(BF16)))
    o = []
    for head in range(ATTN_HEADS):
        oc = jnp.concatenate([s[TOK * head:TOK * (head + 1)] for s in oc_seq], axis=0)
        o.append((on_all[rows * head:rows * (head + 1)] + oc) / dens[head])
    return _merge_heads(o)


def _mix_out(x, oa, oh, proj, w_out_ref, ang_ref, hng_ref, g2_ref, b2_ref):
    oa = _rms_norm(oa, ang_ref[...])
    hng = hng_ref[...]
    gate = jax.nn.silu(proj[:, OFF_HG:])
    ohn = [_rms_norm(oh[h], hng[:, HGRN_DV * h:HGRN_DV * (h + 1)]) for h in range(HGRN_HEADS)]
    mix = jnp.concatenate([oa, jnp.concatenate(ohn, axis=1) * gate], axis=1).astype(BF16)
    y = _dot(mix, w_out_ref[...])
    return _layer_norm(ALPHA * x + y, g2_ref[...], b2_ref[...])


def _proj_stream(ctx, x, w_in_ref, may_start):
    while not may_start():
        yield
    xb = x.astype(BF16)
    chunks = []
    for c in range(D_IN // PROJ_CHUNK):
        chunks.append(_dot(xb, w_in_ref[:, PROJ_CHUNK * c:PROJ_CHUNK * (c + 1)]))
        yield
    ctx["proj"] = jnp.concatenate(chunks, axis=1)


def _prepare_stream(ctx, lb, tri):
    while "proj" not in ctx:
        yield
    proj = ctx["proj"]
    ctx["qz"] = _head_queries(proj[:, :ATTN_WIDTH])
    yield
    ctx["q"], logf, ctx["kin"], ctx["vin"] = _hgrn_gates(proj, lb)
    yield
    ctx["b"] = [_block_cumsum(tri, logf[BLK * blk:BLK * (blk + 1)]) for blk in range(PROMPT_BLOCKS)]
    ctx["ready"] = True


def _attention_stream(ctx, blk, prev_kv, bias, sink_ref):
    while "ready" not in ctx:
        yield
    rows = slice(BLK * blk, BLK * (blk + 1))
    k_prev, v_prev = prev_kv()
    k_all = jnp.concatenate([k_prev, ctx["proj"][rows, OFF_K:OFF_V]], axis=0)
    v_all = jnp.concatenate([v_prev, ctx["proj"][rows, OFF_V:OFF_HQ]], axis=0)
    out = {}
    yield from _attend_steps([z[rows] for z in ctx["qz"]], k_all, v_all, sink_ref, bias, out)
    ctx["oa"][blk] = out["oa"]


def _hgrn_stream(ctx, blk, h, state, turn, xor):
    while "ready" not in ctx:
        yield
    rows = slice(BLK * blk, BLK * (blk + 1))
    sl = slice(HGRN_DK * h, HGRN_DK * (h + 1))
    qh, kh, bh = ctx["q"][rows, sl], ctx["kin"][rows, sl], ctx["b"][blk][:, sl]
    vh = ctx["vin"][rows, sl].astype(BF16)
    scores = {}
    yield from _intra_steps(qh, kh, bh, BLK // 2, xor, scores)
    o_intra = _dot(scores["a"].astype(BF16), vh)
    yield
    b_last = bh[BLK - 1:BLK, :]
    qe = (qh * jnp.exp2(bh)).astype(BF16)
    kd = (kh * jnp.exp2(b_last - bh)).astype(BF16)
    yield
    while state["turn"][h] != turn:
        yield
    st = state["st"][h]
    o_inter = _dot_nt(qe, st.astype(BF16))
    update = _dot_tn(vh, kd)
    yield
    state["st"][h] = st * jnp.exp2(b_last) + update
    state["turn"][h] = turn + 1
    ctx["oh"][blk][h] = o_intra + o_inter


def _mix_out_stream(ctx, x, y_store, w_out_ref, ang_ref, hng_ref, g2_ref, b2_ref):
    while any(o is None for o in ctx["oa"]) or any(o is None for blk in ctx["oh"] for o in blk):
        yield
    oa = jnp.concatenate(ctx["oa"], axis=0)
    oh = [jnp.concatenate([ctx["oh"][blk][h] for blk in range(PROMPT_BLOCKS)], axis=0)
          for h in range(HGRN_HEADS)]
    y_store(_mix_out(x, oa, oh, ctx["proj"], w_out_ref, ang_ref, hng_ref, g2_ref, b2_ref))


def _mixer_prompt_kernel(sink_ref, x_ref, w_in_ref, w_out_ref, lbp_ref, ang_ref, hng_ref, g2_ref,
                         b2_ref, y_ref, s_out_ref, k_out_ref, v_out_ref, st_ref, kv_ref):
    j = pl.program_id(1)

    @pl.when(j == 0)
    def _():
        st_ref[...] = jnp.zeros_like(st_ref)
        kv_ref[...] = jnp.zeros_like(kv_ref)

    lb = _lower_bound(lbp_ref[...])
    tri = _tri_matrix(BLK)
    xor = (lax.broadcasted_iota(jnp.int32, (BLK, BLK), 0)
           ^ lax.broadcasted_iota(jnp.int32, (BLK, BLK), 1))
    trow = lax.broadcasted_iota(jnp.int32, (BLK, 2 * WINDOW), 0)
    full_bias = _alibi_bias(jnp.full(trow.shape, WINDOW, jnp.int32))

    state = {"st": [st_ref[h] for h in range(HGRN_HEADS)], "turn": [0] * HGRN_HEADS}
    ctxs = [{"oa": [None] * PROMPT_BLOCKS,
             "oh": [[None] * HGRN_HEADS for _ in range(PROMPT_BLOCKS)]} for _ in range(PROMPT_TILES)]

    def kv_block(tile, blk):
        rows = slice(BLK * blk, BLK * (blk + 1))
        proj = ctxs[tile]["proj"]
        return proj[rows, OFF_K:OFF_V], proj[rows, OFF_V:OFF_HQ]

    streams = []
    for tile in range(PROMPT_TILES):
        ctx = ctxs[tile]
        x = x_ref[tile]
        may_start = (lambda: True) if tile == 0 else (lambda prev=ctxs[tile - 1]: "ready" in prev)
        streams.append(_proj_stream(ctx, x, w_in_ref, may_start))
        streams.append(_prepare_stream(ctx, lb, tri))
        for blk in range(PROMPT_BLOCKS):
            if tile == 0 and blk == 0:
                prev_kv = lambda: (kv_ref[0], kv_ref[1])
                bias = _alibi_bias(jnp.where(j > 0, WINDOW, trow + 1))
            else:
                g = tile * PROMPT_BLOCKS + blk - 1
                prev_kv = lambda g=g: kv_block(g // PROMPT_BLOCKS, g % PROMPT_BLOCKS)
                bias = full_bias
            streams.append(_attention_stream(ctx, blk, prev_kv, bias, sink_ref))
            for h in range(HGRN_HEADS):
                streams.append(_hgrn_stream(ctx, blk, h, state, tile * PROMPT_BLOCKS + blk, xor))

        def y_store(y, tile=tile):
            y_ref[tile] = y

        streams.append(_mix_out_stream(ctx, x, y_store, w_out_ref, ang_ref, hng_ref, g2_ref, b2_ref))
    _round_robin(streams)

    k_last, v_last = kv_block(PROMPT_TILES - 1, PROMPT_BLOCKS - 1)
    kv_ref[0] = k_last
    kv_ref[1] = v_last
    for h in range(HGRN_HEADS):
        st_ref[h] = state["st"][h]

    @pl.when(j == pl.num_programs(1) - 1)
    def _():
        k_out_ref[0] = k_last
        v_out_ref[0] = v_last
        for h in range(HGRN_HEADS):
            s_out_ref[0, h] = state["st"][h].T


MIXER_WEIGHT_BYTES = (D_MODEL * D_IN + D_MIX * D_MODEL) * BF16_BYTES
STATE_BYTES = HGRN_HEADS * HGRN_DK * HGRN_DV * F32_BYTES
WINDOW_BYTES = WINDOW * KV_WIDTH * F32_BYTES


def _mixer_value_bytes(rows, keys):
    per_row = ((D_IN + 4 * HGRN_KW + 2 * D_MODEL) * F32_BYTES
               + (ATTN_HEADS * LANES + D_MIX) * BF16_BYTES)
    scores = ATTN_HEADS * BLK * keys * (F32_BYTES + BF16_BYTES)
    return rows * per_row + (rows // BLK) * scores


def _mixer_prompt(x, sink, w_in, w_out, lbp, ang, hng, g2, b2):
    nb, seq, _ = x.shape
    tile_rows = PROMPT_BLOCKS * BLK
    steps_per_seq = seq // (PROMPT_TILES * tile_rows)
    n_tiles = nb * seq // tile_rows
    xt = x.reshape(n_tiles, tile_rows, D_MODEL)
    blocks = (2 * PROMPT_TILES * tile_rows * D_MODEL * F32_BYTES
              + STATE_BYTES + 2 * WINDOW_BYTES)
    vmem = _vmem_limit(MIXER_WEIGHT_BYTES + STATE_BYTES + 2 * WINDOW_BYTES, blocks,
                       _mixer_value_bytes(2 * tile_rows, 2 * WINDOW))
    tile_spec = pl.BlockSpec((PROMPT_TILES, tile_rows, D_MODEL),
                             lambda b, j: (b * steps_per_seq + j, 0, 0))
    y, s, k, v = pl.pallas_call(
        _mixer_prompt_kernel,
        grid=(nb, steps_per_seq),
        in_specs=[
            pl.BlockSpec(memory_space=pltpu.SMEM),
            tile_spec,
            _const_spec((D_MODEL, D_IN)),
            _const_spec((D_MIX, D_MODEL)),
            _const_spec((DEPTH + 1, HGRN_KW)),
            _const_spec((1, ATTN_WIDTH)),
            _const_spec((1, HGRN_VW)),
            _const_spec((1, D_MODEL)),
            _const_spec((1, D_MODEL)),
        ],
        out_specs=[
            tile_spec,
            pl.BlockSpec((1, HGRN_HEADS, HGRN_DK, HGRN_DV), lambda b, j: (b, 0, 0, 0)),
            pl.BlockSpec((1, WINDOW, KV_WIDTH), lambda b, j: (b, 0, 0)),
            pl.BlockSpec((1, WINDOW, KV_WIDTH), lambda b, j: (b, 0, 0)),
        ],
        out_shape=[
            jax.ShapeDtypeStruct((n_tiles, tile_rows, D_MODEL), F32),
            jax.ShapeDtypeStruct((nb, HGRN_HEADS, HGRN_DK, HGRN_DV), F32),
            jax.ShapeDtypeStruct((nb, WINDOW, KV_WIDTH), F32),
            jax.ShapeDtypeStruct((nb, WINDOW, KV_WIDTH), F32),
        ],
        scratch_shapes=[
            pltpu.VMEM((HGRN_HEADS, HGRN_DV, HGRN_DK), F32),
            pltpu.VMEM((2, WINDOW, KV_WIDTH), F32),
        ],
        compiler_params=pltpu.CompilerParams(
            dimension_semantics=("arbitrary", "arbitrary"), vmem_limit_bytes=vmem),
        name="mixer_prompt",
    )(sink, xt, w_in, w_out, lbp, ang, hng, g2, b2)
    return y.reshape(nb, seq, D_MODEL), s, k, v


def _mixer_sample_kernel(sink_ref, x_ref, s0_ref, ck_ref, cv_ref, w_in_ref, w_out_ref, lbp_ref,
                         ang_ref, hng_ref, g2_ref, b2_ref, y_ref, s_out_ref, k_out_ref, v_out_ref):
    x = x_ref[...]
    proj = _dot(x.astype(BF16), w_in_ref[...])
    k_new = proj[:, OFF_K:OFF_V]
    v_new = proj[:, OFF_V:OFF_HQ]

    oa = _attend_sample(_head_queries(proj[:, :ATTN_WIDTH]), k_new, v_new, ck_ref, cv_ref, sink_ref)
    for i in range(SEQ_PER_STEP):
        rows = slice(TOK * i, TOK * (i + 1))
        k_out_ref[i] = jnp.concatenate([ck_ref[i, TOK:, :], k_new[rows]], axis=0)
        v_out_ref[i] = jnp.concatenate([cv_ref[i, TOK:, :], v_new[rows]], axis=0)

    q, logf, kin, vin = _hgrn_gates(proj, _lower_bound(lbp_ref[...]))
    b = _block_cumsum(_tri_matrix(TOK), logf)
    xor = (lax.broadcasted_iota(jnp.int32, (BLK, BLK), 0)
           ^ lax.broadcasted_iota(jnp.int32, (BLK, BLK), 1))
    lane = lax.broadcasted_iota(jnp.int32, (HGRN_DK, BLK), 1)
    oh = []
    for h in range(HGRN_HEADS):
        sl = slice(HGRN_DK * h, HGRN_DK * (h + 1))
        qh, kh, bh = q[:, sl], kin[:, sl], b[:, sl]
        vh = vin[:, sl].astype(BF16)
        a = _intra_scores(qh, kh, bh, TOK // 2, xor)
        o_intra = _dot(a.astype(BF16), vh)
        qe = (qh * jnp.exp2(bh)).astype(BF16)
        b3 = bh.reshape(SEQ_PER_STEP, TOK, HGRN_DK)
        b_last = jnp.broadcast_to(b3[:, TOK - 1:TOK, :], b3.shape).reshape(BLK, HGRN_DK)
        kd_t = (kh * jnp.exp2(b_last - bh)).T
        dec_t = jnp.exp2(b_last).T
        o_rows = []
        for i in range(SEQ_PER_STEP):
            s0 = s0_ref[i, h]
            o_rows.append(_dot(qe[TOK * i:TOK * (i + 1)], s0.astype(BF16)))
            kd_i = jnp.where((lane >= TOK * i) & (lane < TOK * (i + 1)), kd_t, 0.0).astype(BF16)
            s_out_ref[i, h] = s0 * dec_t[:, TOK * i:TOK * i + 1] + _dot(kd_i, vh)
        oh.append(o_intra + jnp.concatenate(o_rows, axis=0))

    y_ref[...] = _mix_out(x, oa, oh, proj, w_out_ref, ang_ref, hng_ref, g2_ref, b2_ref)


def _mixer_sample(x, s0, ck, cv, sink, w_in, w_out, lbp, ang, hng, g2, b2):
    n = x.shape[0]
    nseq = s0.shape[0]
    state_spec = pl.BlockSpec((SEQ_PER_STEP, HGRN_HEADS, HGRN_DK, HGRN_DV), lambda i: (i, 0, 0, 0))
    cache_spec = pl.BlockSpec((SEQ_PER_STEP, WINDOW, KV_WIDTH), lambda i: (i, 0, 0))
    blocks = 2 * BLK * D_MODEL * F32_BYTES + 2 * SEQ_PER_STEP * (STATE_BYTES + 2 * WINDOW_BYTES)
    values = (_mixer_value_bytes(BLK, WINDOW + BLK)
              + HGRN_HEADS * 2 * HGRN_DK * BLK * F32_BYTES)
    vmem = _vmem_limit(MIXER_WEIGHT_BYTES, blocks, values)
    return pl.pallas_call(
        _mixer_sample_kernel,
        grid=(n // BLK,),
        in_specs=[
            pl.BlockSpec(memory_space=pltpu.SMEM),
            pl.BlockSpec((BLK, D_MODEL), lambda i: (i, 0)),
            state_spec, cache_spec, cache_spec,
            _const_spec((D_MODEL, D_IN)),
            _const_spec((D_MIX, D_MODEL)),
            _const_spec((DEPTH + 1, HGRN_KW)),
            _const_spec((1, ATTN_WIDTH)),
            _const_spec((1, HGRN_VW)),
            _const_spec((1, D_MODEL)),
            _const_spec((1, D_MODEL)),
        ],
        out_specs=[
            pl.BlockSpec((BLK, D_MODEL), lambda i: (i, 0)),
            state_spec, cache_spec, cache_spec,
        ],
        out_shape=[
            jax.ShapeDtypeStruct((n, D_MODEL), F32),
            jax.ShapeDtypeStruct((nseq, HGRN_HEADS, HGRN_DK, HGRN_DV), F32),
            jax.ShapeDtypeStruct((nseq, WINDOW, KV_WIDTH), F32),
            jax.ShapeDtypeStruct((nseq, WINDOW, KV_WIDTH), F32),
        ],
        compiler_params=pltpu.CompilerParams(
            dimension_semantics=("arbitrary",), vmem_limit_bytes=vmem),
        name="mixer_sample",
    )(sink, x, s0, ck, cv, w_in, w_out, lbp, ang, hng, g2, b2)


def kernel(x_prompt, x_sample, state_hgrn, cache_win_k, cache_win_v, ln1_g, ln1_b, ffn1_w13,
           ffn1_w2, w_in, lb_param, attn_sink, attn_norm_g, hgrn_norm_g, w_out, ln2_g, ln2_b,
           ffn2_w13, ffn2_w2, ln3_g, ln3_b):
    nb, seq, d = x_prompt.shape
    ns, tn, _ = x_sample.shape
    assert state_hgrn.shape[0] == DEPTH == 1 and d == D_MODEL
    assert seq % (PROMPT_TILES * PROMPT_BLOCKS * BLK) == 0 and tn == TOK and (ns * tn) % BLK == 0
    assert cache_win_k.shape[2] == WINDOW

    w13_1 = ffn1_w13[0].astype(BF16)
    w2_1 = ffn1_w2[0].astype(BF16)
    sink = attn_sink[0]
    lbp = lb_param.astype(F32)

    xp = x_prompt.reshape(nb * seq, d)
    xs = x_sample.reshape(ns * tn, d)

    xp, w13_2, w2_2, w_in_b, w_out_b = _ffn_ln(
        xp, w13_1, w2_1, ln1_g, ln1_b, to_cast=(ffn2_w13[0], ffn2_w2[0], w_in[0], w_out[0]))
    xs = _ffn_ln(xs, w13_1, w2_1, ln1_g, ln1_b)
    mix_params = (sink, w_in_b, w_out_b, lbp, attn_norm_g, hgrn_norm_g, ln2_g, ln2_b)

    xp, sp, kp, vp = _mixer_prompt(xp.reshape(nb, seq, d), *mix_params)
    xs, ss, ks, vs = _mixer_sample(
        xs, state_hgrn[0], cache_win_k[0].reshape(ns, WINDOW, KV_WIDTH),
        cache_win_v[0].reshape(ns, WINDOW, KV_WIDTH), *mix_params)

    xp = _ffn_ln(xp.reshape(nb * seq, d), w13_2, w2_2, ln3_g, ln3_b)
    xs = _ffn_ln(xs, w13_2, w2_2, ln3_g, ln3_b)

    cache_tail = (WINDOW, ATTN_KV_HEADS, HEAD_DIM)
    return (xp.reshape(nb, seq, d), xs.reshape(ns, tn, d),
            sp[None], kp.reshape((DEPTH, nb) + cache_tail), vp.reshape((DEPTH, nb) + cache_tail),
            ss[None], ks.reshape((DEPTH, ns) + cache_tail), vs.reshape((DEPTH, ns) + cache_tail))
```

```python
import functools

import jax
import jax.numpy as jnp
from jax import lax
from jax.experimental import pallas as pl
from jax.experimental.pallas import tpu as pltpu

F32 = jnp.float32
BF16 = jnp.bfloat16

D_MODEL = 1024
DEPTH = 1
ATTN_HEADS = 8
ATTN_KV_HEADS = 2
GROUP = ATTN_HEADS // ATTN_KV_HEADS
HEAD_DIM = 64
ATTN_WIDTH = ATTN_HEADS * HEAD_DIM
KV_WIDTH = ATTN_KV_HEADS * HEAD_DIM
WINDOW = 128
ATTN_SCALE = HEAD_DIM ** -0.5
HGRN_HEADS = 4
HGRN_DK = 128
HGRN_DV = 128
HGRN_KW = HGRN_HEADS * HGRN_DK
HGRN_VW = HGRN_HEADS * HGRN_DV
D_MIX = ATTN_WIDTH + HGRN_VW
D_IN = ATTN_WIDTH + 2 * KV_WIDTH + 2 * HGRN_KW + 2 * HGRN_VW
D_FF = 2816
ALPHA = (2.0 * DEPTH) ** 0.25
LN_EPS = 1e-5
RMS_EPS = 1e-6

OFF_K = ATTN_WIDTH
OFF_V = OFF_K + KV_WIDTH
OFF_HQ = OFF_V + KV_WIDTH
OFF_HF = OFF_HQ + HGRN_KW
OFF_HI = OFF_HF + HGRN_KW
OFF_HG = OFF_HI + HGRN_VW

LANES = 128
SUBLANES = 8
BLK = 128
PROMPT_BLOCKS = 2
PROMPT_TILES = 2
PROJ_CHUNK = 256
FFN_TILE = 256
FFN_TILES = 4
FFN_CHUNK = 256
TOK = SUBLANES
SEQ_PER_STEP = BLK // TOK
V7X_VMEM_BYTES = 64 * 1024 * 1024
F32_BYTES = 4
BF16_BYTES = 2

NT_DIMS = (((1,), (1,)), ((), ()))
TN_DIMS = (((0,), (0,)), ((), ()))


def _vmem_limit(resident_bytes, streamed_bytes, value_bytes):
    need = resident_bytes + 2 * streamed_bytes + value_bytes
    assert need <= V7X_VMEM_BYTES, need
    return need


_DONE = object()


def _round_robin(streams):
    live = list(streams)
    rounds = 0
    while live:
        live = [g for g in live if next(g, _DONE) is not _DONE]
        rounds += 1
        assert rounds < 100000, "a stream waits for something no other stream produces"


def _dot(a, b):
    return jnp.dot(a, b, preferred_element_type=F32)


def _dot_nt(a, b):
    return lax.dot_general(a, b, NT_DIMS, preferred_element_type=F32)


def _dot_tn(a, b):
    return lax.dot_general(a, b, TN_DIMS, preferred_element_type=F32)


def _layer_norm(r, g, b):
    mu = jnp.mean(r, axis=-1, keepdims=True)
    c = r - mu
    var = jnp.mean(c * c, axis=-1, keepdims=True)
    return c * lax.rsqrt(var + LN_EPS) * g + b


def _rms_norm(x, g):
    return x * lax.rsqrt(jnp.mean(x * x, axis=-1, keepdims=True) + RMS_EPS) * g


def _ffn_steps(x, w13_ref, w2_ref, g, b, store, on_hidden=lambda: None):
    xb = x.astype(BF16)
    acts = []
    for c in range(D_FF // FFN_CHUNK):
        gate = _dot(xb, w13_ref[:, FFN_CHUNK * c:FFN_CHUNK * (c + 1)])
        up = _dot(xb, w13_ref[:, D_FF + FFN_CHUNK * c:D_FF + FFN_CHUNK * (c + 1)])
        acts.append((jax.nn.silu(gate) * up).astype(BF16))
        yield
    act = jnp.concatenate(acts, axis=1)
    on_hidden()
    ys = []
    for c in range(D_MODEL // FFN_CHUNK):
        ys.append(_dot(act, w2_ref[:, FFN_CHUNK * c:FFN_CHUNK * (c + 1)]))
        yield
    store(_layer_norm(ALPHA * x + 0.5 * jnp.concatenate(ys, axis=1), g, b))


def _ffn_stream(x_ref, o_ref, tile, w13_ref, w2_ref, g, b, flags, may_start):
    while not may_start():
        yield
    rows = slice(FFN_TILE * tile, FFN_TILE * (tile + 1))

    def store(y):
        o_ref[rows, :] = y

    def on_hidden():
        flags[tile] = True

    yield from _ffn_steps(x_ref[rows, :], w13_ref, w2_ref, g, b, store, on_hidden)


def _cast_stream(src_refs, dst_refs):
    for src, dst in zip(src_refs, dst_refs):
        dst[...] = src[...].astype(BF16)
        yield


def _ffn_ln_kernel(n_cast, x_ref, w13_ref, w2_ref, g_ref, b_ref, *refs):
    cast_src, o_ref, cast_dst = refs[:n_cast], refs[n_cast], refs[n_cast + 1:]
    n_tiles = x_ref.shape[0] // FFN_TILE
    g, b = g_ref[...], b_ref[...]
    flags = [False] * n_tiles
    streams = [_ffn_stream(x_ref, o_ref, t, w13_ref, w2_ref, g, b, flags,
                           (lambda: True) if t == 0 else (lambda t=t: flags[t - 1]))
               for t in range(n_tiles)]
    _round_robin(streams + [_cast_stream(cast_src, cast_dst)])


def _const_spec(shape):
    return pl.BlockSpec(shape, lambda *_: (0,) * len(shape), pipeline_mode=pl.Buffered(1))


def _slab_spec(shape, steps):
    rows, cols = shape
    count = max(c for c in range(1, steps + 1)
                if rows % c == 0 and (rows // c) % (2 * SUBLANES) == 0)
    return pl.BlockSpec((rows // count, cols), lambda i: (jnp.minimum(i, count - 1), 0))


def _ffn_ln(x, w13, w2, g, b, to_cast=()):
    n = x.shape[0]
    tm = FFN_TILE * FFN_TILES
    assert n % tm == 0
    steps = n // tm
    slabs = [_slab_spec(w.shape, steps) for w in to_cast]
    weights = (D_MODEL * 2 * D_FF + D_FF * D_MODEL) * BF16_BYTES
    blocks = (2 * tm * D_MODEL * F32_BYTES
              + sum(s.block_shape[0] * s.block_shape[1] for s in slabs) * (F32_BYTES + BF16_BYTES))
    values = FFN_TILE * ((D_MODEL + D_FF) * BF16_BYTES + (2 * FFN_CHUNK + 2 * D_MODEL) * F32_BYTES)
    vmem = _vmem_limit(weights, blocks, values)
    out = pl.pallas_call(
        functools.partial(_ffn_ln_kernel, len(to_cast)),
        grid=(steps,),
        in_specs=[
            pl.BlockSpec((tm, D_MODEL), lambda i: (i, 0)),
            _const_spec((D_MODEL, 2 * D_FF)),
            _const_spec((D_FF, D_MODEL)),
            _const_spec((1, D_MODEL)),
            _const_spec((1, D_MODEL)),
        ] + slabs,
        out_specs=[pl.BlockSpec((tm, D_MODEL), lambda i: (i, 0))] + slabs,
        out_shape=[jax.ShapeDtypeStruct((n, D_MODEL), F32)]
        + [jax.ShapeDtypeStruct(w.shape, BF16) for w in to_cast],
        compiler_params=pltpu.CompilerParams(
            dimension_semantics=("arbitrary",), vmem_limit_bytes=vmem),
        name="ffn_ln",
    )(x, w13, w2, g, b, *to_cast)
    return tuple(out) if to_cast else out[0]


def _lower_bound(lbp):
    mx = jnp.max(lbp, axis=0, keepdims=True)
    e = jnp.exp(lbp - mx)
    return e[0:1] / jnp.sum(e, axis=0, keepdims=True)


def _split3(x):
    hi = x.astype(BF16)
    r = x - hi.astype(F32)
    mid = r.astype(BF16)
    lo = (r - mid.astype(F32)).astype(BF16)
    return hi, mid, lo


def _block_cumsum(tri, g):
    w = g.shape[1]
    parts = jnp.concatenate(_split3(g), axis=1)
    s = _dot(tri, parts)
    return s[:, :w] + s[:, w:2 * w] + s[:, 2 * w:]


def _tri_matrix(group):
    r = lax.broadcasted_iota(jnp.int32, (BLK, BLK), 0)
    c = lax.broadcasted_iota(jnp.int32, (BLK, BLK), 1)
    keep = (c <= r) & ((r ^ c) < group)
    return jnp.where(keep, 1.0, 0.0).astype(BF16)


def _pair_ref_rows(b, m):
    t, w = b.shape
    if m >= SUBLANES:
        b3 = b.reshape(t // (2 * m), 2 * m, w)
        return jnp.broadcast_to(b3[:, m - 1:m, :], b3.shape).reshape(t, w)
    b3 = b.reshape(t // SUBLANES, SUBLANES, w)
    sub = lax.broadcasted_iota(jnp.int32, b3.shape, 1)

    def row(i):
        return jnp.broadcast_to(b3[:, i:i + 1, :], b3.shape)

    if m == 4:
        out = row(3)
    elif m == 2:
        out = jnp.where(sub < 4, row(1), row(5))
    else:
        out = jnp.where(sub < 4, jnp.where(sub < 2, row(0), row(2)),
                        jnp.where(sub < 6, row(4), row(6)))
    return out.reshape(t, w)


def _level_operands(q, k, b, m):
    t, w = q.shape
    ref = _pair_ref_rows(b, m)
    if m >= SUBLANES:
        zero = jnp.zeros((m, w), F32)
        qs, ks = [], []
        for blk in range(t // m):
            rows = slice(blk * m, (blk + 1) * m)
            if blk % 2:
                qs.append(q[rows] * jnp.exp2(b[rows] - ref[rows]))
                ks.append(zero)
            else:
                qs.append(zero)
                ks.append(k[rows] * jnp.exp2(ref[rows] - b[rows]))
        return jnp.concatenate(qs, axis=0).astype(BF16), jnp.concatenate(ks, axis=0).astype(BF16)
    upper = (lax.broadcasted_iota(jnp.int32, q.shape, 0) & m) != 0
    e = jnp.exp2((b - ref) * jnp.where(upper, 1.0, -1.0))
    return (jnp.where(upper, q * e, 0.0).astype(BF16), jnp.where(upper, 0.0, k * e).astype(BF16))


def _intra_steps(q, k, b, max_m, xor, out):
    a = jnp.where(xor == 0, jnp.sum(q * k, axis=-1, keepdims=True), 0.0)
    yield
    m = 1
    while m <= max_m:
        qs, ks = _level_operands(q, k, b, m)
        yield
        p = _dot_nt(qs, ks)
        yield
        a = jnp.where(xor >= m, p, a)
        m *= 2
    if 2 * max_m < q.shape[0]:
        a = jnp.where(xor < 2 * max_m, a, 0.0)
    out["a"] = a


def _intra_scores(q, k, b, max_m, xor):
    out = {}
    for _ in _intra_steps(q, k, b, max_m, xor, out):
        pass
    return out["a"]


def _hgrn_gates(proj, lb):
    q = jax.nn.silu(proj[:, OFF_HQ:OFF_HF])
    f = lb + (1.0 - lb) * jax.nn.sigmoid(proj[:, OFF_HF:OFF_HI])
    return q, jnp.log2(f), 1.0 - f, proj[:, OFF_HI:OFF_HG]


def _head_queries(qa):
    low = lax.broadcasted_iota(jnp.int32, (qa.shape[0], LANES), 1) < HEAD_DIM
    out = []
    for pair in range(ATTN_HEADS // 2):
        piece = qa[:, LANES * pair:LANES * (pair + 1)] * ATTN_SCALE
        rolled = pltpu.roll(piece, HEAD_DIM, axis=1)
        for half in range(2):
            kv = (2 * pair + half) // GROUP
            src = piece if half == kv else rolled
            z = jnp.where(low, src, 0.0) if kv == 0 else jnp.where(low, 0.0, src)
            out.append(z.astype(BF16))
    return out


def _merge_heads(o):
    low = lax.broadcasted_iota(jnp.int32, o[0].shape, 1) < HEAD_DIM
    cols = []
    for pair in range(ATTN_HEADS // 2):
        kv = (2 * pair) // GROUP
        even, odd = o[2 * pair], o[2 * pair + 1]
        if kv == 0:
            odd = pltpu.roll(odd, HEAD_DIM, axis=1)
        else:
            even = pltpu.roll(even, HEAD_DIM, axis=1)
        cols.append(jnp.where(low, even, odd))
    return jnp.concatenate(cols, axis=1)


def _slope(head):
    return 2.0 ** (-8.0 * (head + 1) / ATTN_HEADS)


def _alibi_bias(limit):
    trow = lax.broadcasted_iota(jnp.int32, limit.shape, 0)
    ccol = lax.broadcasted_iota(jnp.int32, limit.shape, 1)
    dist = trow + WINDOW - ccol
    visible = dist.astype(jnp.uint32) < limit.astype(jnp.uint32)
    distf = dist.astype(F32)
    return [jnp.where(visible, -_slope(head) * distf, -jnp.inf) for head in range(ATTN_HEADS)]


def _attend_steps(qz, k_all, v_all, sink_ref, bias, out):
    tq = qz[0].shape[0]
    s_all = _dot_nt(jnp.concatenate(qz, axis=0), k_all.astype(BF16))
    yield
    ps, dens = [], []
    for head in range(ATTN_HEADS):
        sink = sink_ref[head]
        s = s_all[tq * head:tq * (head + 1)] + bias[head]
        mx = jnp.maximum(jnp.max(s, axis=-1, keepdims=True), sink)
        yield
        p = jnp.exp(s - mx)
        dens.append(jnp.sum(p, axis=-1, keepdims=True) + jnp.exp(sink - mx))
        ps.append(p.astype(BF16))
        yield
    o_all = _dot(jnp.concatenate(ps, axis=0), v_all.astype(BF16))
    yield
    out["oa"] = _merge_heads([o_all[tq * h:tq * (h + 1)] / dens[h] for h in range(ATTN_HEADS)])


def _attend_sample(qz, k_new, v_new, ck_ref, cv_ref, sink_ref):
    rows = qz[0].shape[0]
    k_new = k_new.astype(BF16)
    v_new = v_new.astype(BF16)
    sn_all = _dot_nt(jnp.concatenate(qz, axis=0), k_new)
    sc_seq = []
    for i in range(SEQ_PER_STEP):
        lhs = jnp.concatenate([z[TOK * i:TOK * (i + 1)] for z in qz], axis=0)
        sc_seq.append(_dot_nt(lhs, ck_ref[i].astype(BF16)))

    r = lax.broadcasted_iota(jnp.int32, (rows, rows), 0)
    c = lax.broadcasted_iota(jnp.int32, (rows, rows), 1)
    new_visible = ((r ^ c) < TOK) & (c <= r)
    new_dist = (r - c).astype(F32)
    tok = lax.broadcasted_iota(jnp.int32, (rows, WINDOW), 0) & (TOK - 1)
    key = lax.broadcasted_iota(jnp.int32, (rows, WINDOW), 1)
    old_visible = key > tok
    old_dist = (WINDOW + tok - key).astype(F32)

    pn, pc, dens = [], [], []
    for head in range(ATTN_HEADS):
        sink = sink_ref[head]
        slope = _slope(head)
        sn = sn_all[rows * head:rows * (head + 1)]
        sc = jnp.concatenate([s[TOK * head:TOK * (head + 1)] for s in sc_seq], axis=0)
        sn = jnp.where(new_visible, sn - slope * new_dist, -jnp.inf)
        sc = jnp.where(old_visible, sc - slope * old_dist, -jnp.inf)
        mx = jnp.maximum(jnp.maximum(jnp.max(sn, axis=-1, keepdims=True),
                                     jnp.max(sc, axis=-1, keepdims=True)), sink)
        en = jnp.exp(sn - mx)
        ec = jnp.exp(sc - mx)
        dens.append(jnp.sum(en, axis=-1, keepdims=True) + jnp.sum(ec, axis=-1, keepdims=True)
                    + jnp.exp(sink - mx))
        pn.append(en.astype(BF16))
        pc.append(ec.astype(BF16))

    on_all = _dot(jnp.concatenate(pn, axis=0), v_new)
    oc_seq = []
    for i in range(SEQ_PER_STEP):
        lhs = jnp.concatenate([p[TOK * i:TOK * (i + 1)] for p in pc], axis=0)
        oc_seq.append(_dot(lhs, cv_ref[i].astype(BF16)))
    o = []
    for head in range(ATTN_HEADS):
        oc = jnp.concatenate([s[TOK * head:TOK * (head + 1)] for s in oc_seq], axis=0)
        o.append((on_all[rows * head:rows * (head + 1)] + oc) / dens[head])
    return _merge_heads(o)


def _mix_out(x, oa, oh, proj, w_out_ref, ang_ref, hng_ref, g2_ref, b2_ref):
    oa = _rms_norm(oa, ang_ref[...])
    hng = hng_ref[...]
    gate = jax.nn.silu(proj[:, OFF_HG:])
    ohn = [_rms_norm(oh[h], hng[:, HGRN_DV * h:HGRN_DV * (h + 1)]) for h in range(HGRN_HEADS)]
    mix = jnp.concatenate([oa, jnp.concatenate(ohn, axis=1) * gate], axis=1).astype(BF16)
    y = _dot(mix, w_out_ref[...])
    return _layer_norm(ALPHA * x + y, g2_ref[...], b2_ref[...])


def _proj_stream(ctx, x, w_in_ref, may_start):
    while not may_start():
        yield
    xb = x.astype(BF16)
    chunks = []
    for c in range(D_IN // PROJ_CHUNK):
        chunks.append(_dot(xb, w_in_ref[:, PROJ_CHUNK * c:PROJ_CHUNK * (c + 1)]))
        yield
    ctx["proj"] = jnp.concatenate(chunks, axis=1)


def _prepare_stream(ctx, lb, tri):
    while "proj" not in ctx:
        yield
    proj = ctx["proj"]
    ctx["qz"] = _head_queries(proj[:, :ATTN_WIDTH])
    yield
    ctx["q"], logf, ctx["kin"], ctx["vin"] = _hgrn_gates(proj, lb)
    yield
    ctx["b"] = [_block_cumsum(tri, logf[BLK * blk:BLK * (blk + 1)]) for blk in range(PROMPT_BLOCKS)]
    ctx["ready"] = True


def _attention_stream(ctx, blk, prev_kv, bias, sink_ref):
    while "ready" not in ctx:
        yield
    rows = slice(BLK * blk, BLK * (blk + 1))
    k_prev, v_prev = prev_kv()
    k_all = jnp.concatenate([k_prev, ctx["proj"][rows, OFF_K:OFF_V]], axis=0)
    v_all = jnp.concatenate([v_prev, ctx["proj"][rows, OFF_V:OFF_HQ]], axis=0)
    out = {}
    yield from _attend_steps([z[rows] for z in ctx["qz"]], k_all, v_all, sink_ref, bias, out)
    ctx["oa"][blk] = out["oa"]


def _hgrn_stream(ctx, blk, h, state, turn, xor):
    while "ready" not in ctx:
        yield
    rows = slice(BLK * blk, BLK * (blk + 1))
    sl = slice(HGRN_DK * h, HGRN_DK * (h + 1))
    qh, kh, bh = ctx["q"][rows, sl], ctx["kin"][rows, sl], ctx["b"][blk][:, sl]
    vh = ctx["vin"][rows, sl].astype(BF16)
    scores = {}
    yield from _intra_steps(qh, kh, bh, BLK // 2, xor, scores)
    o_intra = _dot(scores["a"].astype(BF16), vh)
    yield
    b_last = bh[BLK - 1:BLK, :]
    qe = (qh * jnp.exp2(bh)).astype(BF16)
    kd = (kh * jnp.exp2(b_last - bh)).astype(BF16)
    yield
    while state["turn"][h] != turn:
        yield
    st = state["st"][h]
    o_inter = _dot_nt(qe, st.astype(BF16))
    update = _dot_tn(vh, kd)
    yield
    state["st"][h] = st * jnp.exp2(b_last) + update
    state["turn"][h] = turn + 1
    ctx["oh"][blk][h] = o_intra + o_inter


def _mix_out_stream(ctx, x, y_store, w_out_ref, ang_ref, hng_ref, g2_ref, b2_ref):
    while any(o is None for o in ctx["oa"]) or any(o is None for blk in ctx["oh"] for o in blk):
        yield
    oa = jnp.concatenate(ctx["oa"], axis=0)
    oh = [jnp.concatenate([ctx["oh"][blk][h] for blk in range(PROMPT_BLOCKS)], axis=0)
          for h in range(HGRN_HEADS)]
    y_store(_mix_out(x, oa, oh, ctx["proj"], w_out_ref, ang_ref, hng_ref, g2_ref, b2_ref))


def _mixer_prompt_kernel(sink_ref, x_ref, w_in_ref, w_out_ref, lbp_ref, ang_ref, hng_ref, g2_ref,
                         b2_ref, y_ref, s_out_ref, k_out_ref, v_out_ref, st_ref, kv_ref):
    j = pl.program_id(1)

    @pl.when(j == 0)
    def _():
        st_ref[...] = jnp.zeros_like(st_ref)
        kv_ref[...] = jnp.zeros_like(kv_ref)

    lb = _lower_bound(lbp_ref[...])
    tri = _tri_matrix(BLK)
    xor = (lax.broadcasted_iota(jnp.int32, (BLK, BLK), 0)
           ^ lax.broadcasted_iota(jnp.int32, (BLK, BLK), 1))
    trow = lax.broadcasted_iota(jnp.int32, (BLK, 2 * WINDOW), 0)
    full_bias = _alibi_bias(jnp.full(trow.shape, WINDOW, jnp.int32))

    state = {"st": [st_ref[h] for h in range(HGRN_HEADS)], "turn": [0] * HGRN_HEADS}
    ctxs = [{"oa": [None] * PROMPT_BLOCKS,
             "oh": [[None] * HGRN_HEADS for _ in range(PROMPT_BLOCKS)]} for _ in range(PROMPT_TILES)]

    def kv_block(tile, blk):
        rows = slice(BLK * blk, BLK * (blk + 1))
        proj = ctxs[tile]["proj"]
        return proj[rows, OFF_K:OFF_V], proj[rows, OFF_V:OFF_HQ]

    streams = []
    for tile in range(PROMPT_TILES):
        ctx = ctxs[tile]
        x = x_ref[tile]
        may_start = (lambda: True) if tile == 0 else (lambda prev=ctxs[tile - 1]: "ready" in prev)
        streams.append(_proj_stream(ctx, x, w_in_ref, may_start))
        streams.append(_prepare_stream(ctx, lb, tri))
        for blk in range(PROMPT_BLOCKS):
            if tile == 0 and blk == 0:
                prev_kv = lambda: (kv_ref[0], kv_ref[1])
                bias = _alibi_bias(jnp.where(j > 0, WINDOW, trow + 1))
            else:
                g = tile * PROMPT_BLOCKS + blk - 1
                prev_kv = lambda g=g: kv_block(g // PROMPT_BLOCKS, g % PROMPT_BLOCKS)
                bias = full_bias
            streams.append(_attention_stream(ctx, blk, prev_kv, bias, sink_ref))
            for h in range(HGRN_HEADS):
                streams.append(_hgrn_stream(ctx, blk, h, state, tile * PROMPT_BLOCKS + blk, xor))

        def y_store(y, tile=tile):
            y_ref[tile] = y

        streams.append(_mix_out_stream(ctx, x, y_store, w_out_ref, ang_ref, hng_ref, g2_ref, b2_ref))
    _round_robin(streams)

    k_last, v_last = kv_block(PROMPT_TILES - 1, PROMPT_BLOCKS - 1)
    kv_ref[0] = k_last
    kv_ref[1] = v_last
    for h in range(HGRN_HEADS):
        st_ref[h] = state["st"][h]

    @pl.when(j == pl.num_programs(1) - 1)
    def _():
        k_out_ref[0] = k_last
        v_out_ref[0] = v_last
        for h in range(HGRN_HEADS):
            s_out_ref[0, h] = state["st"][h].T


MIXER_WEIGHT_BYTES = (D_MODEL * D_IN + D_MIX * D_MODEL) * BF16_BYTES
STATE_BYTES = HGRN_HEADS * HGRN_DK * HGRN_DV * F32_BYTES
WINDOW_BYTES = WINDOW * KV_WIDTH * F32_BYTES


def _mixer_value_bytes(rows, keys):
    per_row = ((D_IN + 4 * HGRN_KW + 2 * D_MODEL) * F32_BYTES
               + (ATTN_HEADS * LANES + D_MIX) * BF16_BYTES)
    scores = ATTN_HEADS * BLK * keys * (F32_BYTES + BF16_BYTES)
    return rows * per_row + (rows // BLK) * scores


def _mixer_prompt(x, sink, w_in, w_out, lbp, ang, hng, g2, b2):
    nb, seq, _ = x.shape
    tile_rows = PROMPT_BLOCKS * BLK
    steps_per_seq = seq // (PROMPT_TILES * tile_rows)
    n_tiles = nb * seq // tile_rows
    xt = x.reshape(n_tiles, tile_rows, D_MODEL)
    blocks = (2 * PROMPT_TILES * tile_rows * D_MODEL * F32_BYTES
              + STATE_BYTES + 2 * WINDOW_BYTES)
    vmem = _vmem_limit(MIXER_WEIGHT_BYTES + STATE_BYTES + 2 * WINDOW_BYTES, blocks,
                       _mixer_value_bytes(tile_rows + BLK, 2 * WINDOW))
    tile_spec = pl.BlockSpec((PROMPT_TILES, tile_rows, D_MODEL),
                             lambda b, j: (b * steps_per_seq + j, 0, 0))
    y, s, k, v = pl.pallas_call(
        _mixer_prompt_kernel,
        grid=(nb, steps_per_seq),
        in_specs=[
            pl.BlockSpec(memory_space=pltpu.SMEM),
            tile_spec,
            _const_spec((D_MODEL, D_IN)),
            _const_spec((D_MIX, D_MODEL)),
            _const_spec((DEPTH + 1, HGRN_KW)),
            _const_spec((1, ATTN_WIDTH)),
            _const_spec((1, HGRN_VW)),
            _const_spec((1, D_MODEL)),
            _const_spec((1, D_MODEL)),
        ],
        out_specs=[
            tile_spec,
            pl.BlockSpec((1, HGRN_HEADS, HGRN_DK, HGRN_DV), lambda b, j: (b, 0, 0, 0)),
            pl.BlockSpec((1, WINDOW, KV_WIDTH), lambda b, j: (b, 0, 0)),
            pl.BlockSpec((1, WINDOW, KV_WIDTH), lambda b, j: (b, 0, 0)),
        ],
        out_shape=[
            jax.ShapeDtypeStruct((n_tiles, tile_rows, D_MODEL), F32),
            jax.ShapeDtypeStruct((nb, HGRN_HEADS, HGRN_DK, HGRN_DV), F32),
            jax.ShapeDtypeStruct((nb, WINDOW, KV_WIDTH), F32),
            jax.ShapeDtypeStruct((nb, WINDOW, KV_WIDTH), F32),
        ],
        scratch_shapes=[
            pltpu.VMEM((HGRN_HEADS, HGRN_DV, HGRN_DK), F32),
            pltpu.VMEM((2, WINDOW, KV_WIDTH), F32),
        ],
        compiler_params=pltpu.CompilerParams(
            dimension_semantics=("arbitrary", "arbitrary"), vmem_limit_bytes=vmem),
        name="mixer_prompt",
    )(sink, xt, w_in, w_out, lbp, ang, hng, g2, b2)
    return y.reshape(nb, seq, D_MODEL), s, k, v


def _mixer_sample_kernel(sink_ref, x_ref, s0_ref, ck_ref, cv_ref, w_in_ref, w_out_ref, lbp_ref,
                         ang_ref, hng_ref, g2_ref, b2_ref, y_ref, s_out_ref, k_out_ref, v_out_ref):
    x = x_ref[...]
    proj = _dot(x.astype(BF16), w_in_ref[...])
    k_new = proj[:, OFF_K:OFF_V]
    v_new = proj[:, OFF_V:OFF_HQ]

    oa = _attend_sample(_head_queries(proj[:, :ATTN_WIDTH]), k_new, v_new, ck_ref, cv_ref, sink_ref)
    for i in range(SEQ_PER_STEP):
        rows = slice(TOK * i, TOK * (i + 1))
        k_out_ref[i] = jnp.concatenate([ck_ref[i, TOK:, :], k_new[rows]], axis=0)
        v_out_ref[i] = jnp.concatenate([cv_ref[i, TOK:, :], v_new[rows]], axis=0)

    q, logf, kin, vin = _hgrn_gates(proj, _lower_bound(lbp_ref[...]))
    b = _block_cumsum(_tri_matrix(TOK), logf)
    xor = (lax.broadcasted_iota(jnp.int32, (BLK, BLK), 0)
           ^ lax.broadcasted_iota(jnp.int32, (BLK, BLK), 1))
    lane = lax.broadcasted_iota(jnp.int32, (HGRN_DK, BLK), 1)
    oh = []
    for h in range(HGRN_HEADS):
        sl = slice(HGRN_DK * h, HGRN_DK * (h + 1))
        qh, kh, bh = q[:, sl], kin[:, sl], b[:, sl]
        vh = vin[:, sl].astype(BF16)
        a = _intra_scores(qh, kh, bh, TOK // 2, xor)
        o_intra = _dot(a.astype(BF16), vh)
        qe = (qh * jnp.exp2(bh)).astype(BF16)
        b3 = bh.reshape(SEQ_PER_STEP, TOK, HGRN_DK)
        b_last = jnp.broadcast_to(b3[:, TOK - 1:TOK, :], b3.shape).reshape(BLK, HGRN_DK)
        kd_t = (kh * jnp.exp2(b_last - bh)).T
        dec_t = jnp.exp2(b_last).T
        o_rows = []
        for i in range(SEQ_PER_STEP):
            s0 = s0_ref[i, h]
            o_rows.append(_dot(qe[TOK * i:TOK * (i + 1)], s0.astype(BF16)))
            kd_i = jnp.where((lane >= TOK * i) & (lane < TOK * (i + 1)), kd_t, 0.0).astype(BF16)
            s_out_ref[i, h] = s0 * dec_t[:, TOK * i:TOK * i + 1] + _dot(kd_i, vh)
        oh.append(o_intra + jnp.concatenate(o_rows, axis=0))

    y_ref[...] = _mix_out(x, oa, oh, proj, w_out_ref, ang_ref, hng_ref, g2_ref, b2_ref)


def _mixer_sample(x, s0, ck, cv, sink, w_in, w_out, lbp, ang, hng, g2, b2):
    n = x.shape[0]
    nseq = s0.shape[0]
    state_spec = pl.BlockSpec((SEQ_PER_STEP, HGRN_HEADS, HGRN_DK, HGRN_DV), lambda i: (i, 0, 0, 0))
    cache_spec = pl.BlockSpec((SEQ_PER_STEP, WINDOW, KV_WIDTH), lambda i: (i, 0, 0))
    blocks = 2 * BLK * D_MODEL * F32_BYTES + 2 * SEQ_PER_STEP * (STATE_BYTES + 2 * WINDOW_BYTES)
    values = (_mixer_value_bytes(BLK, WINDOW + BLK)
              + HGRN_HEADS * 2 * HGRN_DK * BLK * F32_BYTES)
    vmem = _vmem_limit(MIXER_WEIGHT_BYTES, blocks, values)
    return pl.pallas_call(
        _mixer_sample_kernel,
        grid=(n // BLK,),
        in_specs=[
            pl.BlockSpec(memory_space=pltpu.SMEM),
            pl.BlockSpec((BLK, D_MODEL), lambda i: (i, 0)),
            state_spec, cache_spec, cache_spec,
            _const_spec((D_MODEL, D_IN)),
            _const_spec((D_MIX, D_MODEL)),
            _const_spec((DEPTH + 1, HGRN_KW)),
            _const_spec((1, ATTN_WIDTH)),
            _const_spec((1, HGRN_VW)),
            _const_spec((1, D_MODEL)),
            _const_spec((1, D_MODEL)),
        ],
        out_specs=[
            pl.BlockSpec((BLK, D_MODEL), lambda i: (i, 0)),
            state_spec, cache_spec, cache_spec,
        ],
        out_shape=[
            jax.ShapeDtypeStruct((n, D_MODEL), F32),
            jax.ShapeDtypeStruct((nseq, HGRN_HEADS, HGRN_DK, HGRN_DV), F32),
            jax.ShapeDtypeStruct((nseq, WINDOW, KV_WIDTH), F32),
            jax.ShapeDtypeStruct((nseq, WINDOW, KV_WIDTH), F32),
        ],
        compiler_params=pltpu.CompilerParams(
            dimension_semantics=("arbitrary",), vmem_limit_bytes=vmem),
        name="mixer_sample",
    )(sink, x, s0, ck, cv, w_in, w_out, lbp, ang, hng, g2, b2)


def kernel(x_prompt, x_sample, state_hgrn, cache_win_k, cache_win_v, ln1_g, ln1_b, ffn1_w13,
           ffn1_w2, w_in, lb_param, attn_sink, attn_norm_g, hgrn_norm_g, w_out, ln2_g, ln2_b,
           ffn2_w13, ffn2_w2, ln3_g, ln3_b):
    nb, seq, d = x_prompt.shape
    ns, tn, _ = x_sample.shape
    assert state_hgrn.shape[0] == DEPTH == 1 and d == D_MODEL
    assert seq % (PROMPT_TILES * PROMPT_BLOCKS * BLK) == 0 and tn == TOK and (ns * tn) % BLK == 0
    assert cache_win_k.shape[2] == WINDOW

    w13_1 = ffn1_w13[0].astype(BF16)
    w2_1 = ffn1_w2[0].astype(BF16)
    sink = attn_sink[0]
    lbp = lb_param.astype(F32)

    xp = x_prompt.reshape(nb * seq, d)
    xs = x_sample.reshape(ns * tn, d)

    xp, w13_2, w2_2, w_in_b, w_out_b = _ffn_ln(
        xp, w13_1, w2_1, ln1_g, ln1_b, to_cast=(ffn2_w13[0], ffn2_w2[0], w_in[0], w_out[0]))
    xs = _ffn_ln(xs, w13_1, w2_1, ln1_g, ln1_b)
    mix_params = (sink, w_in_b, w_out_b, lbp, attn_norm_g, hgrn_norm_g, ln2_g, ln2_b)

    xp, sp, kp, vp = _mixer_prompt(xp.reshape(nb, seq, d), *mix_params)
    xs, ss, ks, vs = _mixer_sample(
        xs, state_hgrn[0], cache_win_k[0].reshape(ns, WINDOW, KV_WIDTH),
        cache_win_v[0].reshape(ns, WINDOW, KV_WIDTH), *mix_params)

    xp = _ffn_ln(xp.reshape(nb * seq, d), w13_2, w2_2, ln3_g, ln3_b)
    xs = _ffn_ln(xs, w13_2, w2_2, ln3_g, ln3_b)

    cache_tail = (WINDOW, ATTN_KV_HEADS, HEAD_DIM)
    return (xp.reshape(nb, seq, d), xs.reshape(ns, tn, d),
            sp[None], kp.reshape((DEPTH, nb) + cache_tail), vp.reshape((DEPTH, nb) + cache_tail),
            ss[None], ks.reshape((DEPTH, ns) + cache_tail), vs.reshape((DEPTH, ns) + cache_tail))
```

```python
import functools

import jax
import jax.numpy as jnp
from jax import lax
from jax.experimental import pallas as pl
from jax.experimental.pallas import tpu as pltpu

F32 = jnp.float32
BF16 = jnp.bfloat16

D_MODEL = 1024
DEPTH = 1
ATTN_HEADS = 8
ATTN_KV_HEADS = 2
GROUP = ATTN_HEADS // ATTN_KV_HEADS
HEAD_DIM = 64
ATTN_WIDTH = ATTN_HEADS * HEAD_DIM
KV_WIDTH = ATTN_KV_HEADS * HEAD_DIM
WINDOW = 128
ATTN_SCALE = HEAD_DIM ** -0.5
HGRN_HEADS = 4
HGRN_DK = 128
HGRN_DV = 128
HGRN_KW = HGRN_HEADS * HGRN_DK
HGRN_VW = HGRN_HEADS * HGRN_DV
D_MIX = ATTN_WIDTH + HGRN_VW
D_IN = ATTN_WIDTH + 2 * KV_WIDTH + 2 * HGRN_KW + 2 * HGRN_VW
D_FF = 2816
ALPHA = (2.0 * DEPTH) ** 0.25
LN_EPS = 1e-5
RMS_EPS = 1e-6

OFF_K = ATTN_WIDTH
OFF_V = OFF_K + KV_WIDTH
OFF_HQ = OFF_V + KV_WIDTH
OFF_HF = OFF_HQ + HGRN_KW
OFF_HI = OFF_HF + HGRN_KW
OFF_HG = OFF_HI + HGRN_VW

LANES = 128
SUBLANES = 8
BLK = 128
PROMPT_BLOCKS = 2
PROMPT_TILES = 2
PROJ_CHUNK = 256
FFN_TILE = 256
FFN_TILES = 4
FFN_CHUNK = 256
TOK = SUBLANES
SEQ_PER_STEP = BLK // TOK
V7X_VMEM_BYTES = 64 * 1024 * 1024
F32_BYTES = 4
BF16_BYTES = 2

NT_DIMS = (((1,), (1,)), ((), ()))
TN_DIMS = (((0,), (0,)), ((), ()))


def _vmem_limit(resident_bytes, streamed_bytes, value_bytes):
    need = resident_bytes + 2 * streamed_bytes + value_bytes
    assert need <= V7X_VMEM_BYTES, need
    return need


_DONE = object()


def _round_robin(streams):
    live = list(streams)
    rounds = 0
    while live:
        live = [g for g in live if next(g, _DONE) is not _DONE]
        rounds += 1
        assert rounds < 100000, "a stream waits for something no other stream produces"


def _dot(a, b):
    return jnp.dot(a, b, preferred_element_type=F32)


def _dot_nt(a, b):
    return lax.dot_general(a, b, NT_DIMS, preferred_element_type=F32)


def _dot_tn(a, b):
    return lax.dot_general(a, b, TN_DIMS, preferred_element_type=F32)


def _layer_norm(r, g, b):
    mu = jnp.mean(r, axis=-1, keepdims=True)
    c = r - mu
    var = jnp.mean(c * c, axis=-1, keepdims=True)
    return c * lax.rsqrt(var + LN_EPS) * g + b


def _rms_norm(x, g):
    return x * lax.rsqrt(jnp.mean(x * x, axis=-1, keepdims=True) + RMS_EPS) * g


def _ffn_steps(x, w13_ref, w2_ref, g, b, store, on_hidden=lambda: None):
    xb = x.astype(BF16)
    acts = []
    for c in range(D_FF // FFN_CHUNK):
        gate = _dot(xb, w13_ref[:, FFN_CHUNK * c:FFN_CHUNK * (c + 1)])
        up = _dot(xb, w13_ref[:, D_FF + FFN_CHUNK * c:D_FF + FFN_CHUNK * (c + 1)])
        acts.append((jax.nn.silu(gate) * up).astype(BF16))
        yield
    act = jnp.concatenate(acts, axis=1)
    on_hidden()
    ys = []
    for c in range(D_MODEL // FFN_CHUNK):
        ys.append(_dot(act, w2_ref[:, FFN_CHUNK * c:FFN_CHUNK * (c + 1)]))
        yield
    store(_layer_norm(ALPHA * x + 0.5 * jnp.concatenate(ys, axis=1), g, b))


def _ffn_stream(x_ref, o_ref, tile, w13_ref, w2_ref, g, b, flags, may_start):
    while not may_start():
        yield
    rows = slice(FFN_TILE * tile, FFN_TILE * (tile + 1))

    def store(y):
        o_ref[rows, :] = y

    def on_hidden():
        flags[tile] = True

    yield from _ffn_steps(x_ref[rows, :], w13_ref, w2_ref, g, b, store, on_hidden)


def _cast_stream(src_refs, dst_refs):
    for src, dst in zip(src_refs, dst_refs):
        dst[...] = src[...].astype(BF16)
        yield


def _ffn_ln_kernel(n_cast, x_ref, w13_ref, w2_ref, g_ref, b_ref, *refs):
    cast_src, o_ref, cast_dst = refs[:n_cast], refs[n_cast], refs[n_cast + 1:]
    n_tiles = x_ref.shape[0] // FFN_TILE
    g, b = g_ref[...], b_ref[...]
    flags = [False] * n_tiles
    streams = [_ffn_stream(x_ref, o_ref, t, w13_ref, w2_ref, g, b, flags,
                           (lambda: True) if t == 0 else (lambda t=t: flags[t - 1]))
               for t in range(n_tiles)]
    _round_robin(streams + [_cast_stream(cast_src, cast_dst)])


def _const_spec(shape):
    return pl.BlockSpec(shape, lambda *_: (0,) * len(shape), pipeline_mode=pl.Buffered(1))


def _slab_spec(shape, steps):
    rows, cols = shape
    count = max(c for c in range(1, steps + 1)
                if rows % c == 0 and (rows // c) % (2 * SUBLANES) == 0)
    return pl.BlockSpec((rows // count, cols), lambda i: (jnp.minimum(i, count - 1), 0))


def _ffn_ln(x, w13, w2, g, b, to_cast=()):
    n = x.shape[0]
    tm = FFN_TILE * FFN_TILES
    assert n % tm == 0
    steps = n // tm
    slabs = [_slab_spec(w.shape, steps) for w in to_cast]
    weights = (D_MODEL * 2 * D_FF + D_FF * D_MODEL) * BF16_BYTES
    blocks = (2 * tm * D_MODEL * F32_BYTES
              + sum(s.block_shape[0] * s.block_shape[1] for s in slabs) * (F32_BYTES + BF16_BYTES))
    values = 2 * FFN_TILE * ((D_MODEL + D_FF) * BF16_BYTES
                             + (2 * FFN_CHUNK + 2 * D_MODEL) * F32_BYTES)
    vmem = _vmem_limit(weights, blocks, values)
    out = pl.pallas_call(
        functools.partial(_ffn_ln_kernel, len(to_cast)),
        grid=(steps,),
        in_specs=[
            pl.BlockSpec((tm, D_MODEL), lambda i: (i, 0)),
            _const_spec((D_MODEL, 2 * D_FF)),
            _const_spec((D_FF, D_MODEL)),
            _const_spec((1, D_MODEL)),
            _const_spec((1, D_MODEL)),
        ] + slabs,
        out_specs=[pl.BlockSpec((tm, D_MODEL), lambda i: (i, 0))] + slabs,
        out_shape=[jax.ShapeDtypeStruct((n, D_MODEL), F32)]
        + [jax.ShapeDtypeStruct(w.shape, BF16) for w in to_cast],
        compiler_params=pltpu.CompilerParams(
            dimension_semantics=("arbitrary",), vmem_limit_bytes=vmem),
        name="ffn_ln",
    )(x, w13, w2, g, b, *to_cast)
    return tuple(out) if to_cast else out[0]


def _lower_bound(lbp):
    mx = jnp.max(lbp, axis=0, keepdims=True)
    e = jnp.exp(lbp - mx)
    return e[0:1] / jnp.sum(e, axis=0, keepdims=True)


def _split3(x):
    hi = x.astype(BF16)
    r = x - hi.astype(F32)
    mid = r.astype(BF16)
    lo = (r - mid.astype(F32)).astype(BF16)
    return hi, mid, lo


def _block_cumsum(tri, g):
    w = g.shape[1]
    parts = jnp.concatenate(_split3(g), axis=1)
    s = _dot(tri, parts)
    return s[:, :w] + s[:, w:2 * w] + s[:, 2 * w:]


def _tri_matrix(group):
    r = lax.broadcasted_iota(jnp.int32, (BLK, BLK), 0)
    c = lax.broadcasted_iota(jnp.int32, (BLK, BLK), 1)
    keep = (c <= r) & ((r ^ c) < group)
    return jnp.where(keep, 1.0, 0.0).astype(BF16)


def _pair_ref_rows(b, m):
    t, w = b.shape
    if m >= SUBLANES:
        b3 = b.reshape(t // (2 * m), 2 * m, w)
        return jnp.broadcast_to(b3[:, m - 1:m, :], b3.shape).reshape(t, w)
    b3 = b.reshape(t // SUBLANES, SUBLANES, w)
    sub = lax.broadcasted_iota(jnp.int32, b3.shape, 1)

    def row(i):
        return jnp.broadcast_to(b3[:, i:i + 1, :], b3.shape)

    if m == 4:
        out = row(3)
    elif m == 2:
        out = jnp.where(sub < 4, row(1), row(5))
    else:
        out = jnp.where(sub < 4, jnp.where(sub < 2, row(0), row(2)),
                        jnp.where(sub < 6, row(4), row(6)))
    return out.reshape(t, w)


def _level_operands(q, k, b, m):
    t, w = q.shape
    ref = _pair_ref_rows(b, m)
    if m >= SUBLANES:
        zero = jnp.zeros((m, w), F32)
        qs, ks = [], []
        for blk in range(t // m):
            rows = slice(blk * m, (blk + 1) * m)
            if blk % 2:
                qs.append(q[rows] * jnp.exp2(b[rows] - ref[rows]))
                ks.append(zero)
            else:
                qs.append(zero)
                ks.append(k[rows] * jnp.exp2(ref[rows] - b[rows]))
        return jnp.concatenate(qs, axis=0).astype(BF16), jnp.concatenate(ks, axis=0).astype(BF16)
    upper = (lax.broadcasted_iota(jnp.int32, q.shape, 0) & m) != 0
    e = jnp.exp2((b - ref) * jnp.where(upper, 1.0, -1.0))
    return (jnp.where(upper, q * e, 0.0).astype(BF16), jnp.where(upper, 0.0, k * e).astype(BF16))


def _intra_steps(q, k, b, max_m, xor, out):
    a = jnp.where(xor == 0, jnp.sum(q * k, axis=-1, keepdims=True), 0.0)
    yield
    m = 1
    while m <= max_m:
        qs, ks = _level_operands(q, k, b, m)
        yield
        p = _dot_nt(qs, ks)
        yield
        a = jnp.where(xor >= m, p, a)
        m *= 2
    if 2 * max_m < q.shape[0]:
        a = jnp.where(xor < 2 * max_m, a, 0.0)
    out["a"] = a


def _intra_scores(q, k, b, max_m, xor):
    out = {}
    for _ in _intra_steps(q, k, b, max_m, xor, out):
        pass
    return out["a"]


def _hgrn_gates(proj, lb):
    q = jax.nn.silu(proj[:, OFF_HQ:OFF_HF])
    f = lb + (1.0 - lb) * jax.nn.sigmoid(proj[:, OFF_HF:OFF_HI])
    return q, jnp.log2(f), 1.0 - f, proj[:, OFF_HI:OFF_HG]


def _head_queries(qa):
    low = lax.broadcasted_iota(jnp.int32, (qa.shape[0], LANES), 1) < HEAD_DIM
    out = []
    for pair in range(ATTN_HEADS // 2):
        piece = qa[:, LANES * pair:LANES * (pair + 1)] * ATTN_SCALE
        rolled = pltpu.roll(piece, HEAD_DIM, axis=1)
        for half in range(2):
            kv = (2 * pair + half) // GROUP
            src = piece if half == kv else rolled
            z = jnp.where(low, src, 0.0) if kv == 0 else jnp.where(low, 0.0, src)
            out.append(z.astype(BF16))
    return out


def _merge_heads(o):
    low = lax.broadcasted_iota(jnp.int32, o[0].shape, 1) < HEAD_DIM
    cols = []
    for pair in range(ATTN_HEADS // 2):
        kv = (2 * pair) // GROUP
        even, odd = o[2 * pair], o[2 * pair + 1]
        if kv == 0:
            odd = pltpu.roll(odd, HEAD_DIM, axis=1)
        else:
            even = pltpu.roll(even, HEAD_DIM, axis=1)
        cols.append(jnp.where(low, even, odd))
    return jnp.concatenate(cols, axis=1)


def _slope(head):
    return 2.0 ** (-8.0 * (head + 1) / ATTN_HEADS)


def _alibi_bias(limit):
    trow = lax.broadcasted_iota(jnp.int32, limit.shape, 0)
    ccol = lax.broadcasted_iota(jnp.int32, limit.shape, 1)
    dist = trow + WINDOW - ccol
    visible = dist.astype(jnp.uint32) < limit.astype(jnp.uint32)
    distf = dist.astype(F32)
    return [jnp.where(visible, -_slope(head) * distf, -jnp.inf) for head in range(ATTN_HEADS)]


def _attend_steps(qz, k_all, v_all, sink_ref, bias, out):
    tq = qz[0].shape[0]
    s_all = _dot_nt(jnp.concatenate(qz, axis=0), k_all.astype(BF16))
    yield
    ps, dens = [], []
    for head in range(ATTN_HEADS):
        sink = sink_ref[head]
        s = s_all[tq * head:tq * (head + 1)] + bias[head]
        mx = jnp.maximum(jnp.max(s, axis=-1, keepdims=True), sink)
        yield
        p = jnp.exp(s - mx)
        dens.append(jnp.sum(p, axis=-1, keepdims=True) + jnp.exp(sink - mx))
        ps.append(p.astype(BF16))
        yield
    o_all = _dot(jnp.concatenate(ps, axis=0), v_all.astype(BF16))
    yield
    out["oa"] = _merge_heads([o_all[tq * h:tq * (h + 1)] / dens[h] for h in range(ATTN_HEADS)])


def _attend_sample(qz, k_new, v_new, ck_ref, cv_ref, sink_ref):
    rows = qz[0].shape[0]
    k_new = k_new.astype(BF16)
    v_new = v_new.astype(BF16)
    sn_all = _dot_nt(jnp.concatenate(qz, axis=0), k_new)
    sc_seq = []
    for i in range(SEQ_PER_STEP):
        lhs = jnp.concatenate([z[TOK * i:TOK * (i + 1)] for z in qz], axis=0)
        sc_seq.append(_dot_nt(lhs, ck_ref[i].astype(BF16)))

    r = lax.broadcasted_iota(jnp.int32, (rows, rows), 0)
    c = lax.broadcasted_iota(jnp.int32, (rows, rows), 1)
    new_visible = ((r ^ c) < TOK) & (c <= r)
    new_dist = (r - c).astype(F32)
    tok = lax.broadcasted_iota(jnp.int32, (rows, WINDOW), 0) & (TOK - 1)
    key = lax.broadcasted_iota(jnp.int32, (rows, WINDOW), 1)
    old_visible = key > tok
    old_dist = (WINDOW + tok - key).astype(F32)

    pn, pc, dens = [], [], []
    for head in range(ATTN_HEADS):
        sink = sink_ref[head]
        slope = _slope(head)
        sn = sn_all[rows * head:rows * (head + 1)]
        sc = jnp.concatenate([s[TOK * head:TOK * (head + 1)] for s in sc_seq], axis=0)
        sn = jnp.where(new_visible, sn - slope * new_dist, -jnp.inf)
        sc = jnp.where(old_visible, sc - slope * old_dist, -jnp.inf)
        mx = jnp.maximum(jnp.maximum(jnp.max(sn, axis=-1, keepdims=True),
                                     jnp.max(sc, axis=-1, keepdims=True)), sink)
        en = jnp.exp(sn - mx)
        ec = jnp.exp(sc - mx)
        dens.append(jnp.sum(en, axis=-1, keepdims=True) + jnp.sum(ec, axis=-1, keepdims=True)
                    + jnp.exp(sink - mx))
        pn.append(en.astype(BF16))
        pc.append(ec.astype(BF16))

    on_all = _dot(jnp.concatenate(pn, axis=0), v_new)
    oc_seq = []
    for i in range(SEQ_PER_STEP):
        lhs = jnp.concatenate([p[TOK * i:TOK * (i + 1)] for p in pc], axis=0)
        oc_seq.append(_dot(lhs, cv_ref[i].astype(BF16)))
    o = []
    for head in range(ATTN_HEADS):
        oc = jnp.concatenate([s[TOK * head:TOK * (head + 1)] for s in oc_seq], axis=0)
        o.append((on_all[rows * head:rows * (head + 1)] + oc) / dens[head])
    return _merge_heads(o)


def _mix_out(x, oa, oh, proj, w_out_ref, ang_ref, hng_ref, g2_ref, b2_ref):
    oa = _rms_norm(oa, ang_ref[...])
    hng = hng_ref[...]
    gate = jax.nn.silu(proj[:, OFF_HG:])
    ohn = [_rms_norm(oh[h], hng[:, HGRN_DV * h:HGRN_DV * (h + 1)]) for h in range(HGRN_HEADS)]
    mix = jnp.concatenate([oa, jnp.concatenate(ohn, axis=1) * gate], axis=1).astype(BF16)
    y = _dot(mix, w_out_ref[...])
    return _layer_norm(ALPHA * x + y, g2_ref[...], b2_ref[...])


def _proj_stream(ctx, x, w_in_ref, may_start):
    while not may_start():
        yield
    xb = x.astype(BF16)
    chunks = []
    for c in range(D_IN // PROJ_CHUNK):
        chunks.append(_dot(xb, w_in_ref[:, PROJ_CHUNK * c:PROJ_CHUNK * (c + 1)]))
        yield
    ctx["proj"] = jnp.concatenate(chunks, axis=1)


def _prepare_stream(ctx, lb, tri):
    while "proj" not in ctx:
        yield
    proj = ctx["proj"]
    ctx["qz"] = _head_queries(proj[:, :ATTN_WIDTH])
    yield
    ctx["q"], logf, ctx["kin"], ctx["vin"] = _hgrn_gates(proj, lb)
    yield
    ctx["b"] = [_block_cumsum(tri, logf[BLK * blk:BLK * (blk + 1)]) for blk in range(PROMPT_BLOCKS)]
    ctx["ready"] = True


def _attention_stream(ctx, blk, prev_kv, bias, sink_ref):
    while "ready" not in ctx:
        yield
    rows = slice(BLK * blk, BLK * (blk + 1))
    k_prev, v_prev = prev_kv()
    k_all = jnp.concatenate([k_prev, ctx["proj"][rows, OFF_K:OFF_V]], axis=0)
    v_all = jnp.concatenate([v_prev, ctx["proj"][rows, OFF_V:OFF_HQ]], axis=0)
    out = {}
    yield from _attend_steps([z[rows] for z in ctx["qz"]], k_all, v_all, sink_ref, bias, out)
    ctx["oa"][blk] = out["oa"]


def _hgrn_stream(ctx, blk, h, state, turn, xor):
    while "ready" not in ctx:
        yield
    rows = slice(BLK * blk, BLK * (blk + 1))
    sl = slice(HGRN_DK * h, HGRN_DK * (h + 1))
    qh, kh, bh = ctx["q"][rows, sl], ctx["kin"][rows, sl], ctx["b"][blk][:, sl]
    vh = ctx["vin"][rows, sl].astype(BF16)
    scores = {}
    yield from _intra_steps(qh, kh, bh, BLK // 2, xor, scores)
    o_intra = _dot(scores["a"].astype(BF16), vh)
    yield
    b_last = bh[BLK - 1:BLK, :]
    qe = (qh * jnp.exp2(bh)).astype(BF16)
    kd = (kh * jnp.exp2(b_last - bh)).astype(BF16)
    yield
    while state["turn"][h] != turn:
        yield
    st = state["st"][h]
    o_inter = _dot_nt(qe, st.astype(BF16))
    update = _dot_tn(vh, kd)
    yield
    state["st"][h] = st * jnp.exp2(b_last) + update
    state["turn"][h] = turn + 1
    ctx["oh"][blk][h] = o_intra + o_inter


def _mix_out_stream(ctx, x, y_store, w_out_ref, ang_ref, hng_ref, g2_ref, b2_ref):
    while any(o is None for o in ctx["oa"]) or any(o is None for blk in ctx["oh"] for o in blk):
        yield
    oa = jnp.concatenate(ctx["oa"], axis=0)
    oh = [jnp.concatenate([ctx["oh"][blk][h] for blk in range(PROMPT_BLOCKS)], axis=0)
          for h in range(HGRN_HEADS)]
    y_store(_mix_out(x, oa, oh, ctx["proj"], w_out_ref, ang_ref, hng_ref, g2_ref, b2_ref))


def _mixer_prompt_kernel(sink_ref, x_ref, w_in_ref, w_out_ref, lbp_ref, ang_ref, hng_ref, g2_ref,
                         b2_ref, y_ref, s_out_ref, k_out_ref, v_out_ref, st_ref, kv_ref):
    j = pl.program_id(1)

    @pl.when(j == 0)
    def _():
        st_ref[...] = jnp.zeros_like(st_ref)
        kv_ref[...] = jnp.zeros_like(kv_ref)

    lb = _lower_bound(lbp_ref[...])
    tri = _tri_matrix(BLK)
    xor = (lax.broadcasted_iota(jnp.int32, (BLK, BLK), 0)
           ^ lax.broadcasted_iota(jnp.int32, (BLK, BLK), 1))
    trow = lax.broadcasted_iota(jnp.int32, (BLK, 2 * WINDOW), 0)
    full_bias = _alibi_bias(jnp.full(trow.shape, WINDOW, jnp.int32))

    state = {"st": [st_ref[h] for h in range(HGRN_HEADS)], "turn": [0] * HGRN_HEADS}
    ctxs = [{"oa": [None] * PROMPT_BLOCKS,
             "oh": [[None] * HGRN_HEADS for _ in range(PROMPT_BLOCKS)]} for _ in range(PROMPT_TILES)]

    def kv_block(tile, blk):
        rows = slice(BLK * blk, BLK * (blk + 1))
        proj = ctxs[tile]["proj"]
        return proj[rows, OFF_K:OFF_V], proj[rows, OFF_V:OFF_HQ]

    streams = []
    for tile in range(PROMPT_TILES):
        ctx = ctxs[tile]
        x = x_ref[tile]
        may_start = (lambda: True) if tile == 0 else (lambda prev=ctxs[tile - 1]: "ready" in prev)
        streams.append(_proj_stream(ctx, x, w_in_ref, may_start))
        streams.append(_prepare_stream(ctx, lb, tri))
        for blk in range(PROMPT_BLOCKS):
            if tile == 0 and blk == 0:
                prev_kv = lambda: (kv_ref[0], kv_ref[1])
                bias = _alibi_bias(jnp.where(j > 0, WINDOW, trow + 1))
            else:
                g = tile * PROMPT_BLOCKS + blk - 1
                prev_kv = lambda g=g: kv_block(g // PROMPT_BLOCKS, g % PROMPT_BLOCKS)
                bias = full_bias
            streams.append(_attention_stream(ctx, blk, prev_kv, bias, sink_ref))
            for h in range(HGRN_HEADS):
                streams.append(_hgrn_stream(ctx, blk, h, state, tile * PROMPT_BLOCKS + blk, xor))

        def y_store(y, tile=tile):
            y_ref[tile] = y

        streams.append(_mix_out_stream(ctx, x, y_store, w_out_ref, ang_ref, hng_ref, g2_ref, b2_ref))
    _round_robin(streams)

    k_last, v_last = kv_block(PROMPT_TILES - 1, PROMPT_BLOCKS - 1)
    kv_ref[0] = k_last
    kv_ref[1] = v_last
    for h in range(HGRN_HEADS):
        st_ref[h] = state["st"][h]

    @pl.when(j == pl.num_programs(1) - 1)
    def _():
        k_out_ref[0] = k_last
        v_out_ref[0] = v_last
        for h in range(HGRN_HEADS):
            s_out_ref[0, h] = state["st"][h].T


MIXER_WEIGHT_BYTES = (D_MODEL * D_IN + D_MIX * D_MODEL) * BF16_BYTES
STATE_BYTES = HGRN_HEADS * HGRN_DK * HGRN_DV * F32_BYTES
WINDOW_BYTES = WINDOW * KV_WIDTH * F32_BYTES


def _mixer_value_bytes(rows, keys):
    per_row = ((D_IN + 4 * HGRN_KW + 2 * D_MODEL) * F32_BYTES
               + (ATTN_HEADS * LANES + D_MIX) * BF16_BYTES)
    scores = ATTN_HEADS * BLK * keys * (F32_BYTES + BF16_BYTES)
    return rows * per_row + (rows // BLK) * scores


def _mixer_prompt(x, sink, w_in, w_out, lbp, ang, hng, g2, b2):
    nb, seq, _ = x.shape
    tile_rows = PROMPT_BLOCKS * BLK
    steps_per_seq = seq // (PROMPT_TILES * tile_rows)
    n_tiles = nb * seq // tile_rows
    xt = x.reshape(n_tiles, tile_rows, D_MODEL)
    blocks = (2 * PROMPT_TILES * tile_rows * D_MODEL * F32_BYTES
              + STATE_BYTES + 2 * WINDOW_BYTES)
    vmem = _vmem_limit(MIXER_WEIGHT_BYTES + STATE_BYTES + 2 * WINDOW_BYTES, blocks,
                       _mixer_value_bytes(2 * tile_rows, 2 * WINDOW))
    tile_spec = pl.BlockSpec((PROMPT_TILES, tile_rows, D_MODEL),
                             lambda b, j: (b * steps_per_seq + j, 0, 0))
    y, s, k, v = pl.pallas_call(
        _mixer_prompt_kernel,
        grid=(nb, steps_per_seq),
        in_specs=[
            pl.BlockSpec(memory_space=pltpu.SMEM),
            tile_spec,
            _const_spec((D_MODEL, D_IN)),
            _const_spec((D_MIX, D_MODEL)),
            _const_spec((DEPTH + 1, HGRN_KW)),
            _const_spec((1, ATTN_WIDTH)),
            _const_spec((1, HGRN_VW)),
            _const_spec((1, D_MODEL)),
            _const_spec((1, D_MODEL)),
        ],
        out_specs=[
            tile_spec,
            pl.BlockSpec((1, HGRN_HEADS, HGRN_DK, HGRN_DV), lambda b, j: (b, 0, 0, 0)),
            pl.BlockSpec((1, WINDOW, KV_WIDTH), lambda b, j: (b, 0, 0)),
            pl.BlockSpec((1, WINDOW, KV_WIDTH), lambda b, j: (b, 0, 0)),
        ],
        out_shape=[
            jax.ShapeDtypeStruct((n_tiles, tile_rows, D_MODEL), F32),
            jax.ShapeDtypeStruct((nb, HGRN_HEADS, HGRN_DK, HGRN_DV), F32),
            jax.ShapeDtypeStruct((nb, WINDOW, KV_WIDTH), F32),
            jax.ShapeDtypeStruct((nb, WINDOW, KV_WIDTH), F32),
        ],
        scratch_shapes=[
            pltpu.VMEM((HGRN_HEADS, HGRN_DV, HGRN_DK), F32),
            pltpu.VMEM((2, WINDOW, KV_WIDTH), F32),
        ],
        compiler_params=pltpu.CompilerParams(
            dimension_semantics=("arbitrary", "arbitrary"), vmem_limit_bytes=vmem),
        name="mixer_prompt",
    )(sink, xt, w_in, w_out, lbp, ang, hng, g2, b2)
    return y.reshape(nb, seq, D_MODEL), s, k, v


def _mixer_sample_kernel(sink_ref, x_ref, s0_ref, ck_ref, cv_ref, w_in_ref, w_out_ref, lbp_ref,
                         ang_ref, hng_ref, g2_ref, b2_ref, y_ref, s_out_ref, k_out_ref, v_out_ref):
    x = x_ref[...]
    proj = _dot(x.astype(BF16), w_in_ref[...])
    k_new = proj[:, OFF_K:OFF_V]
    v_new = proj[:, OFF_V:OFF_HQ]

    oa = _attend_sample(_head_queries(proj[:, :ATTN_WIDTH]), k_new, v_new, ck_ref, cv_ref, sink_ref)
    for i in range(SEQ_PER_STEP):
        rows = slice(TOK * i, TOK * (i + 1))
        k_out_ref[i] = jnp.concatenate([ck_ref[i, TOK:, :], k_new[rows]], axis=0)
        v_out_ref[i] = jnp.concatenate([cv_ref[i, TOK:, :], v_new[rows]], axis=0)

    q, logf, kin, vin = _hgrn_gates(proj, _lower_bound(lbp_ref[...]))
    b = _block_cumsum(_tri_matrix(TOK), logf)
    xor = (lax.broadcasted_iota(jnp.int32, (BLK, BLK), 0)
           ^ lax.broadcasted_iota(jnp.int32, (BLK, BLK), 1))
    lane = lax.broadcasted_iota(jnp.int32, (HGRN_DK, BLK), 1)
    oh = []
    for h in range(HGRN_HEADS):
        sl = slice(HGRN_DK * h, HGRN_DK * (h + 1))
        qh, kh, bh = q[:, sl], kin[:, sl], b[:, sl]
        vh = vin[:, sl].astype(BF16)
        a = _intra_scores(qh, kh, bh, TOK // 2, xor)
        o_intra = _dot(a.astype(BF16), vh)
        qe = (qh * jnp.exp2(bh)).astype(BF16)
        b3 = bh.reshape(SEQ_PER_STEP, TOK, HGRN_DK)
        b_last = jnp.broadcast_to(b3[:, TOK - 1:TOK, :], b3.shape).reshape(BLK, HGRN_DK)
        kd_t = (kh * jnp.exp2(b_last - bh)).T
        dec_t = jnp.exp2(b_last).T
        o_rows = []
        for i in range(SEQ_PER_STEP):
            s0 = s0_ref[i, h]
            o_rows.append(_dot(qe[TOK * i:TOK * (i + 1)], s0.astype(BF16)))
            kd_i = jnp.where((lane >= TOK * i) & (lane < TOK * (i + 1)), kd_t, 0.0).astype(BF16)
            s_out_ref[i, h] = s0 * dec_t[:, TOK * i:TOK * i + 1] + _dot(kd_i, vh)
        oh.append(o_intra + jnp.concatenate(o_rows, axis=0))

    y_ref[...] = _mix_out(x, oa, oh, proj, w_out_ref, ang_ref, hng_ref, g2_ref, b2_ref)


def _mixer_sample(x, s0, ck, cv, sink, w_in, w_out, lbp, ang, hng, g2, b2):
    n = x.shape[0]
    nseq = s0.shape[0]
    state_spec = pl.BlockSpec((SEQ_PER_STEP, HGRN_HEADS, HGRN_DK, HGRN_DV), lambda i: (i, 0, 0, 0))
    cache_spec = pl.BlockSpec((SEQ_PER_STEP, WINDOW, KV_WIDTH), lambda i: (i, 0, 0))
    blocks = 2 * BLK * D_MODEL * F32_BYTES + 2 * SEQ_PER_STEP * (STATE_BYTES + 2 * WINDOW_BYTES)
    values = (_mixer_value_bytes(BLK, WINDOW + BLK)
              + HGRN_HEADS * 2 * HGRN_DK * BLK * F32_BYTES)
    vmem = _vmem_limit(MIXER_WEIGHT_BYTES, blocks, values)
    return pl.pallas_call(
        _mixer_sample_kernel,
        grid=(n // BLK,),
        in_specs=[
            pl.BlockSpec(memory_space=pltpu.SMEM),
            pl.BlockSpec((BLK, D_MODEL), lambda i: (i, 0)),
            state_spec, cache_spec, cache_spec,
            _const_spec((D_MODEL, D_IN)),
            _const_spec((D_MIX, D_MODEL)),
            _const_spec((DEPTH + 1, HGRN_KW)),
            _const_spec((1, ATTN_WIDTH)),
            _const_spec((1, HGRN_VW)),
            _const_spec((1, D_MODEL)),
            _const_spec((1, D_MODEL)),
        ],
        out_specs=[
            pl.BlockSpec((BLK, D_MODEL), lambda i: (i, 0)),
            state_spec, cache_spec, cache_spec,
        ],
        out_shape=[
            jax.ShapeDtypeStruct((n, D_MODEL), F32),
            jax.ShapeDtypeStruct((nseq, HGRN_HEADS, HGRN_DK, HGRN_DV), F32),
            jax.ShapeDtypeStruct((nseq, WINDOW, KV_WIDTH), F32),
            jax.ShapeDtypeStruct((nseq, WINDOW, KV_WIDTH), F32),
        ],
        compiler_params=pltpu.CompilerParams(
            dimension_semantics=("arbitrary",), vmem_limit_bytes=vmem),
        name="mixer_sample",
    )(sink, x, s0, ck, cv, w_in, w_out, lbp, ang, hng, g2, b2)


def kernel(x_prompt, x_sample, state_hgrn, cache_win_k, cache_win_v, ln1_g, ln1_b, ffn1_w13,
           ffn1_w2, w_in, lb_param, attn_sink, attn_norm_g, hgrn_norm_g, w_out, ln2_g, ln2_b,
           ffn2_w13, ffn2_w2, ln3_g, ln3_b):
    nb, seq, d = x_prompt.shape
    ns, tn, _ = x_sample.shape
    assert state_hgrn.shape[0] == DEPTH == 1 and d == D_MODEL
    assert seq % (PROMPT_TILES * PROMPT_BLOCKS * BLK) == 0 and tn == TOK and (ns * tn) % BLK == 0
    assert cache_win_k.shape[2] == WINDOW

    w13_1 = ffn1_w13[0].astype(BF16)
    w2_1 = ffn1_w2[0].astype(BF16)
    sink = attn_sink[0]
    lbp = lb_param.astype(F32)

    xp = x_prompt.reshape(nb * seq, d)
    xs = x_sample.reshape(ns * tn, d)

    xs = _ffn_ln(xs, w13_1, w2_1, ln1_g, ln1_b)
    xp, w13_2, w2_2, w_in_b, w_out_b = _ffn_ln(
        xp, w13_1, w2_1, ln1_g, ln1_b, to_cast=(ffn2_w13[0], ffn2_w2[0], w_in[0], w_out[0]))
    mix_params = (sink, w_in_b, w_out_b, lbp, attn_norm_g, hgrn_norm_g, ln2_g, ln2_b)

    xp, sp, kp, vp = _mixer_prompt(xp.reshape(nb, seq, d), *mix_params)
    xs, ss, ks, vs = _mixer_sample(
        xs, state_hgrn[0], cache_win_k[0].reshape(ns, WINDOW, KV_WIDTH),
        cache_win_v[0].reshape(ns, WINDOW, KV_WIDTH), *mix_params)

    xp = _ffn_ln(xp.reshape(nb * seq, d), w13_2, w2_2, ln3_g, ln3_b)
    xs = _ffn_ln(xs, w13_2, w2_2, ln3_g, ln3_b)

    cache_tail = (WINDOW, ATTN_KV_HEADS, HEAD_DIM)
    return (xp.reshape(nb, seq, d), xs.reshape(ns, tn, d),
            sp[None], kp.reshape((DEPTH, nb) + cache_tail), vp.reshape((DEPTH, nb) + cache_tail),
            ss[None], ks.reshape((DEPTH, ns) + cache_tail), vs.reshape((DEPTH, ns) + cache_tail))
```

```python
import functools

import jax
import jax.numpy as jnp
from jax import lax
from jax.experimental import pallas as pl
from jax.experimental.pallas import tpu as pltpu

F32 = jnp.float32
BF16 = jnp.bfloat16

D_MODEL = 1024
DEPTH = 1
ATTN_HEADS = 8
ATTN_KV_HEADS = 2
GROUP = ATTN_HEADS // ATTN_KV_HEADS
HEAD_DIM = 64
ATTN_WIDTH = ATTN_HEADS * HEAD_DIM
KV_WIDTH = ATTN_KV_HEADS * HEAD_DIM
WINDOW = 128
ATTN_SCALE = HEAD_DIM ** -0.5
HGRN_HEADS = 4
HGRN_DK = 128
HGRN_DV = 128
HGRN_KW = HGRN_HEADS * HGRN_DK
HGRN_VW = HGRN_HEADS * HGRN_DV
D_MIX = ATTN_WIDTH + HGRN_VW
D_IN = ATTN_WIDTH + 2 * KV_WIDTH + 2 * HGRN_KW + 2 * HGRN_VW
D_FF = 2816
ALPHA = (2.0 * DEPTH) ** 0.25
LN_EPS = 1e-5
RMS_EPS = 1e-6

OFF_K = ATTN_WIDTH
OFF_V = OFF_K + KV_WIDTH
OFF_HQ = OFF_V + KV_WIDTH
OFF_HF = OFF_HQ + HGRN_KW
OFF_HI = OFF_HF + HGRN_KW
OFF_HG = OFF_HI + HGRN_VW

LANES = 128
SUBLANES = 8
BF16_TILE_ROWS = 2 * SUBLANES
MXU_WIDTH = 256
BLK = 128
PROMPT_BLOCKS = 2
PROMPT_TILES = 2
PROJ_CHUNK = MXU_WIDTH
FFN_TILE = 256
FFN_TILES = 4
FFN_CHUNK = MXU_WIDTH
TOK = SUBLANES
SEQ_PER_STEP = BLK // TOK
V7X_VMEM_BYTES = 64 * 1024 * 1024
F32_BYTES = 4
BF16_BYTES = 2

NT_DIMS = (((1,), (1,)), ((), ()))
TN_DIMS = (((0,), (0,)), ((), ()))


def _vmem_limit(resident_bytes, streamed_bytes, value_bytes):
    need = resident_bytes + 2 * streamed_bytes + value_bytes
    assert need <= V7X_VMEM_BYTES, need
    return need


_DONE = object()


def _round_robin(streams):
    live = list(streams)
    rounds = 0
    while live:
        live = [g for g in live if next(g, _DONE) is not _DONE]
        rounds += 1
        assert rounds < 100000, "a stream waits for something no other stream produces"


def _dot(a, b):
    return jnp.dot(a, b, preferred_element_type=F32)


def _dot_nt(a, b):
    return lax.dot_general(a, b, NT_DIMS, preferred_element_type=F32)


def _dot_tn(a, b):
    return lax.dot_general(a, b, TN_DIMS, preferred_element_type=F32)


def _layer_norm(r, g, b):
    mu = jnp.mean(r, axis=-1, keepdims=True)
    c = r - mu
    var = jnp.mean(c * c, axis=-1, keepdims=True)
    return c * lax.rsqrt(var + LN_EPS) * g + b


def _rms_norm(x, g):
    return x * lax.rsqrt(jnp.mean(x * x, axis=-1, keepdims=True) + RMS_EPS) * g


def _ffn_steps(x, w13_ref, w2_ref, g, b, store, on_hidden=lambda: None):
    xb = x.astype(BF16)
    acts = []
    for c in range(D_FF // FFN_CHUNK):
        gate = _dot(xb, w13_ref[:, FFN_CHUNK * c:FFN_CHUNK * (c + 1)])
        up = _dot(xb, w13_ref[:, D_FF + FFN_CHUNK * c:D_FF + FFN_CHUNK * (c + 1)])
        acts.append((jax.nn.silu(gate) * up).astype(BF16))
        yield
    act = jnp.concatenate(acts, axis=1)
    on_hidden()
    ys = []
    for c in range(D_MODEL // FFN_CHUNK):
        ys.append(_dot(act, w2_ref[:, FFN_CHUNK * c:FFN_CHUNK * (c + 1)]))
        yield
    store(_layer_norm(ALPHA * x + 0.5 * jnp.concatenate(ys, axis=1), g, b))


def _ffn_stream(x_ref, o_ref, tile, w13_ref, w2_ref, g, b, flags, may_start):
    while not may_start():
        yield
    rows = slice(FFN_TILE * tile, FFN_TILE * (tile + 1))

    def store(y):
        o_ref[rows, :] = y

    def on_hidden():
        flags[tile] = True

    yield from _ffn_steps(x_ref[rows, :], w13_ref, w2_ref, g, b, store, on_hidden)


def _cast_stream(src_refs, dst_refs):
    for src, dst in zip(src_refs, dst_refs):
        dst[...] = src[...].astype(BF16)
        yield


def _ffn_ln_kernel(n_cast, x_ref, w13_ref, w2_ref, g_ref, b_ref, *refs):
    cast_src, o_ref, cast_dst = refs[:n_cast], refs[n_cast], refs[n_cast + 1:]
    n_tiles = x_ref.shape[0] // FFN_TILE
    g, b = g_ref[...], b_ref[...]
    flags = [False] * n_tiles
    streams = [_ffn_stream(x_ref, o_ref, t, w13_ref, w2_ref, g, b, flags,
                           (lambda: True) if t == 0 else (lambda t=t: flags[t - 1]))
               for t in range(n_tiles)]
    _round_robin(streams + [_cast_stream(cast_src, cast_dst)])


def _const_spec(shape):
    return pl.BlockSpec(shape, lambda *_: (0,) * len(shape), pipeline_mode=pl.Buffered(1))


def _slab_spec(shape, steps):
    rows, cols = shape
    count = max(c for c in range(1, steps + 1)
                if rows % c == 0 and (rows // c) % BF16_TILE_ROWS == 0)
    return pl.BlockSpec((rows // count, cols), lambda i: (jnp.minimum(i, count - 1), 0))


def _ffn_ln(x, w13, w2, g, b, to_cast=()):
    n = x.shape[0]
    tm = FFN_TILE * FFN_TILES
    assert n % tm == 0
    steps = n // tm
    slabs = [_slab_spec(w.shape, steps) for w in to_cast]
    weights = (D_MODEL * 2 * D_FF + D_FF * D_MODEL) * BF16_BYTES
    blocks = (2 * tm * D_MODEL * F32_BYTES
              + sum(s.block_shape[0] * s.block_shape[1] for s in slabs) * (F32_BYTES + BF16_BYTES))
    values = 2 * FFN_TILE * ((D_MODEL + D_FF) * BF16_BYTES
                             + (2 * FFN_CHUNK + 2 * D_MODEL) * F32_BYTES)
    vmem = _vmem_limit(weights, blocks, values)
    out = pl.pallas_call(
        functools.partial(_ffn_ln_kernel, len(to_cast)),
        grid=(steps,),
        in_specs=[
            pl.BlockSpec((tm, D_MODEL), lambda i: (i, 0)),
            _const_spec((D_MODEL, 2 * D_FF)),
            _const_spec((D_FF, D_MODEL)),
            _const_spec((1, D_MODEL)),
            _const_spec((1, D_MODEL)),
        ] + slabs,
        out_specs=[pl.BlockSpec((tm, D_MODEL), lambda i: (i, 0))] + slabs,
        out_shape=[jax.ShapeDtypeStruct((n, D_MODEL), F32)]
        + [jax.ShapeDtypeStruct(w.shape, BF16) for w in to_cast],
        compiler_params=pltpu.CompilerParams(
            dimension_semantics=("arbitrary",), vmem_limit_bytes=vmem),
        name="ffn_ln",
    )(x, w13, w2, g, b, *to_cast)
    return tuple(out) if to_cast else out[0]


def _lower_bound(lbp):
    mx = jnp.max(lbp, axis=0, keepdims=True)
    e = jnp.exp(lbp - mx)
    return e[0:1] / jnp.sum(e, axis=0, keepdims=True)


def _split3(x):
    hi = x.astype(BF16)
    r = x - hi.astype(F32)
    mid = r.astype(BF16)
    lo = (r - mid.astype(F32)).astype(BF16)
    return hi, mid, lo


def _block_cumsum(tri, g):
    w = g.shape[1]
    parts = jnp.concatenate(_split3(g), axis=1)
    s = _dot(tri, parts)
    return s[:, :w] + s[:, w:2 * w] + s[:, 2 * w:]


def _tri_matrix(group):
    r = lax.broadcasted_iota(jnp.int32, (BLK, BLK), 0)
    c = lax.broadcasted_iota(jnp.int32, (BLK, BLK), 1)
    keep = (c <= r) & ((r ^ c) < group)
    return jnp.where(keep, 1.0, 0.0).astype(BF16)


def _pair_ref_rows(b, m):
    t, w = b.shape
    if m >= SUBLANES:
        b3 = b.reshape(t // (2 * m), 2 * m, w)
        return jnp.broadcast_to(b3[:, m - 1:m, :], b3.shape).reshape(t, w)
    b3 = b.reshape(t // SUBLANES, SUBLANES, w)
    sub = lax.broadcasted_iota(jnp.int32, b3.shape, 1)

    def row(i):
        return jnp.broadcast_to(b3[:, i:i + 1, :], b3.shape)

    if m == 4:
        out = row(3)
    elif m == 2:
        out = jnp.where(sub < 4, row(1), row(5))
    else:
        out = jnp.where(sub < 4, jnp.where(sub < 2, row(0), row(2)),
                        jnp.where(sub < 6, row(4), row(6)))
    return out.reshape(t, w)


def _level_operands(q, k, b, m):
    t, w = q.shape
    ref = _pair_ref_rows(b, m)
    if m >= SUBLANES:
        zero = jnp.zeros((m, w), F32)
        qs, ks = [], []
        for blk in range(t // m):
            rows = slice(blk * m, (blk + 1) * m)
            if blk % 2:
                qs.append(q[rows] * jnp.exp2(b[rows] - ref[rows]))
                ks.append(zero)
            else:
                qs.append(zero)
                ks.append(k[rows] * jnp.exp2(ref[rows] - b[rows]))
        return jnp.concatenate(qs, axis=0).astype(BF16), jnp.concatenate(ks, axis=0).astype(BF16)
    upper = (lax.broadcasted_iota(jnp.int32, q.shape, 0) & m) != 0
    e = jnp.exp2((b - ref) * jnp.where(upper, 1.0, -1.0))
    return (jnp.where(upper, q * e, 0.0).astype(BF16), jnp.where(upper, 0.0, k * e).astype(BF16))


def _intra_steps(q, k, b, max_m, xor, out):
    a = jnp.where(xor == 0, jnp.sum(q * k, axis=-1, keepdims=True), 0.0)
    yield
    m = 1
    while m <= max_m:
        qs, ks = _level_operands(q, k, b, m)
        yield
        p = _dot_nt(qs, ks)
        yield
        a = jnp.where(xor >= m, p, a)
        m *= 2
    if 2 * max_m < q.shape[0]:
        a = jnp.where(xor < 2 * max_m, a, 0.0)
    out["a"] = a


def _intra_scores(q, k, b, max_m, xor):
    out = {}
    for _ in _intra_steps(q, k, b, max_m, xor, out):
        pass
    return out["a"]


def _hgrn_gates(proj, lb):
    q = jax.nn.silu(proj[:, OFF_HQ:OFF_HF])
    f = lb + (1.0 - lb) * jax.nn.sigmoid(proj[:, OFF_HF:OFF_HI])
    return q, jnp.log2(f), 1.0 - f, proj[:, OFF_HI:OFF_HG]


def _head_queries(qa):
    low = lax.broadcasted_iota(jnp.int32, (qa.shape[0], LANES), 1) < HEAD_DIM
    out = []
    for pair in range(ATTN_HEADS // 2):
        piece = qa[:, LANES * pair:LANES * (pair + 1)] * ATTN_SCALE
        rolled = pltpu.roll(piece, HEAD_DIM, axis=1)
        for half in range(2):
            kv = (2 * pair + half) // GROUP
            src = piece if half == kv else rolled
            z = jnp.where(low, src, 0.0) if kv == 0 else jnp.where(low, 0.0, src)
            out.append(z.astype(BF16))
    return out


def _merge_heads(o):
    low = lax.broadcasted_iota(jnp.int32, o[0].shape, 1) < HEAD_DIM
    cols = []
    for pair in range(ATTN_HEADS // 2):
        kv = (2 * pair) // GROUP
        even, odd = o[2 * pair], o[2 * pair + 1]
        if kv == 0:
            odd = pltpu.roll(odd, HEAD_DIM, axis=1)
        else:
            even = pltpu.roll(even, HEAD_DIM, axis=1)
        cols.append(jnp.where(low, even, odd))
    return jnp.concatenate(cols, axis=1)


def _slope(head):
    return 2.0 ** (-8.0 * (head + 1) / ATTN_HEADS)


def _alibi_bias(limit):
    trow = lax.broadcasted_iota(jnp.int32, limit.shape, 0)
    ccol = lax.broadcasted_iota(jnp.int32, limit.shape, 1)
    dist = trow + WINDOW - ccol
    visible = dist.astype(jnp.uint32) < limit.astype(jnp.uint32)
    distf = dist.astype(F32)
    return [jnp.where(visible, -_slope(head) * distf, -jnp.inf) for head in range(ATTN_HEADS)]


def _attend_steps(qz, k_all, v_all, sink_ref, bias, out):
    tq = qz[0].shape[0]
    s_all = _dot_nt(jnp.concatenate(qz, axis=0), k_all.astype(BF16))
    yield
    ps, dens = [], []
    for head in range(ATTN_HEADS):
        sink = sink_ref[head]
        s = s_all[tq * head:tq * (head + 1)] + bias[head]
        mx = jnp.maximum(jnp.max(s, axis=-1, keepdims=True), sink)
        yield
        p = jnp.exp(s - mx)
        dens.append(jnp.sum(p, axis=-1, keepdims=True) + jnp.exp(sink - mx))
        ps.append(p.astype(BF16))
        yield
    o_all = _dot(jnp.concatenate(ps, axis=0), v_all.astype(BF16))
    yield
    out["oa"] = _merge_heads([o_all[tq * h:tq * (h + 1)] / dens[h] for h in range(ATTN_HEADS)])


def _attend_sample(qz, k_new, v_new, ck_ref, cv_ref, sink_ref):
    rows = qz[0].shape[0]
    k_new = k_new.astype(BF16)
    v_new = v_new.astype(BF16)
    sn_all = _dot_nt(jnp.concatenate(qz, axis=0), k_new)
    sc_seq = []
    for i in range(SEQ_PER_STEP):
        lhs = jnp.concatenate([z[TOK * i:TOK * (i + 1)] for z in qz], axis=0)
        sc_seq.append(_dot_nt(lhs, ck_ref[i].astype(BF16)))

    r = lax.broadcasted_iota(jnp.int32, (rows, rows), 0)
    c = lax.broadcasted_iota(jnp.int32, (rows, rows), 1)
    new_visible = ((r ^ c) < TOK) & (c <= r)
    new_dist = (r - c).astype(F32)
    tok = lax.broadcasted_iota(jnp.int32, (rows, WINDOW), 0) & (TOK - 1)
    key = lax.broadcasted_iota(jnp.int32, (rows, WINDOW), 1)
    old_visible = key > tok
    old_dist = (WINDOW + tok - key).astype(F32)

    pn, pc, dens = [], [], []
    for head in range(ATTN_HEADS):
        sink = sink_ref[head]
        slope = _slope(head)
        sn = sn_all[rows * head:rows * (head + 1)]
        sc = jnp.concatenate([s[TOK * head:TOK * (head + 1)] for s in sc_seq], axis=0)
        sn = jnp.where(new_visible, sn - slope * new_dist, -jnp.inf)
        sc = jnp.where(old_visible, sc - slope * old_dist, -jnp.inf)
        mx = jnp.maximum(jnp.maximum(jnp.max(sn, axis=-1, keepdims=True),
                                     jnp.max(sc, axis=-1, keepdims=True)), sink)
        en = jnp.exp(sn - mx)
        ec = jnp.exp(sc - mx)
        dens.append(jnp.sum(en, axis=-1, keepdims=True) + jnp.sum(ec, axis=-1, keepdims=True)
                    + jnp.exp(sink - mx))
        pn.append(en.astype(BF16))
        pc.append(ec.astype(BF16))

    on_all = _dot(jnp.concatenate(pn, axis=0), v_new)
    oc_seq = []
    for i in range(SEQ_PER_STEP):
        lhs = jnp.concatenate([p[TOK * i:TOK * (i + 1)] for p in pc], axis=0)
        oc_seq.append(_dot(lhs, cv_ref[i].astype(BF16)))
    o = []
    for head in range(ATTN_HEADS):
        oc = jnp.concatenate([s[TOK * head:TOK * (head + 1)] for s in oc_seq], axis=0)
        o.append((on_all[rows * head:rows * (head + 1)] + oc) / dens[head])
    return _merge_heads(o)


def _mix_out(x, oa, oh, proj, w_out_ref, ang_ref, hng_ref, g2_ref, b2_ref):
    oa = _rms_norm(oa, ang_ref[...])
    hng = hng_ref[...]
    gate = jax.nn.silu(proj[:, OFF_HG:])
    ohn = [_rms_norm(oh[h], hng[:, HGRN_DV * h:HGRN_DV * (h + 1)]) for h in range(HGRN_HEADS)]
    mix = jnp.concatenate([oa, jnp.concatenate(ohn, axis=1) * gate], axis=1).astype(BF16)
    y = _dot(mix, w_out_ref[...])
    return _layer_norm(ALPHA * x + y, g2_ref[...], b2_ref[...])


def _proj_stream(ctx, x, w_in_ref, may_start):
    while not may_start():
        yield
    xb = x.astype(BF16)
    chunks = []
    for c in range(D_IN // PROJ_CHUNK):
        chunks.append(_dot(xb, w_in_ref[:, PROJ_CHUNK * c:PROJ_CHUNK * (c + 1)]))
        yield
    ctx["proj"] = jnp.concatenate(chunks, axis=1)


def _prepare_stream(ctx, lb, tri):
    while "proj" not in ctx:
        yield
    proj = ctx["proj"]
    ctx["qz"] = _head_queries(proj[:, :ATTN_WIDTH])
    yield
    ctx["q"], logf, ctx["kin"], ctx["vin"] = _hgrn_gates(proj, lb)
    yield
    ctx["b"] = [_block_cumsum(tri, logf[BLK * blk:BLK * (blk + 1)]) for blk in range(PROMPT_BLOCKS)]
    ctx["ready"] = True


def _attention_stream(ctx, blk, prev_kv, bias, sink_ref):
    while "ready" not in ctx:
        yield
    rows = slice(BLK * blk, BLK * (blk + 1))
    k_prev, v_prev = prev_kv()
    k_all = jnp.concatenate([k_prev, ctx["proj"][rows, OFF_K:OFF_V]], axis=0)
    v_all = jnp.concatenate([v_prev, ctx["proj"][rows, OFF_V:OFF_HQ]], axis=0)
    out = {}
    yield from _attend_steps([z[rows] for z in ctx["qz"]], k_all, v_all, sink_ref, bias, out)
    ctx["oa"][blk] = out["oa"]


def _hgrn_stream(ctx, blk, h, state, turn, xor):
    while "ready" not in ctx:
        yield
    rows = slice(BLK * blk, BLK * (blk + 1))
    sl = slice(HGRN_DK * h, HGRN_DK * (h + 1))
    qh, kh, bh = ctx["q"][rows, sl], ctx["kin"][rows, sl], ctx["b"][blk][:, sl]
    vh = ctx["vin"][rows, sl].astype(BF16)
    scores = {}
    yield from _intra_steps(qh, kh, bh, BLK // 2, xor, scores)
    o_intra = _dot(scores["a"].astype(BF16), vh)
    yield
    b_last = bh[BLK - 1:BLK, :]
    qe = (qh * jnp.exp2(bh)).astype(BF16)
    kd = (kh * jnp.exp2(b_last - bh)).astype(BF16)
    yield
    while state["turn"][h] != turn:
        yield
    st = state["st"][h]
    o_inter = _dot_nt(qe, st.astype(BF16))
    update = _dot_tn(vh, kd)
    yield
    state["st"][h] = st * jnp.exp2(b_last) + update
    state["turn"][h] = turn + 1
    ctx["oh"][blk][h] = o_intra + o_inter


def _mix_out_stream(ctx, x, y_store, w_out_ref, ang_ref, hng_ref, g2_ref, b2_ref):
    while any(o is None for o in ctx["oa"]) or any(o is None for blk in ctx["oh"] for o in blk):
        yield
    oa = jnp.concatenate(ctx["oa"], axis=0)
    oh = [jnp.concatenate([ctx["oh"][blk][h] for blk in range(PROMPT_BLOCKS)], axis=0)
          for h in range(HGRN_HEADS)]
    y_store(_mix_out(x, oa, oh, ctx["proj"], w_out_ref, ang_ref, hng_ref, g2_ref, b2_ref))


def _mixer_prompt_kernel(sink_ref, x_ref, w_in_ref, w_out_ref, lbp_ref, ang_ref, hng_ref, g2_ref,
                         b2_ref, y_ref, s_out_ref, k_out_ref, v_out_ref, st_ref, kv_ref):
    j = pl.program_id(1)

    @pl.when(j == 0)
    def _():
        st_ref[...] = jnp.zeros_like(st_ref)
        kv_ref[...] = jnp.zeros_like(kv_ref)

    lb = _lower_bound(lbp_ref[...])
    tri = _tri_matrix(BLK)
    xor = (lax.broadcasted_iota(jnp.int32, (BLK, BLK), 0)
           ^ lax.broadcasted_iota(jnp.int32, (BLK, BLK), 1))
    trow = lax.broadcasted_iota(jnp.int32, (BLK, 2 * WINDOW), 0)
    full_bias = _alibi_bias(jnp.full(trow.shape, WINDOW, jnp.int32))

    state = {"st": [st_ref[h] for h in range(HGRN_HEADS)], "turn": [0] * HGRN_HEADS}
    ctxs = [{"oa": [None] * PROMPT_BLOCKS,
             "oh": [[None] * HGRN_HEADS for _ in range(PROMPT_BLOCKS)]} for _ in range(PROMPT_TILES)]

    def kv_block(tile, blk):
        rows = slice(BLK * blk, BLK * (blk + 1))
        proj = ctxs[tile]["proj"]
        return proj[rows, OFF_K:OFF_V], proj[rows, OFF_V:OFF_HQ]

    streams = []
    for tile in range(PROMPT_TILES):
        ctx = ctxs[tile]
        x = x_ref[tile]
        may_start = (lambda: True) if tile == 0 else (lambda prev=ctxs[tile - 1]: "ready" in prev)
        streams.append(_proj_stream(ctx, x, w_in_ref, may_start))
        streams.append(_prepare_stream(ctx, lb, tri))
        for blk in range(PROMPT_BLOCKS):
            if tile == 0 and blk == 0:
                prev_kv = lambda: (kv_ref[0], kv_ref[1])
                bias = _alibi_bias(jnp.where(j > 0, WINDOW, trow + 1))
            else:
                g = tile * PROMPT_BLOCKS + blk - 1
                prev_kv = lambda g=g: kv_block(g // PROMPT_BLOCKS, g % PROMPT_BLOCKS)
                bias = full_bias
            streams.append(_attention_stream(ctx, blk, prev_kv, bias, sink_ref))
            for h in range(HGRN_HEADS):
                streams.append(_hgrn_stream(ctx, blk, h, state, tile * PROMPT_BLOCKS + blk, xor))

        def y_store(y, tile=tile):
            y_ref[tile] = y

        streams.append(_mix_out_stream(ctx, x, y_store, w_out_ref, ang_ref, hng_ref, g2_ref, b2_ref))
    _round_robin(streams)

    k_last, v_last = kv_block(PROMPT_TILES - 1, PROMPT_BLOCKS - 1)
    kv_ref[0] = k_last
    kv_ref[1] = v_last
    for h in range(HGRN_HEADS):
        st_ref[h] = state["st"][h]

    @pl.when(j == pl.num_programs(1) - 1)
    def _():
        k_out_ref[0] = k_last
        v_out_ref[0] = v_last
        for h in range(HGRN_HEADS):
            s_out_ref[0, h] = state["st"][h].T


MIXER_WEIGHT_BYTES = (D_MODEL * D_IN + D_MIX * D_MODEL) * BF16_BYTES
STATE_BYTES = HGRN_HEADS * HGRN_DK * HGRN_DV * F32_BYTES
WINDOW_BYTES = WINDOW * KV_WIDTH * F32_BYTES


def _mixer_value_bytes(rows, keys):
    per_row = ((D_IN + 4 * HGRN_KW + 2 * D_MODEL) * F32_BYTES
               + (ATTN_HEADS * LANES + D_MIX) * BF16_BYTES)
    scores = ATTN_HEADS * BLK * keys * (F32_BYTES + BF16_BYTES)
    return rows * per_row + (rows // BLK) * scores


def _mixer_prompt(x, sink, w_in, w_out, lbp, ang, hng, g2, b2):
    nb, seq, _ = x.shape
    tile_rows = PROMPT_BLOCKS * BLK
    steps_per_seq = seq // (PROMPT_TILES * tile_rows)
    n_tiles = nb * seq // tile_rows
    xt = x.reshape(n_tiles, tile_rows, D_MODEL)
    blocks = (2 * PROMPT_TILES * tile_rows * D_MODEL * F32_BYTES
              + STATE_BYTES + 2 * WINDOW_BYTES)
    vmem = _vmem_limit(MIXER_WEIGHT_BYTES + STATE_BYTES + 2 * WINDOW_BYTES, blocks,
                       _mixer_value_bytes(2 * tile_rows, 2 * WINDOW))
    tile_spec = pl.BlockSpec((PROMPT_TILES, tile_rows, D_MODEL),
                             lambda b, j: (b * steps_per_seq + j, 0, 0))
    y, s, k, v = pl.pallas_call(
        _mixer_prompt_kernel,
        grid=(nb, steps_per_seq),
        in_specs=[
            pl.BlockSpec(memory_space=pltpu.SMEM),
            tile_spec,
            _const_spec((D_MODEL, D_IN)),
            _const_spec((D_MIX, D_MODEL)),
            _const_spec((DEPTH + 1, HGRN_KW)),
            _const_spec((1, ATTN_WIDTH)),
            _const_spec((1, HGRN_VW)),
            _const_spec((1, D_MODEL)),
            _const_spec((1, D_MODEL)),
        ],
        out_specs=[
            tile_spec,
            pl.BlockSpec((1, HGRN_HEADS, HGRN_DK, HGRN_DV), lambda b, j: (b, 0, 0, 0)),
            pl.BlockSpec((1, WINDOW, KV_WIDTH), lambda b, j: (b, 0, 0)),
            pl.BlockSpec((1, WINDOW, KV_WIDTH), lambda b, j: (b, 0, 0)),
        ],
        out_shape=[
            jax.ShapeDtypeStruct((n_tiles, tile_rows, D_MODEL), F32),
            jax.ShapeDtypeStruct((nb, HGRN_HEADS, HGRN_DK, HGRN_DV), F32),
            jax.ShapeDtypeStruct((nb, WINDOW, KV_WIDTH), F32),
            jax.ShapeDtypeStruct((nb, WINDOW, KV_WIDTH), F32),
        ],
        scratch_shapes=[
            pltpu.VMEM((HGRN_HEADS, HGRN_DV, HGRN_DK), F32),
            pltpu.VMEM((2, WINDOW, KV_WIDTH), F32),
        ],
        compiler_params=pltpu.CompilerParams(
            dimension_semantics=("arbitrary", "arbitrary"), vmem_limit_bytes=vmem),
        name="mixer_prompt",
    )(sink, xt, w_in, w_out, lbp, ang, hng, g2, b2)
    return y.reshape(nb, seq, D_MODEL), s, k, v


def _mixer_sample_kernel(sink_ref, x_ref, s0_ref, ck_ref, cv_ref, w_in_ref, w_out_ref, lbp_ref,
                         ang_ref, hng_ref, g2_ref, b2_ref, y_ref, s_out_ref, k_out_ref, v_out_ref):
    x = x_ref[...]
    proj = _dot(x.astype(BF16), w_in_ref[...])
    k_new = proj[:, OFF_K:OFF_V]
    v_new = proj[:, OFF_V:OFF_HQ]

    oa = _attend_sample(_head_queries(proj[:, :ATTN_WIDTH]), k_new, v_new, ck_ref, cv_ref, sink_ref)
    for i in range(SEQ_PER_STEP):
        rows = slice(TOK * i, TOK * (i + 1))
        k_out_ref[i] = jnp.concatenate([ck_ref[i, TOK:, :], k_new[rows]], axis=0)
        v_out_ref[i] = jnp.concatenate([cv_ref[i, TOK:, :], v_new[rows]], axis=0)

    q, logf, kin, vin = _hgrn_gates(proj, _lower_bound(lbp_ref[...]))
    b = _block_cumsum(_tri_matrix(TOK), logf)
    xor = (lax.broadcasted_iota(jnp.int32, (BLK, BLK), 0)
           ^ lax.broadcasted_iota(jnp.int32, (BLK, BLK), 1))
    lane = lax.broadcasted_iota(jnp.int32, (HGRN_DK, BLK), 1)
    oh = []
    for h in range(HGRN_HEADS):
        sl = slice(HGRN_DK * h, HGRN_DK * (h + 1))
        qh, kh, bh = q[:, sl], kin[:, sl], b[:, sl]
        vh = vin[:, sl].astype(BF16)
        a = _intra_scores(qh, kh, bh, TOK // 2, xor)
        o_intra = _dot(a.astype(BF16), vh)
        qe = (qh * jnp.exp2(bh)).astype(BF16)
        b3 = bh.reshape(SEQ_PER_STEP, TOK, HGRN_DK)
        b_last = jnp.broadcast_to(b3[:, TOK - 1:TOK, :], b3.shape).reshape(BLK, HGRN_DK)
        kd_t = (kh * jnp.exp2(b_last - bh)).T
        dec_t = jnp.exp2(b_last).T
        o_rows = []
        for i in range(SEQ_PER_STEP):
            s0 = s0_ref[i, h]
            o_rows.append(_dot(qe[TOK * i:TOK * (i + 1)], s0.astype(BF16)))
            kd_i = jnp.where((lane >= TOK * i) & (lane < TOK * (i + 1)), kd_t, 0.0).astype(BF16)
            s_out_ref[i, h] = s0 * dec_t[:, TOK * i:TOK * i + 1] + _dot(kd_i, vh)
        oh.append(o_intra + jnp.concatenate(o_rows, axis=0))

    y_ref[...] = _mix_out(x, oa, oh, proj, w_out_ref, ang_ref, hng_ref, g2_ref, b2_ref)


def _mixer_sample(x, s0, ck, cv, sink, w_in, w_out, lbp, ang, hng, g2, b2):
    n = x.shape[0]
    nseq = s0.shape[0]
    state_spec = pl.BlockSpec((SEQ_PER_STEP, HGRN_HEADS, HGRN_DK, HGRN_DV), lambda i: (i, 0, 0, 0))
    cache_spec = pl.BlockSpec((SEQ_PER_STEP, WINDOW, KV_WIDTH), lambda i: (i, 0, 0))
    blocks = 2 * BLK * D_MODEL * F32_BYTES + 2 * SEQ_PER_STEP * (STATE_BYTES + 2 * WINDOW_BYTES)
    values = (_mixer_value_bytes(BLK, WINDOW + BLK)
              + HGRN_HEADS * 2 * HGRN_DK * BLK * F32_BYTES)
    vmem = _vmem_limit(MIXER_WEIGHT_BYTES, blocks, values)
    return pl.pallas_call(
        _mixer_sample_kernel,
        grid=(n // BLK,),
        in_specs=[
            pl.BlockSpec(memory_space=pltpu.SMEM),
            pl.BlockSpec((BLK, D_MODEL), lambda i: (i, 0)),
            state_spec, cache_spec, cache_spec,
            _const_spec((D_MODEL, D_IN)),
            _const_spec((D_MIX, D_MODEL)),
            _const_spec((DEPTH + 1, HGRN_KW)),
            _const_spec((1, ATTN_WIDTH)),
            _const_spec((1, HGRN_VW)),
            _const_spec((1, D_MODEL)),
            _const_spec((1, D_MODEL)),
        ],
        out_specs=[
            pl.BlockSpec((BLK, D_MODEL), lambda i: (i, 0)),
            state_spec, cache_spec, cache_spec,
        ],
        out_shape=[
            jax.ShapeDtypeStruct((n, D_MODEL), F32),
            jax.ShapeDtypeStruct((nseq, HGRN_HEADS, HGRN_DK, HGRN_DV), F32),
            jax.ShapeDtypeStruct((nseq, WINDOW, KV_WIDTH), F32),
            jax.ShapeDtypeStruct((nseq, WINDOW, KV_WIDTH), F32),
        ],
        compiler_params=pltpu.CompilerParams(
            dimension_semantics=("arbitrary",), vmem_limit_bytes=vmem),
        name="mixer_sample",
    )(sink, x, s0, ck, cv, w_in, w_out, lbp, ang, hng, g2, b2)


def kernel(x_prompt, x_sample, state_hgrn, cache_win_k, cache_win_v, ln1_g, ln1_b, ffn1_w13,
           ffn1_w2, w_in, lb_param, attn_sink, attn_norm_g, hgrn_norm_g, w_out, ln2_g, ln2_b,
           ffn2_w13, ffn2_w2, ln3_g, ln3_b):
    nb, seq, d = x_prompt.shape
    ns, tn, _ = x_sample.shape
    assert state_hgrn.shape[0] == DEPTH == 1 and d == D_MODEL
    assert seq % (PROMPT_TILES * PROMPT_BLOCKS * BLK) == 0 and tn == TOK and (ns * tn) % BLK == 0
    assert cache_win_k.shape[2] == WINDOW

    w13_1 = ffn1_w13[0].astype(BF16)
    w2_1 = ffn1_w2[0].astype(BF16)
    sink = attn_sink[0]
    lbp = lb_param.astype(F32)

    xp = x_prompt.reshape(nb * seq, d)
    xs = x_sample.reshape(ns * tn, d)

    xp, w13_2, w2_2, w_in_b, w_out_b = _ffn_ln(
        xp, w13_1, w2_1, ln1_g, ln1_b, to_cast=(ffn2_w13[0], ffn2_w2[0], w_in[0], w_out[0]))
    xs = _ffn_ln(xs, w13_1, w2_1, ln1_g, ln1_b)
    mix_params = (sink, w_in_b, w_out_b, lbp, attn_norm_g, hgrn_norm_g, ln2_g, ln2_b)

    xp, sp, kp, vp = _mixer_prompt(xp.reshape(nb, seq, d), *mix_params)
    xs, ss, ks, vs = _mixer_sample(
        xs, state_hgrn[0], cache_win_k[0].reshape(ns, WINDOW, KV_WIDTH),
        cache_win_v[0].reshape(ns, WINDOW, KV_WIDTH), *mix_params)

    xp = _ffn_ln(xp.reshape(nb * seq, d), w13_2, w2_2, ln3_g, ln3_b)
    xs = _ffn_ln(xs, w13_2, w2_2, ln3_g, ln3_b)

    cache_tail = (WINDOW, ATTN_KV_HEADS, HEAD_DIM)
    return (xp.reshape(nb, seq, d), xs.reshape(ns, tn, d),
            sp[None], kp.reshape((DEPTH, nb) + cache_tail), vp.reshape((DEPTH, nb) + cache_tail),
            ss[None], ks.reshape((DEPTH, ns) + cache_tail), vs.reshape((DEPTH, ns) + cache_tail))
```

```python
import functools

import jax
import jax.numpy as jnp
from jax import lax
from jax.experimental import pallas as pl
from jax.experimental.pallas import tpu as pltpu

F32 = jnp.float32
BF16 = jnp.bfloat16

D_MODEL = 1024
DEPTH = 1
ATTN_HEADS = 8
ATTN_KV_HEADS = 2
GROUP = ATTN_HEADS // ATTN_KV_HEADS
HEAD_DIM = 64
ATTN_WIDTH = ATTN_HEADS * HEAD_DIM
KV_WIDTH = ATTN_KV_HEADS * HEAD_DIM
WINDOW = 128
ATTN_SCALE = HEAD_DIM ** -0.5
HGRN_HEADS = 4
HGRN_DK = 128
HGRN_DV = 128
HGRN_KW = HGRN_HEADS * HGRN_DK
HGRN_VW = HGRN_HEADS * HGRN_DV
D_MIX = ATTN_WIDTH + HGRN_VW
D_IN = ATTN_WIDTH + 2 * KV_WIDTH + 2 * HGRN_KW + 2 * HGRN_VW
D_FF = 2816
ALPHA = (2.0 * DEPTH) ** 0.25
LN_EPS = 1e-5
RMS_EPS = 1e-6

OFF_K = ATTN_WIDTH
OFF_V = OFF_K + KV_WIDTH
OFF_HQ = OFF_V + KV_WIDTH
OFF_HF = OFF_HQ + HGRN_KW
OFF_HI = OFF_HF + HGRN_KW
OFF_HG = OFF_HI + HGRN_VW

LANES = 128
SUBLANES = 8
BF16_TILE_ROWS = 2 * SUBLANES
MXU_WIDTH = 256
BLK = 128
PROMPT_BLOCKS = 2
PROMPT_TILES = 2
PROJ_CHUNK = MXU_WIDTH
FFN_TILE = 256
FFN_TILES = 4
FFN_CHUNK = MXU_WIDTH
TOK = SUBLANES
SEQ_PER_STEP = BLK // TOK
V7X_VMEM_BYTES = 64 * 1024 * 1024
F32_BYTES = 4
BF16_BYTES = 2

NT_DIMS = (((1,), (1,)), ((), ()))
TN_DIMS = (((0,), (0,)), ((), ()))


def _vmem_limit(resident_bytes, streamed_bytes, value_bytes):
    need = resident_bytes + 2 * streamed_bytes + value_bytes
    assert need <= V7X_VMEM_BYTES, need
    return need


_DONE = object()


def _round_robin(streams):
    live = list(streams)
    rounds = 0
    while live:
        live = [g for g in live if next(g, _DONE) is not _DONE]
        rounds += 1
        assert rounds < 100000, "a stream waits for something no other stream produces"


def _dot(a, b):
    return jnp.dot(a, b, preferred_element_type=F32)


def _dot_nt(a, b):
    return lax.dot_general(a, b, NT_DIMS, preferred_element_type=F32)


def _dot_tn(a, b):
    return lax.dot_general(a, b, TN_DIMS, preferred_element_type=F32)


def _layer_norm(r, g, b):
    mu = jnp.mean(r, axis=-1, keepdims=True)
    c = r - mu
    var = jnp.mean(c * c, axis=-1, keepdims=True)
    return c * lax.rsqrt(var + LN_EPS) * g + b


def _rms_norm(x, g):
    return x * lax.rsqrt(jnp.mean(x * x, axis=-1, keepdims=True) + RMS_EPS) * g


def _ffn_steps(x, w13_ref, w2_ref, g, b, store, on_hidden=lambda: None):
    xb = x.astype(BF16)
    acts = []
    for c in range(D_FF // FFN_CHUNK):
        gate = _dot(xb, w13_ref[:, FFN_CHUNK * c:FFN_CHUNK * (c + 1)])
        up = _dot(xb, w13_ref[:, D_FF + FFN_CHUNK * c:D_FF + FFN_CHUNK * (c + 1)])
        acts.append((jax.nn.silu(gate) * up).astype(BF16))
        yield
    act = jnp.concatenate(acts, axis=1)
    on_hidden()
    ys = []
    for c in range(D_MODEL // FFN_CHUNK):
        ys.append(_dot(act, w2_ref[:, FFN_CHUNK * c:FFN_CHUNK * (c + 1)]))
        yield
    store(_layer_norm(ALPHA * x + 0.5 * jnp.concatenate(ys, axis=1), g, b))


def _ffn_stream(x_ref, o_ref, tile, w13_ref, w2_ref, g, b, flags, may_start):
    while not may_start():
        yield
    rows = slice(FFN_TILE * tile, FFN_TILE * (tile + 1))

    def store(y):
        o_ref[rows, :] = y

    def on_hidden():
        flags[tile] = True

    yield from _ffn_steps(x_ref[rows, :], w13_ref, w2_ref, g, b, store, on_hidden)


def _cast_stream(src_refs, dst_refs):
    for src, dst in zip(src_refs, dst_refs):
        dst[...] = src[...].astype(BF16)
        yield


def _ffn_ln_kernel(n_cast, x_ref, w13_ref, w2_ref, g_ref, b_ref, *refs):
    cast_src, o_ref, cast_dst = refs[:n_cast], refs[n_cast], refs[n_cast + 1:]
    n_tiles = x_ref.shape[0] // FFN_TILE
    g, b = g_ref[...], b_ref[...]
    flags = [False] * n_tiles
    streams = [_ffn_stream(x_ref, o_ref, t, w13_ref, w2_ref, g, b, flags,
                           (lambda: True) if t == 0 else (lambda t=t: flags[t - 1]))
               for t in range(n_tiles)]
    _round_robin(streams + [_cast_stream(cast_src, cast_dst)])


def _const_spec(shape):
    return pl.BlockSpec(shape, lambda *_: (0,) * len(shape), pipeline_mode=pl.Buffered(1))


def _slab_spec(shape, steps):
    rows, cols = shape
    count = max(c for c in range(1, steps + 1)
                if rows % c == 0 and (rows // c) % BF16_TILE_ROWS == 0)
    return pl.BlockSpec((rows // count, cols), lambda i: (jnp.minimum(i, count - 1), 0))


def _ffn_ln(x, w13, w2, g, b, to_cast=()):
    n = x.shape[0]
    tm = FFN_TILE * FFN_TILES
    assert n % tm == 0
    steps = n // tm
    slabs = [_slab_spec(w.shape, steps) for w in to_cast]
    weights = (D_MODEL * 2 * D_FF + D_FF * D_MODEL) * BF16_BYTES
    blocks = (2 * tm * D_MODEL * F32_BYTES
              + sum(s.block_shape[0] * s.block_shape[1] for s in slabs) * (F32_BYTES + BF16_BYTES))
    values = 2 * FFN_TILE * ((D_MODEL + D_FF) * BF16_BYTES
                             + (2 * FFN_CHUNK + 2 * D_MODEL) * F32_BYTES)
    vmem = _vmem_limit(weights, blocks, values)
    out = pl.pallas_call(
        functools.partial(_ffn_ln_kernel, len(to_cast)),
        grid=(steps,),
        in_specs=[
            pl.BlockSpec((tm, D_MODEL), lambda i: (i, 0)),
            _const_spec((D_MODEL, 2 * D_FF)),
            _const_spec((D_FF, D_MODEL)),
            _const_spec((1, D_MODEL)),
            _const_spec((1, D_MODEL)),
        ] + slabs,
        out_specs=[pl.BlockSpec((tm, D_MODEL), lambda i: (i, 0))] + slabs,
        out_shape=[jax.ShapeDtypeStruct((n, D_MODEL), F32)]
        + [jax.ShapeDtypeStruct(w.shape, BF16) for w in to_cast],
        compiler_params=pltpu.CompilerParams(
            dimension_semantics=("arbitrary",), vmem_limit_bytes=vmem,
            allow_input_fusion=[False, True, True, False, False] + [False] * len(to_cast)),
        name="ffn_ln",
    )(x, w13, w2, g, b, *to_cast)
    return tuple(out) if to_cast else out[0]


def _lower_bound(lbp):
    mx = jnp.max(lbp, axis=0, keepdims=True)
    e = jnp.exp(lbp - mx)
    return e[0:1] / jnp.sum(e, axis=0, keepdims=True)


def _split3(x):
    hi = x.astype(BF16)
    r = x - hi.astype(F32)
    mid = r.astype(BF16)
    lo = (r - mid.astype(F32)).astype(BF16)
    return hi, mid, lo


def _block_cumsum(tri, g):
    w = g.shape[1]
    parts = jnp.concatenate(_split3(g), axis=1)
    s = _dot(tri, parts)
    return s[:, :w] + s[:, w:2 * w] + s[:, 2 * w:]


def _tri_matrix(group):
    r = lax.broadcasted_iota(jnp.int32, (BLK, BLK), 0)
    c = lax.broadcasted_iota(jnp.int32, (BLK, BLK), 1)
    keep = (c <= r) & ((r ^ c) < group)
    return jnp.where(keep, 1.0, 0.0).astype(BF16)


def _pair_ref_rows(b, m):
    t, w = b.shape
    if m >= SUBLANES:
        b3 = b.reshape(t // (2 * m), 2 * m, w)
        return jnp.broadcast_to(b3[:, m - 1:m, :], b3.shape).reshape(t, w)
    b3 = b.reshape(t // SUBLANES, SUBLANES, w)
    sub = lax.broadcasted_iota(jnp.int32, b3.shape, 1)

    def row(i):
        return jnp.broadcast_to(b3[:, i:i + 1, :], b3.shape)

    if m == 4:
        out = row(3)
    elif m == 2:
        out = jnp.where(sub < 4, row(1), row(5))
    else:
        out = jnp.where(sub < 4, jnp.where(sub < 2, row(0), row(2)),
                        jnp.where(sub < 6, row(4), row(6)))
    return out.reshape(t, w)


def _level_operands(q, k, b, m):
    t, w = q.shape
    ref = _pair_ref_rows(b, m)
    if m >= SUBLANES:
        zero = jnp.zeros((m, w), F32)
        qs, ks = [], []
        for blk in range(t // m):
            rows = slice(blk * m, (blk + 1) * m)
            if blk % 2:
                qs.append(q[rows] * jnp.exp2(b[rows] - ref[rows]))
                ks.append(zero)
            else:
                qs.append(zero)
                ks.append(k[rows] * jnp.exp2(ref[rows] - b[rows]))
        return jnp.concatenate(qs, axis=0).astype(BF16), jnp.concatenate(ks, axis=0).astype(BF16)
    upper = (lax.broadcasted_iota(jnp.int32, q.shape, 0) & m) != 0
    e = jnp.exp2((b - ref) * jnp.where(upper, 1.0, -1.0))
    return (jnp.where(upper, q * e, 0.0).astype(BF16), jnp.where(upper, 0.0, k * e).astype(BF16))


def _intra_steps(q, k, b, max_m, xor, out):
    a = jnp.where(xor == 0, jnp.sum(q * k, axis=-1, keepdims=True), 0.0)
    yield
    m = 1
    while m <= max_m:
        qs, ks = _level_operands(q, k, b, m)
        yield
        p = _dot_nt(qs, ks)
        yield
        a = jnp.where(xor >= m, p, a)
        m *= 2
    if 2 * max_m < q.shape[0]:
        a = jnp.where(xor < 2 * max_m, a, 0.0)
    out["a"] = a


def _intra_scores(q, k, b, max_m, xor):
    out = {}
    for _ in _intra_steps(q, k, b, max_m, xor, out):
        pass
    return out["a"]


def _hgrn_gates(proj, lb):
    q = jax.nn.silu(proj[:, OFF_HQ:OFF_HF])
    f = lb + (1.0 - lb) * jax.nn.sigmoid(proj[:, OFF_HF:OFF_HI])
    return q, jnp.log2(f), 1.0 - f, proj[:, OFF_HI:OFF_HG]


def _head_queries(qa):
    low = lax.broadcasted_iota(jnp.int32, (qa.shape[0], LANES), 1) < HEAD_DIM
    out = []
    for pair in range(ATTN_HEADS // 2):
        piece = qa[:, LANES * pair:LANES * (pair + 1)] * ATTN_SCALE
        rolled = pltpu.roll(piece, HEAD_DIM, axis=1)
        for half in range(2):
            kv = (2 * pair + half) // GROUP
            src = piece if half == kv else rolled
            z = jnp.where(low, src, 0.0) if kv == 0 else jnp.where(low, 0.0, src)
            out.append(z.astype(BF16))
    return out


def _merge_heads(o):
    low = lax.broadcasted_iota(jnp.int32, o[0].shape, 1) < HEAD_DIM
    cols = []
    for pair in range(ATTN_HEADS // 2):
        kv = (2 * pair) // GROUP
        even, odd = o[2 * pair], o[2 * pair + 1]
        if kv == 0:
            odd = pltpu.roll(odd, HEAD_DIM, axis=1)
        else:
            even = pltpu.roll(even, HEAD_DIM, axis=1)
        cols.append(jnp.where(low, even, odd))
    return jnp.concatenate(cols, axis=1)


def _slope(head):
    return 2.0 ** (-8.0 * (head + 1) / ATTN_HEADS)


def _alibi_bias(limit):
    trow = lax.broadcasted_iota(jnp.int32, limit.shape, 0)
    ccol = lax.broadcasted_iota(jnp.int32, limit.shape, 1)
    dist = trow + WINDOW - ccol
    visible = dist.astype(jnp.uint32) < limit.astype(jnp.uint32)
    distf = dist.astype(F32)
    return [jnp.where(visible, -_slope(head) * distf, -jnp.inf) for head in range(ATTN_HEADS)]


def _attend_steps(qz, k_all, v_all, sink_ref, bias, out):
    tq = qz[0].shape[0]
    s_all = _dot_nt(jnp.concatenate(qz, axis=0), k_all.astype(BF16))
    yield
    ps, dens = [], []
    for head in range(ATTN_HEADS):
        sink = sink_ref[head]
        s = s_all[tq * head:tq * (head + 1)] + bias[head]
        mx = jnp.maximum(jnp.max(s, axis=-1, keepdims=True), sink)
        yield
        p = jnp.exp(s - mx)
        dens.append(jnp.sum(p, axis=-1, keepdims=True) + jnp.exp(sink - mx))
        ps.append(p.astype(BF16))
        yield
    o_all = _dot(jnp.concatenate(ps, axis=0), v_all.astype(BF16))
    yield
    out["oa"] = _merge_heads([o_all[tq * h:tq * (h + 1)] / dens[h] for h in range(ATTN_HEADS)])


def _attend_sample(qz, k_new, v_new, ck_ref, cv_ref, sink_ref):
    rows = qz[0].shape[0]
    k_new = k_new.astype(BF16)
    v_new = v_new.astype(BF16)
    sn_all = _dot_nt(jnp.concatenate(qz, axis=0), k_new)
    sc_seq = []
    for i in range(SEQ_PER_STEP):
        lhs = jnp.concatenate([z[TOK * i:TOK * (i + 1)] for z in qz], axis=0)
        sc_seq.append(_dot_nt(lhs, ck_ref[i].astype(BF16)))

    r = lax.broadcasted_iota(jnp.int32, (rows, rows), 0)
    c = lax.broadcasted_iota(jnp.int32, (rows, rows), 1)
    new_visible = ((r ^ c) < TOK) & (c <= r)
    new_dist = (r - c).astype(F32)
    tok = lax.broadcasted_iota(jnp.int32, (rows, WINDOW), 0) & (TOK - 1)
    key = lax.broadcasted_iota(jnp.int32, (rows, WINDOW), 1)
    old_visible = key > tok
    old_dist = (WINDOW + tok - key).astype(F32)

    pn, pc, dens = [], [], []
    for head in range(ATTN_HEADS):
        sink = sink_ref[head]
        slope = _slope(head)
        sn = sn_all[rows * head:rows * (head + 1)]
        sc = jnp.concatenate([s[TOK * head:TOK * (head + 1)] for s in sc_seq], axis=0)
        sn = jnp.where(new_visible, sn - slope * new_dist, -jnp.inf)
        sc = jnp.where(old_visible, sc - slope * old_dist, -jnp.inf)
        mx = jnp.maximum(jnp.maximum(jnp.max(sn, axis=-1, keepdims=True),
                                     jnp.max(sc, axis=-1, keepdims=True)), sink)
        en = jnp.exp(sn - mx)
        ec = jnp.exp(sc - mx)
        dens.append(jnp.sum(en, axis=-1, keepdims=True) + jnp.sum(ec, axis=-1, keepdims=True)
                    + jnp.exp(sink - mx))
        pn.append(en.astype(BF16))
        pc.append(ec.astype(BF16))

    on_all = _dot(jnp.concatenate(pn, axis=0), v_new)
    oc_seq = []
    for i in range(SEQ_PER_STEP):
        lhs = jnp.concatenate([p[TOK * i:TOK * (i + 1)] for p in pc], axis=0)
        oc_seq.append(_dot(lhs, cv_ref[i].astype(BF16)))
    o = []
    for head in range(ATTN_HEADS):
        oc = jnp.concatenate([s[TOK * head:TOK * (head + 1)] for s in oc_seq], axis=0)
        o.append((on_all[rows * head:rows * (head + 1)] + oc) / dens[head])
    return _merge_heads(o)


def _mix_out(x, oa, oh, proj, w_out_ref, ang_ref, hng_ref, g2_ref, b2_ref):
    oa = _rms_norm(oa, ang_ref[...])
    hng = hng_ref[...]
    gate = jax.nn.silu(proj[:, OFF_HG:])
    ohn = [_rms_norm(oh[h], hng[:, HGRN_DV * h:HGRN_DV * (h + 1)]) for h in range(HGRN_HEADS)]
    mix = jnp.concatenate([oa, jnp.concatenate(ohn, axis=1) * gate], axis=1).astype(BF16)
    y = _dot(mix, w_out_ref[...])
    return _layer_norm(ALPHA * x + y, g2_ref[...], b2_ref[...])


def _proj_stream(ctx, x, w_in_ref, may_start):
    while not may_start():
        yield
    xb = x.astype(BF16)
    chunks = []
    for c in range(D_IN // PROJ_CHUNK):
        chunks.append(_dot(xb, w_in_ref[:, PROJ_CHUNK * c:PROJ_CHUNK * (c + 1)]))
        yield
    ctx["proj"] = jnp.concatenate(chunks, axis=1)


def _prepare_stream(ctx, lb, tri):
    while "proj" not in ctx:
        yield
    proj = ctx["proj"]
    ctx["qz"] = _head_queries(proj[:, :ATTN_WIDTH])
    yield
    ctx["q"], logf, ctx["kin"], ctx["vin"] = _hgrn_gates(proj, lb)
    yield
    ctx["b"] = [_block_cumsum(tri, logf[BLK * blk:BLK * (blk + 1)]) for blk in range(PROMPT_BLOCKS)]
    ctx["ready"] = True


def _attention_stream(ctx, blk, prev_kv, bias, sink_ref):
    while "ready" not in ctx:
        yield
    rows = slice(BLK * blk, BLK * (blk + 1))
    k_prev, v_prev = prev_kv()
    k_all = jnp.concatenate([k_prev, ctx["proj"][rows, OFF_K:OFF_V]], axis=0)
    v_all = jnp.concatenate([v_prev, ctx["proj"][rows, OFF_V:OFF_HQ]], axis=0)
    out = {}
    yield from _attend_steps([z[rows] for z in ctx["qz"]], k_all, v_all, sink_ref, bias, out)
    ctx["oa"][blk] = out["oa"]


def _hgrn_stream(ctx, blk, h, state, turn, xor):
    while "ready" not in ctx:
        yield
    rows = slice(BLK * blk, BLK * (blk + 1))
    sl = slice(HGRN_DK * h, HGRN_DK * (h + 1))
    qh, kh, bh = ctx["q"][rows, sl], ctx["kin"][rows, sl], ctx["b"][blk][:, sl]
    vh = ctx["vin"][rows, sl].astype(BF16)
    scores = {}
    yield from _intra_steps(qh, kh, bh, BLK // 2, xor, scores)
    o_intra = _dot(scores["a"].astype(BF16), vh)
    yield
    b_last = bh[BLK - 1:BLK, :]
    qe = (qh * jnp.exp2(bh)).astype(BF16)
    kd = (kh * jnp.exp2(b_last - bh)).astype(BF16)
    yield
    while state["turn"][h] != turn:
        yield
    st = state["st"][h]
    o_inter = _dot_nt(qe, st.astype(BF16))
    update = _dot_tn(vh, kd)
    yield
    state["st"][h] = st * jnp.exp2(b_last) + update
    state["turn"][h] = turn + 1
    ctx["oh"][blk][h] = o_intra + o_inter


def _mix_out_stream(ctx, x, y_store, w_out_ref, ang_ref, hng_ref, g2_ref, b2_ref):
    while any(o is None for o in ctx["oa"]) or any(o is None for blk in ctx["oh"] for o in blk):
        yield
    oa = jnp.concatenate(ctx["oa"], axis=0)
    oh = [jnp.concatenate([ctx["oh"][blk][h] for blk in range(PROMPT_BLOCKS)], axis=0)
          for h in range(HGRN_HEADS)]
    y_store(_mix_out(x, oa, oh, ctx["proj"], w_out_ref, ang_ref, hng_ref, g2_ref, b2_ref))


def _mixer_prompt_kernel(sink_ref, x_ref, w_in_ref, w_out_ref, lbp_ref, ang_ref, hng_ref, g2_ref,
                         b2_ref, y_ref, s_out_ref, k_out_ref, v_out_ref, st_ref, kv_ref):
    j = pl.program_id(1)

    @pl.when(j == 0)
    def _():
        st_ref[...] = jnp.zeros_like(st_ref)
        kv_ref[...] = jnp.zeros_like(kv_ref)

    lb = _lower_bound(lbp_ref[...])
    tri = _tri_matrix(BLK)
    xor = (lax.broadcasted_iota(jnp.int32, (BLK, BLK), 0)
           ^ lax.broadcasted_iota(jnp.int32, (BLK, BLK), 1))
    trow = lax.broadcasted_iota(jnp.int32, (BLK, 2 * WINDOW), 0)
    full_bias = _alibi_bias(jnp.full(trow.shape, WINDOW, jnp.int32))

    state = {"st": [st_ref[h] for h in range(HGRN_HEADS)], "turn": [0] * HGRN_HEADS}
    ctxs = [{"oa": [None] * PROMPT_BLOCKS,
             "oh": [[None] * HGRN_HEADS for _ in range(PROMPT_BLOCKS)]} for _ in range(PROMPT_TILES)]

    def kv_block(tile, blk):
        rows = slice(BLK * blk, BLK * (blk + 1))
        proj = ctxs[tile]["proj"]
        return proj[rows, OFF_K:OFF_V], proj[rows, OFF_V:OFF_HQ]

    streams = []
    for tile in range(PROMPT_TILES):
        ctx = ctxs[tile]
        x = x_ref[tile]
        may_start = (lambda: True) if tile == 0 else (lambda prev=ctxs[tile - 1]: "ready" in prev)
        streams.append(_proj_stream(ctx, x, w_in_ref, may_start))
        streams.append(_prepare_stream(ctx, lb, tri))
        for blk in range(PROMPT_BLOCKS):
            if tile == 0 and blk == 0:
                prev_kv = lambda: (kv_ref[0], kv_ref[1])
                bias = _alibi_bias(jnp.where(j > 0, WINDOW, trow + 1))
            else:
                g = tile * PROMPT_BLOCKS + blk - 1
                prev_kv = lambda g=g: kv_block(g // PROMPT_BLOCKS, g % PROMPT_BLOCKS)
                bias = full_bias
            streams.append(_attention_stream(ctx, blk, prev_kv, bias, sink_ref))
            for h in range(HGRN_HEADS):
                streams.append(_hgrn_stream(ctx, blk, h, state, tile * PROMPT_BLOCKS + blk, xor))

        def y_store(y, tile=tile):
            y_ref[tile] = y

        streams.append(_mix_out_stream(ctx, x, y_store, w_out_ref, ang_ref, hng_ref, g2_ref, b2_ref))
    _round_robin(streams)

    k_last, v_last = kv_block(PROMPT_TILES - 1, PROMPT_BLOCKS - 1)
    kv_ref[0] = k_last
    kv_ref[1] = v_last
    for h in range(HGRN_HEADS):
        st_ref[h] = state["st"][h]

    @pl.when(j == pl.num_programs(1) - 1)
    def _():
        k_out_ref[0] = k_last
        v_out_ref[0] = v_last
        for h in range(HGRN_HEADS):
            s_out_ref[0, h] = state["st"][h].T


MIXER_WEIGHT_BYTES = (D_MODEL * D_IN + D_MIX * D_MODEL) * BF16_BYTES
STATE_BYTES = HGRN_HEADS * HGRN_DK * HGRN_DV * F32_BYTES
WINDOW_BYTES = WINDOW * KV_WIDTH * F32_BYTES


def _mixer_value_bytes(rows, keys):
    per_row = ((D_IN + 4 * HGRN_KW + 2 * D_MODEL) * F32_BYTES
               + (ATTN_HEADS * LANES + D_MIX) * BF16_BYTES)
    scores = ATTN_HEADS * BLK * keys * (F32_BYTES + BF16_BYTES)
    return rows * per_row + (rows // BLK) * scores


def _mixer_prompt(x, sink, w_in, w_out, lbp, ang, hng, g2, b2):
    nb, seq, _ = x.shape
    tile_rows = PROMPT_BLOCKS * BLK
    steps_per_seq = seq // (PROMPT_TILES * tile_rows)
    n_tiles = nb * seq // tile_rows
    xt = x.reshape(n_tiles, tile_rows, D_MODEL)
    blocks = (2 * PROMPT_TILES * tile_rows * D_MODEL * F32_BYTES
              + STATE_BYTES + 2 * WINDOW_BYTES)
    vmem = _vmem_limit(MIXER_WEIGHT_BYTES + STATE_BYTES + 2 * WINDOW_BYTES, blocks,
                       _mixer_value_bytes(2 * tile_rows, 2 * WINDOW))
    tile_spec = pl.BlockSpec((PROMPT_TILES, tile_rows, D_MODEL),
                             lambda b, j: (b * steps_per_seq + j, 0, 0))
    y, s, k, v = pl.pallas_call(
        _mixer_prompt_kernel,
        grid=(nb, steps_per_seq),
        in_specs=[
            pl.BlockSpec(memory_space=pltpu.SMEM),
            tile_spec,
            _const_spec((D_MODEL, D_IN)),
            _const_spec((D_MIX, D_MODEL)),
            _const_spec((DEPTH + 1, HGRN_KW)),
            _const_spec((1, ATTN_WIDTH)),
            _const_spec((1, HGRN_VW)),
            _const_spec((1, D_MODEL)),
            _const_spec((1, D_MODEL)),
        ],
        out_specs=[
            tile_spec,
            pl.BlockSpec((1, HGRN_HEADS, HGRN_DK, HGRN_DV), lambda b, j: (b, 0, 0, 0)),
            pl.BlockSpec((1, WINDOW, KV_WIDTH), lambda b, j: (b, 0, 0)),
            pl.BlockSpec((1, WINDOW, KV_WIDTH), lambda b, j: (b, 0, 0)),
        ],
        out_shape=[
            jax.ShapeDtypeStruct((n_tiles, tile_rows, D_MODEL), F32),
            jax.ShapeDtypeStruct((nb, HGRN_HEADS, HGRN_DK, HGRN_DV), F32),
            jax.ShapeDtypeStruct((nb, WINDOW, KV_WIDTH), F32),
            jax.ShapeDtypeStruct((nb, WINDOW, KV_WIDTH), F32),
        ],
        scratch_shapes=[
            pltpu.VMEM((HGRN_HEADS, HGRN_DV, HGRN_DK), F32),
            pltpu.VMEM((2, WINDOW, KV_WIDTH), F32),
        ],
        compiler_params=pltpu.CompilerParams(
            dimension_semantics=("arbitrary", "arbitrary"), vmem_limit_bytes=vmem),
        name="mixer_prompt",
    )(sink, xt, w_in, w_out, lbp, ang, hng, g2, b2)
    return y.reshape(nb, seq, D_MODEL), s, k, v


def _mixer_sample_kernel(sink_ref, x_ref, s0_ref, ck_ref, cv_ref, w_in_ref, w_out_ref, lbp_ref,
                         ang_ref, hng_ref, g2_ref, b2_ref, y_ref, s_out_ref, k_out_ref, v_out_ref):
    x = x_ref[...]
    proj = _dot(x.astype(BF16), w_in_ref[...])
    k_new = proj[:, OFF_K:OFF_V]
    v_new = proj[:, OFF_V:OFF_HQ]

    oa = _attend_sample(_head_queries(proj[:, :ATTN_WIDTH]), k_new, v_new, ck_ref, cv_ref, sink_ref)
    for i in range(SEQ_PER_STEP):
        rows = slice(TOK * i, TOK * (i + 1))
        k_out_ref[i] = jnp.concatenate([ck_ref[i, TOK:, :], k_new[rows]], axis=0)
        v_out_ref[i] = jnp.concatenate([cv_ref[i, TOK:, :], v_new[rows]], axis=0)

    q, logf, kin, vin = _hgrn_gates(proj, _lower_bound(lbp_ref[...]))
    b = _block_cumsum(_tri_matrix(TOK), logf)
    xor = (lax.broadcasted_iota(jnp.int32, (BLK, BLK), 0)
           ^ lax.broadcasted_iota(jnp.int32, (BLK, BLK), 1))
    lane = lax.broadcasted_iota(jnp.int32, (HGRN_DK, BLK), 1)
    oh = []
    for h in range(HGRN_HEADS):
        sl = slice(HGRN_DK * h, HGRN_DK * (h + 1))
        qh, kh, bh = q[:, sl], kin[:, sl], b[:, sl]
        vh = vin[:, sl].astype(BF16)
        a = _intra_scores(qh, kh, bh, TOK // 2, xor)
        o_intra = _dot(a.astype(BF16), vh)
        qe = (qh * jnp.exp2(bh)).astype(BF16)
        b3 = bh.reshape(SEQ_PER_STEP, TOK, HGRN_DK)
        b_last = jnp.broadcast_to(b3[:, TOK - 1:TOK, :], b3.shape).reshape(BLK, HGRN_DK)
        kd_t = (kh * jnp.exp2(b_last - bh)).T
        dec_t = jnp.exp2(b_last).T
        o_rows = []
        for i in range(SEQ_PER_STEP):
            s0 = s0_ref[i, h]
            o_rows.append(_dot(qe[TOK * i:TOK * (i + 1)], s0.astype(BF16)))
            kd_i = jnp.where((lane >= TOK * i) & (lane < TOK * (i + 1)), kd_t, 0.0).astype(BF16)
            s_out_ref[i, h] = s0 * dec_t[:, TOK * i:TOK * i + 1] + _dot(kd_i, vh)
        oh.append(o_intra + jnp.concatenate(o_rows, axis=0))

    y_ref[...] = _mix_out(x, oa, oh, proj, w_out_ref, ang_ref, hng_ref, g2_ref, b2_ref)


def _mixer_sample(x, s0, ck, cv, sink, w_in, w_out, lbp, ang, hng, g2, b2):
    n = x.shape[0]
    nseq = s0.shape[0]
    state_spec = pl.BlockSpec((SEQ_PER_STEP, HGRN_HEADS, HGRN_DK, HGRN_DV), lambda i: (i, 0, 0, 0))
    cache_spec = pl.BlockSpec((SEQ_PER_STEP, WINDOW, KV_WIDTH), lambda i: (i, 0, 0))
    blocks = 2 * BLK * D_MODEL * F32_BYTES + 2 * SEQ_PER_STEP * (STATE_BYTES + 2 * WINDOW_BYTES)
    values = (_mixer_value_bytes(BLK, WINDOW + BLK)
              + HGRN_HEADS * 2 * HGRN_DK * BLK * F32_BYTES)
    vmem = _vmem_limit(MIXER_WEIGHT_BYTES, blocks, values)
    return pl.pallas_call(
        _mixer_sample_kernel,
        grid=(n // BLK,),
        in_specs=[
            pl.BlockSpec(memory_space=pltpu.SMEM),
            pl.BlockSpec((BLK, D_MODEL), lambda i: (i, 0)),
            state_spec, cache_spec, cache_spec,
            _const_spec((D_MODEL, D_IN)),
            _const_spec((D_MIX, D_MODEL)),
            _const_spec((DEPTH + 1, HGRN_KW)),
            _const_spec((1, ATTN_WIDTH)),
            _const_spec((1, HGRN_VW)),
            _const_spec((1, D_MODEL)),
            _const_spec((1, D_MODEL)),
        ],
        out_specs=[
            pl.BlockSpec((BLK, D_MODEL), lambda i: (i, 0)),
            state_spec, cache_spec, cache_spec,
        ],
        out_shape=[
            jax.ShapeDtypeStruct((n, D_MODEL), F32),
            jax.ShapeDtypeStruct((nseq, HGRN_HEADS, HGRN_DK, HGRN_DV), F32),
            jax.ShapeDtypeStruct((nseq, WINDOW, KV_WIDTH), F32),
            jax.ShapeDtypeStruct((nseq, WINDOW, KV_WIDTH), F32),
        ],
        compiler_params=pltpu.CompilerParams(
            dimension_semantics=("arbitrary",), vmem_limit_bytes=vmem),
        name="mixer_sample",
    )(sink, x, s0, ck, cv, w_in, w_out, lbp, ang, hng, g2, b2)


def kernel(x_prompt, x_sample, state_hgrn, cache_win_k, cache_win_v, ln1_g, ln1_b, ffn1_w13,
           ffn1_w2, w_in, lb_param, attn_sink, attn_norm_g, hgrn_norm_g, w_out, ln2_g, ln2_b,
           ffn2_w13, ffn2_w2, ln3_g, ln3_b):
    nb, seq, d = x_prompt.shape
    ns, tn, _ = x_sample.shape
    assert state_hgrn.shape[0] == DEPTH == 1 and d == D_MODEL
    assert seq % (PROMPT_TILES * PROMPT_BLOCKS * BLK) == 0 and tn == TOK and (ns * tn) % BLK == 0
    assert cache_win_k.shape[2] == WINDOW

    w13_1 = ffn1_w13[0].astype(BF16)
    w2_1 = ffn1_w2[0].astype(BF16)
    sink = attn_sink[0]
    lbp = lb_param.astype(F32)

    xp = x_prompt.reshape(nb * seq, d)
    xs = x_sample.reshape(ns * tn, d)

    xp, w13_2, w2_2, w_in_b, w_out_b = _ffn_ln(
        xp, w13_1, w2_1, ln1_g, ln1_b, to_cast=(ffn2_w13[0], ffn2_w2[0], w_in[0], w_out[0]))
    xs = _ffn_ln(xs, w13_1, w2_1, ln1_g, ln1_b)
    mix_params = (sink, w_in_b, w_out_b, lbp, attn_norm_g, hgrn_norm_g, ln2_g, ln2_b)

    xp, sp, kp, vp = _mixer_prompt(xp.reshape(nb, seq, d), *mix_params)
    xs, ss, ks, vs = _mixer_sample(
        xs, state_hgrn[0], cache_win_k[0].reshape(ns, WINDOW, KV_WIDTH),
        cache_win_v[0].reshape(ns, WINDOW, KV_WIDTH), *mix_params)

    xp = _ffn_ln(xp.reshape(nb * seq, d), w13_2, w2_2, ln3_g, ln3_b)
    xs = _ffn_ln(xs, w13_2, w2_2, ln3_g, ln3_b)

    cache_tail = (WINDOW, ATTN_KV_HEADS, HEAD_DIM)
    return (xp.reshape(nb, seq, d), xs.reshape(ns, tn, d),
            sp[None], kp.reshape((DEPTH, nb) + cache_tail), vp.reshape((DEPTH, nb) + cache_tail),
            ss[None], ks.reshape((DEPTH, ns) + cache_tail), vs.reshape((DEPTH, ns) + cache_tail))
```

```python
import functools

import jax
import jax.numpy as jnp
from jax import lax
from jax.experimental import pallas as pl
from jax.experimental.pallas import tpu as pltpu

F32 = jnp.float32
BF16 = jnp.bfloat16

D_MODEL = 1024
DEPTH = 1
ATTN_HEADS = 8
ATTN_KV_HEADS = 2
GROUP = ATTN_HEADS // ATTN_KV_HEADS
HEAD_DIM = 64
ATTN_WIDTH = ATTN_HEADS * HEAD_DIM
KV_WIDTH = ATTN_KV_HEADS * HEAD_DIM
WINDOW = 128
ATTN_SCALE = HEAD_DIM ** -0.5
HGRN_HEADS = 4
HGRN_DK = 128
HGRN_DV = 128
HGRN_KW = HGRN_HEADS * HGRN_DK
HGRN_VW = HGRN_HEADS * HGRN_DV
D_MIX = ATTN_WIDTH + HGRN_VW
D_IN = ATTN_WIDTH + 2 * KV_WIDTH + 2 * HGRN_KW + 2 * HGRN_VW
D_FF = 2816
ALPHA = (2.0 * DEPTH) ** 0.25
LN_EPS = 1e-5
RMS_EPS = 1e-6

OFF_K = ATTN_WIDTH
OFF_V = OFF_K + KV_WIDTH
OFF_HQ = OFF_V + KV_WIDTH
OFF_HF = OFF_HQ + HGRN_KW
OFF_HI = OFF_HF + HGRN_KW
OFF_HG = OFF_HI + HGRN_VW

LANES = 128
SUBLANES = 8
BF16_TILE_ROWS = 2 * SUBLANES
MXU_WIDTH = 256
BLK = 128
PROMPT_BLOCKS = 2
PROMPT_TILES = 2
PROJ_CHUNK = MXU_WIDTH
FFN_TILE = 256
FFN_TILES = 4
FFN_CHUNK = MXU_WIDTH
TOK = SUBLANES
SEQ_PER_STEP = BLK // TOK
V7X_VMEM_BYTES = 64 * 1024 * 1024
F32_BYTES = 4
BF16_BYTES = 2

NT_DIMS = (((1,), (1,)), ((), ()))
TN_DIMS = (((0,), (0,)), ((), ()))


def _vmem_limit(resident_bytes, streamed_bytes, value_bytes):
    need = resident_bytes + 2 * streamed_bytes + value_bytes
    assert need <= V7X_VMEM_BYTES, need
    return need


_DONE = object()


def _round_robin(streams):
    live = list(streams)
    rounds = 0
    while live:
        live = [g for g in live if next(g, _DONE) is not _DONE]
        rounds += 1
        assert rounds < 100000, "a stream waits for something no other stream produces"


def _dot(a, b):
    return jnp.dot(a, b, preferred_element_type=F32)


def _dot_nt(a, b):
    return lax.dot_general(a, b, NT_DIMS, preferred_element_type=F32)


def _dot_tn(a, b):
    return lax.dot_general(a, b, TN_DIMS, preferred_element_type=F32)


def _layer_norm(r, g, b):
    mu = jnp.mean(r, axis=-1, keepdims=True)
    c = r - mu
    var = jnp.mean(c * c, axis=-1, keepdims=True)
    return c * lax.rsqrt(var + LN_EPS) * g + b


def _rms_norm(x, g):
    return x * lax.rsqrt(jnp.mean(x * x, axis=-1, keepdims=True) + RMS_EPS) * g


def _ffn_steps(x, w13_ref, w2_ref, g, b, store, on_hidden=lambda: None):
    xb = x.astype(BF16)
    acts = []
    for c in range(D_FF // FFN_CHUNK):
        gate = _dot(xb, w13_ref[:, FFN_CHUNK * c:FFN_CHUNK * (c + 1)])
        up = _dot(xb, w13_ref[:, D_FF + FFN_CHUNK * c:D_FF + FFN_CHUNK * (c + 1)])
        acts.append((jax.nn.silu(gate) * up).astype(BF16))
        yield
    act = jnp.concatenate(acts, axis=1)
    on_hidden()
    ys = []
    for c in range(D_MODEL // FFN_CHUNK):
        ys.append(_dot(act, w2_ref[:, FFN_CHUNK * c:FFN_CHUNK * (c + 1)]))
        yield
    store(_layer_norm(ALPHA * x + 0.5 * jnp.concatenate(ys, axis=1), g, b))


def _ffn_stream(x_ref, o_ref, tile, w13_ref, w2_ref, g, b, flags, may_start):
    while not may_start():
        yield
    rows = slice(FFN_TILE * tile, FFN_TILE * (tile + 1))

    def store(y):
        o_ref[rows, :] = y

    def on_hidden():
        flags[tile] = True

    yield from _ffn_steps(x_ref[rows, :], w13_ref, w2_ref, g, b, store, on_hidden)


def _cast_stream(src_refs, dst_refs):
    for src, dst in zip(src_refs, dst_refs):
        dst[...] = src[...].astype(BF16)
        yield


def _ffn_ln_kernel(n_cast, x_ref, w13_ref, w2_ref, g_ref, b_ref, *refs):
    cast_src, o_ref, cast_dst = refs[:n_cast], refs[n_cast], refs[n_cast + 1:]
    n_tiles = x_ref.shape[0] // FFN_TILE
    g, b = g_ref[...], b_ref[...]
    flags = [False] * n_tiles
    streams = [_ffn_stream(x_ref, o_ref, t, w13_ref, w2_ref, g, b, flags,
                           (lambda: True) if t == 0 else (lambda t=t: flags[t - 1]))
               for t in range(n_tiles)]
    _round_robin(streams + [_cast_stream(cast_src, cast_dst)])


def _const_spec(shape):
    return pl.BlockSpec(shape, lambda *_: (0,) * len(shape), pipeline_mode=pl.Buffered(1))


def _slab_spec(shape, steps):
    rows, cols = shape
    count = max(c for c in range(1, steps + 1)
                if rows % c == 0 and (rows // c) % BF16_TILE_ROWS == 0)
    return pl.BlockSpec((rows // count, cols), lambda i: (jnp.minimum(i, count - 1), 0))


def _ffn_ln(x, w13, w2, g, b, to_cast=()):
    n = x.shape[0]
    tm = FFN_TILE * FFN_TILES
    assert n % tm == 0
    steps = n // tm
    slabs = [_slab_spec(w.shape, steps) for w in to_cast]
    weights = (D_MODEL * 2 * D_FF + D_FF * D_MODEL) * BF16_BYTES
    blocks = (2 * tm * D_MODEL * F32_BYTES
              + sum(s.block_shape[0] * s.block_shape[1] for s in slabs) * (F32_BYTES + BF16_BYTES))
    values = 2 * FFN_TILE * ((D_MODEL + D_FF) * BF16_BYTES
                             + (2 * FFN_CHUNK + 2 * D_MODEL) * F32_BYTES)
    vmem = _vmem_limit(weights, blocks, values)
    out = pl.pallas_call(
        functools.partial(_ffn_ln_kernel, len(to_cast)),
        grid=(steps,),
        in_specs=[
            pl.BlockSpec((tm, D_MODEL), lambda i: (i, 0)),
            _const_spec((D_MODEL, 2 * D_FF)),
            _const_spec((D_FF, D_MODEL)),
            _const_spec((1, D_MODEL)),
            _const_spec((1, D_MODEL)),
        ] + slabs,
        out_specs=[pl.BlockSpec((tm, D_MODEL), lambda i: (i, 0))] + slabs,
        out_shape=[jax.ShapeDtypeStruct((n, D_MODEL), F32)]
        + [jax.ShapeDtypeStruct(w.shape, BF16) for w in to_cast],
        compiler_params=pltpu.CompilerParams(
            dimension_semantics=("arbitrary",), vmem_limit_bytes=vmem),
        name="ffn_ln",
    )(x, w13, w2, g, b, *to_cast)
    return tuple(out) if to_cast else out[0]


def _lower_bound(lbp):
    mx = jnp.max(lbp, axis=0, keepdims=True)
    e = jnp.exp(lbp - mx)
    return e[0:1] / jnp.sum(e, axis=0, keepdims=True)


def _split3(x):
    hi = x.astype(BF16)
    r = x - hi.astype(F32)
    mid = r.astype(BF16)
    lo = (r - mid.astype(F32)).astype(BF16)
    return hi, mid, lo


def _block_cumsum(tri, g):
    w = g.shape[1]
    parts = jnp.concatenate(_split3(g), axis=1)
    s = _dot(tri, parts)
    return s[:, :w] + s[:, w:2 * w] + s[:, 2 * w:]


def _tri_matrix(group):
    r = lax.broadcasted_iota(jnp.int32, (BLK, BLK), 0)
    c = lax.broadcasted_iota(jnp.int32, (BLK, BLK), 1)
    keep = (c <= r) & ((r ^ c) < group)
    return jnp.where(keep, 1.0, 0.0).astype(BF16)


def _pair_ref_rows(b, m):
    t, w = b.shape
    if m >= SUBLANES:
        b3 = b.reshape(t // (2 * m), 2 * m, w)
        return jnp.broadcast_to(b3[:, m - 1:m, :], b3.shape).reshape(t, w)
    b3 = b.reshape(t // SUBLANES, SUBLANES, w)
    sub = lax.broadcasted_iota(jnp.int32, b3.shape, 1)

    def row(i):
        return jnp.broadcast_to(b3[:, i:i + 1, :], b3.shape)

    if m == 4:
        out = row(3)
    elif m == 2:
        out = jnp.where(sub < 4, row(1), row(5))
    else:
        out = jnp.where(sub < 4, jnp.where(sub < 2, row(0), row(2)),
                        jnp.where(sub < 6, row(4), row(6)))
    return out.reshape(t, w)


def _level_operands(q, k, b, m):
    t, w = q.shape
    ref = _pair_ref_rows(b, m)
    if m >= SUBLANES:
        zero = jnp.zeros((m, w), F32)
        qs, ks = [], []
        for blk in range(t // m):
            rows = slice(blk * m, (blk + 1) * m)
            if blk % 2:
                qs.append(q[rows] * jnp.exp2(b[rows] - ref[rows]))
                ks.append(zero)
            else:
                qs.append(zero)
                ks.append(k[rows] * jnp.exp2(ref[rows] - b[rows]))
        return jnp.concatenate(qs, axis=0).astype(BF16), jnp.concatenate(ks, axis=0).astype(BF16)
    upper = (lax.broadcasted_iota(jnp.int32, q.shape, 0) & m) != 0
    e = jnp.exp2((b - ref) * jnp.where(upper, 1.0, -1.0))
    return (jnp.where(upper, q * e, 0.0).astype(BF16), jnp.where(upper, 0.0, k * e).astype(BF16))


def _intra_steps(q, k, b, max_m, xor, out):
    a = jnp.where(xor == 0, jnp.sum(q * k, axis=-1, keepdims=True), 0.0)
    yield
    m = 1
    while m <= max_m:
        qs, ks = _level_operands(q, k, b, m)
        yield
        p = _dot_nt(qs, ks)
        yield
        a = jnp.where(xor >= m, p, a)
        m *= 2
    if 2 * max_m < q.shape[0]:
        a = jnp.where(xor < 2 * max_m, a, 0.0)
    out["a"] = a


def _intra_scores(q, k, b, max_m, xor):
    out = {}
    for _ in _intra_steps(q, k, b, max_m, xor, out):
        pass
    return out["a"]


def _hgrn_gates(proj, lb):
    q = jax.nn.silu(proj[:, OFF_HQ:OFF_HF])
    f = lb + (1.0 - lb) * jax.nn.sigmoid(proj[:, OFF_HF:OFF_HI])
    return q, jnp.log2(f), 1.0 - f, proj[:, OFF_HI:OFF_HG]


def _head_queries(qa):
    low = lax.broadcasted_iota(jnp.int32, (qa.shape[0], LANES), 1) < HEAD_DIM
    out = []
    for pair in range(ATTN_HEADS // 2):
        piece = qa[:, LANES * pair:LANES * (pair + 1)] * ATTN_SCALE
        rolled = pltpu.roll(piece, HEAD_DIM, axis=1)
        for half in range(2):
            kv = (2 * pair + half) // GROUP
            src = piece if half == kv else rolled
            z = jnp.where(low, src, 0.0) if kv == 0 else jnp.where(low, 0.0, src)
            out.append(z.astype(BF16))
    return out


def _merge_heads(o):
    low = lax.broadcasted_iota(jnp.int32, o[0].shape, 1) < HEAD_DIM
    cols = []
    for pair in range(ATTN_HEADS // 2):
        kv = (2 * pair) // GROUP
        even, odd = o[2 * pair], o[2 * pair + 1]
        if kv == 0:
            odd = pltpu.roll(odd, HEAD_DIM, axis=1)
        else:
            even = pltpu.roll(even, HEAD_DIM, axis=1)
        cols.append(jnp.where(low, even, odd))
    return jnp.concatenate(cols, axis=1)


def _slope(head):
    return 2.0 ** (-8.0 * (head + 1) / ATTN_HEADS)


def _alibi_bias(limit):
    trow = lax.broadcasted_iota(jnp.int32, limit.shape, 0)
    ccol = lax.broadcasted_iota(jnp.int32, limit.shape, 1)
    dist = trow + WINDOW - ccol
    visible = dist.astype(jnp.uint32) < limit.astype(jnp.uint32)
    distf = dist.astype(F32)
    return [jnp.where(visible, -_slope(head) * distf, -jnp.inf) for head in range(ATTN_HEADS)]


def _attend_steps(qz, k_all, v_all, sink_ref, bias, out):
    tq = qz[0].shape[0]
    s_all = _dot_nt(jnp.concatenate(qz, axis=0), k_all.astype(BF16))
    yield
    ps, dens = [], []
    for head in range(ATTN_HEADS):
        sink = sink_ref[head]
        s = s_all[tq * head:tq * (head + 1)] + bias[head]
        mx = jnp.maximum(jnp.max(s, axis=-1, keepdims=True), sink)
        yield
        p = jnp.exp(s - mx)
        dens.append(jnp.sum(p, axis=-1, keepdims=True) + jnp.exp(sink - mx))
        ps.append(p.astype(BF16))
        yield
    o_all = _dot(jnp.concatenate(ps, axis=0), v_all.astype(BF16))
    yield
    out["oa"] = _merge_heads([o_all[tq * h:tq * (h + 1)] / dens[h] for h in range(ATTN_HEADS)])


def _attend_sample(qz, k_new, v_new, ck_ref, cv_ref, sink_ref):
    rows = qz[0].shape[0]
    k_new = k_new.astype(BF16)
    v_new = v_new.astype(BF16)
    sn_all = _dot_nt(jnp.concatenate(qz, axis=0), k_new)
    sc_seq = []
    for i in range(SEQ_PER_STEP):
        lhs = jnp.concatenate([z[TOK * i:TOK * (i + 1)] for z in qz], axis=0)
        sc_seq.append(_dot_nt(lhs, ck_ref[i].astype(BF16)))

    r = lax.broadcasted_iota(jnp.int32, (rows, rows), 0)
    c = lax.broadcasted_iota(jnp.int32, (rows, rows), 1)
    new_visible = ((r ^ c) < TOK) & (c <= r)
    new_dist = (r - c).astype(F32)
    tok = lax.broadcasted_iota(jnp.int32, (rows, WINDOW), 0) & (TOK - 1)
    key = lax.broadcasted_iota(jnp.int32, (rows, WINDOW), 1)
    old_visible = key > tok
    old_dist = (WINDOW + tok - key).astype(F32)

    pn, pc, dens = [], [], []
    for head in range(ATTN_HEADS):
        sink = sink_ref[head]
        slope = _slope(head)
        sn = sn_all[rows * head:rows * (head + 1)]
        sc = jnp.concatenate([s[TOK * head:TOK * (head + 1)] for s in sc_seq], axis=0)
        sn = jnp.where(new_visible, sn - slope * new_dist, -jnp.inf)
        sc = jnp.where(old_visible, sc - slope * old_dist, -jnp.inf)
        mx = jnp.maximum(jnp.maximum(jnp.max(sn, axis=-1, keepdims=True),
                                     jnp.max(sc, axis=-1, keepdims=True)), sink)
        en = jnp.exp(sn - mx)
        ec = jnp.exp(sc - mx)
        dens.append(jnp.sum(en, axis=-1, keepdims=True) + jnp.sum(ec, axis=-1, keepdims=True)
                    + jnp.exp(sink - mx))
        pn.append(en.astype(BF16))
        pc.append(ec.astype(BF16))

    on_all = _dot(jnp.concatenate(pn, axis=0), v_new)
    oc_seq = []
    for i in range(SEQ_PER_STEP):
        lhs = jnp.concatenate([p[TOK * i:TOK * (i + 1)] for p in pc], axis=0)
        oc_seq.append(_dot(lhs, cv_ref[i].astype(BF16)))
    o = []
    for head in range(ATTN_HEADS):
        oc = jnp.concatenate([s[TOK * head:TOK * (head + 1)] for s in oc_seq], axis=0)
        o.append((on_all[rows * head:rows * (head + 1)] + oc) / dens[head])
    return _merge_heads(o)


def _mix_out(x, oa, oh, proj, w_out_ref, ang_ref, hng_ref, g2_ref, b2_ref):
    oa = _rms_norm(oa, ang_ref[...])
    hng = hng_ref[...]
    gate = jax.nn.silu(proj[:, OFF_HG:])
    ohn = [_rms_norm(oh[h], hng[:, HGRN_DV * h:HGRN_DV * (h + 1)]) for h in range(HGRN_HEADS)]
    mix = jnp.concatenate([oa, jnp.concatenate(ohn, axis=1) * gate], axis=1).astype(BF16)
    y = _dot(mix, w_out_ref[...])
    return _layer_norm(ALPHA * x + y, g2_ref[...], b2_ref[...])


def _proj_stream(ctx, x, w_in_ref, may_start):
    while not may_start():
        yield
    xb = x.astype(BF16)
    chunks = []
    for c in range(D_IN // PROJ_CHUNK):
        chunks.append(_dot(xb, w_in_ref[:, PROJ_CHUNK * c:PROJ_CHUNK * (c + 1)]))
        yield
    ctx["proj"] = jnp.concatenate(chunks, axis=1)


def _prepare_stream(ctx, lb, tri):
    while "proj" not in ctx:
        yield
    proj = ctx["proj"]
    ctx["qz"] = _head_queries(proj[:, :ATTN_WIDTH])
    yield
    ctx["q"], logf, ctx["kin"], ctx["vin"] = _hgrn_gates(proj, lb)
    yield
    ctx["b"] = [_block_cumsum(tri, logf[BLK * blk:BLK * (blk + 1)]) for blk in range(PROMPT_BLOCKS)]
    ctx["ready"] = True


def _attention_stream(ctx, blk, prev_kv, bias, sink_ref):
    while "ready" not in ctx:
        yield
    rows = slice(BLK * blk, BLK * (blk + 1))
    k_prev, v_prev = prev_kv()
    k_all = jnp.concatenate([k_prev, ctx["proj"][rows, OFF_K:OFF_V]], axis=0)
    v_all = jnp.concatenate([v_prev, ctx["proj"][rows, OFF_V:OFF_HQ]], axis=0)
    out = {}
    yield from _attend_steps([z[rows] for z in ctx["qz"]], k_all, v_all, sink_ref, bias, out)
    ctx["oa"][blk] = out["oa"]


def _hgrn_stream(ctx, blk, h, state, turn, xor):
    while "ready" not in ctx:
        yield
    rows = slice(BLK * blk, BLK * (blk + 1))
    sl = slice(HGRN_DK * h, HGRN_DK * (h + 1))
    qh, kh, bh = ctx["q"][rows, sl], ctx["kin"][rows, sl], ctx["b"][blk][:, sl]
    vh = ctx["vin"][rows, sl].astype(BF16)
    scores = {}
    yield from _intra_steps(qh, kh, bh, BLK // 2, xor, scores)
    o_intra = _dot(scores["a"].astype(BF16), vh)
    yield
    b_last = bh[BLK - 1:BLK, :]
    qe = (qh * jnp.exp2(bh)).astype(BF16)
    kd = (kh * jnp.exp2(b_last - bh)).astype(BF16)
    yield
    while state["turn"][h] != turn:
        yield
    st = state["st"][h]
    o_inter = _dot_nt(qe, st.astype(BF16))
    update = _dot_tn(vh, kd)
    yield
    state["st"][h] = st * jnp.exp2(b_last) + update
    state["turn"][h] = turn + 1
    ctx["oh"][blk][h] = o_intra + o_inter


def _mix_out_stream(ctx, x, y_store, w_out_ref, ang_ref, hng_ref, g2_ref, b2_ref):
    while any(o is None for o in ctx["oa"]) or any(o is None for blk in ctx["oh"] for o in blk):
        yield
    oa = jnp.concatenate(ctx["oa"], axis=0)
    oh = [jnp.concatenate([ctx["oh"][blk][h] for blk in range(PROMPT_BLOCKS)], axis=0)
          for h in range(HGRN_HEADS)]
    y_store(_mix_out(x, oa, oh, ctx["proj"], w_out_ref, ang_ref, hng_ref, g2_ref, b2_ref))


def _mixer_prompt_kernel(sink_ref, x_ref, w_in_ref, w_out_ref, lbp_ref, ang_ref, hng_ref, g2_ref,
                         b2_ref, y_ref, s_out_ref, k_out_ref, v_out_ref, st_ref, kv_ref):
    j = pl.program_id(1)

    @pl.when(j == 0)
    def _():
        st_ref[...] = jnp.zeros_like(st_ref)
        kv_ref[...] = jnp.zeros_like(kv_ref)

    lb = _lower_bound(lbp_ref[...])
    tri = _tri_matrix(BLK)
    xor = (lax.broadcasted_iota(jnp.int32, (BLK, BLK), 0)
           ^ lax.broadcasted_iota(jnp.int32, (BLK, BLK), 1))
    trow = lax.broadcasted_iota(jnp.int32, (BLK, 2 * WINDOW), 0)
    full_bias = _alibi_bias(jnp.full(trow.shape, WINDOW, jnp.int32))

    state = {"st": [st_ref[h] for h in range(HGRN_HEADS)], "turn": [0] * HGRN_HEADS}
    ctxs = [{"oa": [None] * PROMPT_BLOCKS,
             "oh": [[None] * HGRN_HEADS for _ in range(PROMPT_BLOCKS)]} for _ in range(PROMPT_TILES)]

    def kv_block(tile, blk):
        rows = slice(BLK * blk, BLK * (blk + 1))
        proj = ctxs[tile]["proj"]
        return proj[rows, OFF_K:OFF_V], proj[rows, OFF_V:OFF_HQ]

    streams = []
    for tile in range(PROMPT_TILES):
        ctx = ctxs[tile]
        x = x_ref[tile]
        may_start = (lambda: True) if tile == 0 else (lambda prev=ctxs[tile - 1]: "ready" in prev)
        streams.append(_proj_stream(ctx, x, w_in_ref, may_start))
        streams.append(_prepare_stream(ctx, lb, tri))
        for blk in range(PROMPT_BLOCKS):
            if tile == 0 and blk == 0:
                prev_kv = lambda: (kv_ref[0], kv_ref[1])
                bias = _alibi_bias(jnp.where(j > 0, WINDOW, trow + 1))
            else:
                g = tile * PROMPT_BLOCKS + blk - 1
                prev_kv = lambda g=g: kv_block(g // PROMPT_BLOCKS, g % PROMPT_BLOCKS)
                bias = full_bias
            streams.append(_attention_stream(ctx, blk, prev_kv, bias, sink_ref))
            for h in range(HGRN_HEADS):
                streams.append(_hgrn_stream(ctx, blk, h, state, tile * PROMPT_BLOCKS + blk, xor))

        def y_store(y, tile=tile):
            y_ref[tile] = y

        streams.append(_mix_out_stream(ctx, x, y_store, w_out_ref, ang_ref, hng_ref, g2_ref, b2_ref))
    _round_robin(streams)

    k_last, v_last = kv_block(PROMPT_TILES - 1, PROMPT_BLOCKS - 1)
    kv_ref[0] = k_last
    kv_ref[1] = v_last
    for h in range(HGRN_HEADS):
        st_ref[h] = state["st"][h]

    @pl.when(j == pl.num_programs(1) - 1)
    def _():
        k_out_ref[0] = k_last
        v_out_ref[0] = v_last
        for h in range(HGRN_HEADS):
            s_out_ref[0, h] = state["st"][h].T


MIXER_WEIGHT_BYTES = (D_MODEL * D_IN + D_MIX * D_MODEL) * BF16_BYTES
STATE_BYTES = HGRN_HEADS * HGRN_DK * HGRN_DV * F32_BYTES
WINDOW_BYTES = WINDOW * KV_WIDTH * F32_BYTES


def _mixer_value_bytes(rows, keys):
    per_row = ((D_IN + 4 * HGRN_KW + 2 * D_MODEL) * F32_BYTES
               + (ATTN_HEADS * LANES + D_MIX) * BF16_BYTES)
    scores = ATTN_HEADS * BLK * keys * (F32_BYTES + BF16_BYTES)
    return rows * per_row + (rows // BLK) * scores


def _mixer_prompt(x, sink, w_in, w_out, lbp, ang, hng, g2, b2):
    nb, seq, _ = x.shape
    tile_rows = PROMPT_BLOCKS * BLK
    steps_per_seq = seq // (PROMPT_TILES * tile_rows)
    n_tiles = nb * seq // tile_rows
    xt = x.reshape(n_tiles, tile_rows, D_MODEL)
    blocks = (2 * PROMPT_TILES * tile_rows * D_MODEL * F32_BYTES
              + STATE_BYTES + 2 * WINDOW_BYTES)
    vmem = _vmem_limit(MIXER_WEIGHT_BYTES + STATE_BYTES + 2 * WINDOW_BYTES, blocks,
                       _mixer_value_bytes(2 * tile_rows, 2 * WINDOW))
    tile_spec = pl.BlockSpec((PROMPT_TILES, tile_rows, D_MODEL),
                             lambda b, j: (b * steps_per_seq + j, 0, 0))
    y, s, k, v = pl.pallas_call(
        _mixer_prompt_kernel,
        grid=(nb, steps_per_seq),
        in_specs=[
            pl.BlockSpec(memory_space=pltpu.SMEM),
            tile_spec,
            _const_spec((D_MODEL, D_IN)),
            _const_spec((D_MIX, D_MODEL)),
            _const_spec((DEPTH + 1, HGRN_KW)),
            _const_spec((1, ATTN_WIDTH)),
            _const_spec((1, HGRN_VW)),
            _const_spec((1, D_MODEL)),
            _const_spec((1, D_MODEL)),
        ],
        out_specs=[
            tile_spec,
            pl.BlockSpec((1, HGRN_HEADS, HGRN_DK, HGRN_DV), lambda b, j: (b, 0, 0, 0)),
            pl.BlockSpec((1, WINDOW, KV_WIDTH), lambda b, j: (b, 0, 0)),
            pl.BlockSpec((1, WINDOW, KV_WIDTH), lambda b, j: (b, 0, 0)),
        ],
        out_shape=[
            jax.ShapeDtypeStruct((n_tiles, tile_rows, D_MODEL), F32),
            jax.ShapeDtypeStruct((nb, HGRN_HEADS, HGRN_DK, HGRN_DV), F32),
            jax.ShapeDtypeStruct((nb, WINDOW, KV_WIDTH), F32),
            jax.ShapeDtypeStruct((nb, WINDOW, KV_WIDTH), F32),
        ],
        scratch_shapes=[
            pltpu.VMEM((HGRN_HEADS, HGRN_DV, HGRN_DK), F32),
            pltpu.VMEM((2, WINDOW, KV_WIDTH), F32),
        ],
        compiler_params=pltpu.CompilerParams(
            dimension_semantics=("arbitrary", "arbitrary"), vmem_limit_bytes=vmem),
        name="mixer_prompt",
    )(sink, xt, w_in, w_out, lbp, ang, hng, g2, b2)
    return y.reshape(nb, seq, D_MODEL), s, k, v


def _mixer_sample_kernel(sink_ref, x_ref, s0_ref, ck_ref, cv_ref, w_in_ref, w_out_ref, lbp_ref,
                         ang_ref, hng_ref, g2_ref, b2_ref, y_ref, s_out_ref, k_out_ref, v_out_ref):
    x = x_ref[...]
    proj = _dot(x.astype(BF16), w_in_ref[...])
    k_new = proj[:, OFF_K:OFF_V]
    v_new = proj[:, OFF_V:OFF_HQ]

    oa = _attend_sample(_head_queries(proj[:, :ATTN_WIDTH]), k_new, v_new, ck_ref, cv_ref, sink_ref)
    for i in range(SEQ_PER_STEP):
        rows = slice(TOK * i, TOK * (i + 1))
        k_out_ref[i] = jnp.concatenate([ck_ref[i, TOK:, :], k_new[rows]], axis=0)
        v_out_ref[i] = jnp.concatenate([cv_ref[i, TOK:, :], v_new[rows]], axis=0)

    q, logf, kin, vin = _hgrn_gates(proj, _lower_bound(lbp_ref[...]))
    b = _block_cumsum(_tri_matrix(TOK), logf)
    xor = (lax.broadcasted_iota(jnp.int32, (BLK, BLK), 0)
           ^ lax.broadcasted_iota(jnp.int32, (BLK, BLK), 1))
    lane = lax.broadcasted_iota(jnp.int32, (HGRN_DK, BLK), 1)
    oh = []
    for h in range(HGRN_HEADS):
        sl = slice(HGRN_DK * h, HGRN_DK * (h + 1))
        qh, kh, bh = q[:, sl], kin[:, sl], b[:, sl]
        vh = vin[:, sl].astype(BF16)
        a = _intra_scores(qh, kh, bh, TOK // 2, xor)
        o_intra = _dot(a.astype(BF16), vh)
        qe = (qh * jnp.exp2(bh)).astype(BF16)
        b3 = bh.reshape(SEQ_PER_STEP, TOK, HGRN_DK)
        b_last = jnp.broadcast_to(b3[:, TOK - 1:TOK, :], b3.shape).reshape(BLK, HGRN_DK)
        kd_t = (kh * jnp.exp2(b_last - bh)).T
        dec_t = jnp.exp2(b_last).T
        o_rows = []
        for i in range(SEQ_PER_STEP):
            s0 = s0_ref[i, h]
            o_rows.append(_dot(qe[TOK * i:TOK * (i + 1)], s0.astype(BF16)))
            kd_i = jnp.where((lane >= TOK * i) & (lane < TOK * (i + 1)), kd_t, 0.0).astype(BF16)
            s_out_ref[i, h] = s0 * dec_t[:, TOK * i:TOK * i + 1] + _dot(kd_i, vh)
        oh.append(o_intra + jnp.concatenate(o_rows, axis=0))

    y_ref[...] = _mix_out(x, oa, oh, proj, w_out_ref, ang_ref, hng_ref, g2_ref, b2_ref)


def _mixer_sample(x, s0, ck, cv, sink, w_in, w_out, lbp, ang, hng, g2, b2):
    n = x.shape[0]
    nseq = s0.shape[0]
    state_spec = pl.BlockSpec((SEQ_PER_STEP, HGRN_HEADS, HGRN_DK, HGRN_DV), lambda i: (i, 0, 0, 0))
    cache_spec = pl.BlockSpec((SEQ_PER_STEP, WINDOW, KV_WIDTH), lambda i: (i, 0, 0))
    blocks = 2 * BLK * D_MODEL * F32_BYTES + 2 * SEQ_PER_STEP * (STATE_BYTES + 2 * WINDOW_BYTES)
    values = (_mixer_value_bytes(BLK, WINDOW + BLK)
              + HGRN_HEADS * 2 * HGRN_DK * BLK * F32_BYTES)
    vmem = _vmem_limit(MIXER_WEIGHT_BYTES, blocks, values)
    return pl.pallas_call(
        _mixer_sample_kernel,
        grid=(n // BLK,),
        in_specs=[
            pl.BlockSpec(memory_space=pltpu.SMEM),
            pl.BlockSpec((BLK, D_MODEL), lambda i: (i, 0)),
            state_spec, cache_spec, cache_spec,
            _const_spec((D_MODEL, D_IN)),
            _const_spec((D_MIX, D_MODEL)),
            _const_spec((DEPTH + 1, HGRN_KW)),
            _const_spec((1, ATTN_WIDTH)),
            _const_spec((1, HGRN_VW)),
            _const_spec((1, D_MODEL)),
            _const_spec((1, D_MODEL)),
        ],
        out_specs=[
            pl.BlockSpec((BLK, D_MODEL), lambda i: (i, 0)),
            state_spec, cache_spec, cache_spec,
        ],
        out_shape=[
            jax.ShapeDtypeStruct((n, D_MODEL), F32),
            jax.ShapeDtypeStruct((nseq, HGRN_HEADS, HGRN_DK, HGRN_DV), F32),
            jax.ShapeDtypeStruct((nseq, WINDOW, KV_WIDTH), F32),
            jax.ShapeDtypeStruct((nseq, WINDOW, KV_WIDTH), F32),
        ],
        compiler_params=pltpu.CompilerParams(
            dimension_semantics=("arbitrary",), vmem_limit_bytes=vmem),
        name="mixer_sample",
    )(sink, x, s0, ck, cv, w_in, w_out, lbp, ang, hng, g2, b2)


def kernel(x_prompt, x_sample, state_hgrn, cache_win_k, cache_win_v, ln1_g, ln1_b, ffn1_w13,
           ffn1_w2, w_in, lb_param, attn_sink, attn_norm_g, hgrn_norm_g, w_out, ln2_g, ln2_b,
           ffn2_w13, ffn2_w2, ln3_g, ln3_b):
    nb, seq, d = x_prompt.shape
    ns, tn, _ = x_sample.shape
    assert state_hgrn.shape[0] == DEPTH == 1 and d == D_MODEL
    assert seq % (PROMPT_TILES * PROMPT_BLOCKS * BLK) == 0 and tn == TOK and (ns * tn) % BLK == 0
    assert cache_win_k.shape[2] == WINDOW

    w13_1 = ffn1_w13[0].astype(BF16)
    w2_1 = ffn1_w2[0].astype(BF16)
    sink = attn_sink[0]
    lbp = lb_param.astype(F32)

    xp = x_prompt.reshape(nb * seq, d)
    xs = x_sample.reshape(ns * tn, d)

    xp, w13_2, w2_2, w_in_b, w_out_b = _ffn_ln(
        xp, w13_1, w2_1, ln1_g, ln1_b, to_cast=(ffn2_w13[0], ffn2_w2[0], w_in[0], w_out[0]))
    mix_params = (sink, w_in_b, w_out_b, lbp, attn_norm_g, hgrn_norm_g, ln2_g, ln2_b)
    xp, sp, kp, vp = _mixer_prompt(xp.reshape(nb, seq, d), *mix_params)
    xp = _ffn_ln(xp.reshape(nb * seq, d), w13_2, w2_2, ln3_g, ln3_b)

    xs = _ffn_ln(xs, w13_1, w2_1, ln1_g, ln1_b)
    xs, ss, ks, vs = _mixer_sample(
        xs, state_hgrn[0], cache_win_k[0].reshape(ns, WINDOW, KV_WIDTH),
        cache_win_v[0].reshape(ns, WINDOW, KV_WIDTH), *mix_params)
    xs = _ffn_ln(xs, w13_2, w2_2, ln3_g, ln3_b)

    cache_tail = (WINDOW, ATTN_KV_HEADS, HEAD_DIM)
    return (xp.reshape(nb, seq, d), xs.reshape(ns, tn, d),
            sp[None], kp.reshape((DEPTH, nb) + cache_tail), vp.reshape((DEPTH, nb) + cache_tail),
            ss[None], ks.reshape((DEPTH, ns) + cache_tail), vs.reshape((DEPTH, ns) + cache_tail))
```
